```python
import jax, jax.numpy as jnp
from jax import lax
import numpy as np

D_MODEL = 1024
BATCH = 4
SEQ = 4096
DEPTH = 1

HEAD_DIM = 64
N_Q_HEADS = D_MODEL // HEAD_DIM
N_KV_HEADS = N_Q_HEADS // 4
D_CONV = D_MODEL
CONV_WIDTH = 3
WINDOW = 128
BLOCK = 128
D_FF = ((8 * D_MODEL // 3 + 127) // 128) * 128
EPS = 1e-6
SPLITS = [D_CONV, D_CONV, D_CONV,
          N_Q_HEADS * HEAD_DIM, N_KV_HEADS * HEAD_DIM, N_KV_HEADS * HEAD_DIM,
          D_MODEL, D_MODEL]
D_IN = sum(SPLITS)

kernel_name = "hybrid_gated_conv_swa_encoder_block"


def rms_norm(x, gain):
    xf = x.astype(jnp.float32)
    y = xf * lax.rsqrt(jnp.mean(xf * xf, axis=-1, keepdims=True) + EPS)
    return (y * gain.astype(jnp.float32)).astype(x.dtype)


def dwconv3(x, w):
    ch = x.shape[-1]
    pad = (CONV_WIDTH - 1) // 2
    return lax.conv_general_dilated(
        x, w.astype(x.dtype)[:, None, :], window_strides=(1,),
        padding=((pad, pad),), dimension_numbers=("NWC", "WIO", "NWC"),
        feature_group_count=ch)


def alibi_slopes(n_heads):
    return jnp.asarray(2.0 ** (-8.0 * np.arange(1, n_heads + 1) / n_heads), dtype=jnp.float32)


def windowed_gqa(q, k, v, sink):
    b, s, _, hd = q.shape
    g = N_Q_HEADS // N_KV_HEADS
    nb = s // BLOCK
    qb = q.reshape(b, nb, BLOCK, N_KV_HEADS, g, hd)

    def band(t):
        tp = jnp.pad(t, ((0, 0), (BLOCK, BLOCK), (0, 0), (0, 0)))
        tp = tp.reshape(b, nb + 2, BLOCK, N_KV_HEADS, hd)
        return jnp.concatenate([tp[:, :-2], tp[:, 1:-1], tp[:, 2:]], axis=2)

    kb, vb = band(k), band(v)
    scores = jnp.einsum('bnqkgd,bnjkd->bkgnqj', qb, kb).astype(jnp.float32)
    scores = scores * (1.0 / np.sqrt(hd)).astype(np.float32)

    blk = jnp.arange(nb)[:, None]
    qpos = blk * BLOCK + jnp.arange(BLOCK)[None, :]
    kpos = (blk - 1) * BLOCK + jnp.arange(3 * BLOCK)[None, :]
    dist = jnp.abs(qpos[:, :, None] - kpos[:, None, :])
    valid = (dist <= WINDOW) & (kpos[:, None, :] >= 0) & (kpos[:, None, :] < s)
    slopes = alibi_slopes(N_Q_HEADS).reshape(N_KV_HEADS, g, 1, 1, 1)
    bias = -slopes * dist.astype(jnp.float32)[None, None]
    scores = jnp.where(valid[None, None], scores + bias[None], -1e30)

    sink_b = sink.astype(jnp.float32).reshape(1, N_KV_HEADS, g, 1, 1, 1)
    m = jnp.maximum(jnp.max(scores, axis=-1, keepdims=True), sink_b)
    p = jnp.exp(scores - m)
    p = p / (jnp.sum(p, axis=-1, keepdims=True) + jnp.exp(sink_b - m))
    out = jnp.einsum('bkgnqj,bnjkd->bnqkgd', p.astype(v.dtype), vb)
    return out.reshape(b, s, N_Q_HEADS * hd)


def token_mixer(h, w_in, conv_w, q_gain, k_gain, sink, w_o):
    b, s, _ = h.shape
    proj = h @ w_in
    cb, cc, cu, q, k, v, ga, gb = jnp.split(proj, np.cumsum(SPLITS)[:-1].tolist(), axis=-1)
    y_a = cb * dwconv3(cc * cu, conv_w)
    q = rms_norm(q.reshape(b, s, N_Q_HEADS, HEAD_DIM), q_gain)
    k = rms_norm(k.reshape(b, s, N_KV_HEADS, HEAD_DIM), k_gain)
    v = v.reshape(b, s, N_KV_HEADS, HEAD_DIM)
    y_b = windowed_gqa(q, k, v, sink)
    merged = jax.nn.sigmoid(ga) * y_a + jax.nn.sigmoid(gb) * y_b
    return merged @ w_o


def conv_ffn(h, w_up, ffn_conv_w, ffn_conv_b, w_down):
    a, gbr = jnp.split(h @ w_up, 2, axis=-1)
    a = dwconv3(a, ffn_conv_w) + ffn_conv_b
    return (jax.nn.gelu(a) * gbr) @ w_down


def setup_inputs(seed: int = 0) -> dict:
    key = jax.random.key(seed)
    ks = jax.random.split(key, 16)
    nrm = jax.random.normal
    f32 = jnp.float32
    L, D = DEPTH, D_MODEL
    return {
        "x": nrm(ks[0], (BATCH, SEQ, D), f32),
        "c": nrm(ks[1], (BATCH, D), f32),
        "w_ada": nrm(ks[2], (L, D, 6 * D), f32) * (0.5 * D ** -0.5),
        "b_ada": nrm(ks[3], (L, 6 * D), f32) * 0.01,
        "norm1_gain": 1.0 + 0.02 * nrm(ks[4], (L, D), f32),
        "w_in": nrm(ks[5], (L, D, D_IN), f32) * D ** -0.5,
        "conv_w": nrm(ks[6], (L, CONV_WIDTH, D_CONV), f32) * CONV_WIDTH ** -0.5,
        "q_gain": 1.0 + 0.02 * nrm(ks[7], (L, HEAD_DIM), f32),
        "k_gain": 1.0 + 0.02 * nrm(ks[8], (L, HEAD_DIM), f32),
        "sink": 0.5 * nrm(ks[9], (L, N_Q_HEADS), f32),
        "w_o": nrm(ks[10], (L, D, D), f32) * D ** -0.5,
        "norm2_gain": 1.0 + 0.02 * nrm(ks[11], (L, D), f32),
        "w_up": nrm(ks[12], (L, D, 2 * D_FF), f32) * D ** -0.5,
        "ffn_conv_w": nrm(ks[13], (L, CONV_WIDTH, D_FF), f32) * CONV_WIDTH ** -0.5,
        "ffn_conv_b": 0.01 * nrm(ks[14], (L, D_FF), f32),
        "w_down": nrm(ks[15], (L, D_FF, D), f32) * D_FF ** -0.5,
    }


def reference(x, c, w_ada, b_ada, norm1_gain, w_in, conv_w, q_gain, k_gain, sink,
              w_o, norm2_gain, w_up, ffn_conv_w, ffn_conv_b, w_down):
    c_act = jax.nn.silu(c)
    for l in range(DEPTH):
        mod = c_act @ w_ada[l] + b_ada[l]
        sh1, sc1, g1, sh2, sc2, g2 = [t[:, None, :] for t in jnp.split(mod, 6, axis=-1)]
        h = rms_norm(x, norm1_gain[l]) * (1.0 + sc1) + sh1
        x = x + g1 * token_mixer(h, w_in[l], conv_w[l], q_gain[l], k_gain[l], sink[l], w_o[l])
        h = rms_norm(x, norm2_gain[l]) * (1.0 + sc2) + sh2
        x = x + g2 * conv_ffn(h, w_up[l], ffn_conv_w[l], ffn_conv_b[l], w_down[l])
    return x
```

```python
import functools
import math

import jax
import jax.numpy as jnp
import numpy as np
from jax import lax
from jax.experimental import pallas as pl
from jax.experimental.pallas import tpu as pltpu

D_MODEL = 1024
HEAD_DIM = 64
N_Q_HEADS = D_MODEL // HEAD_DIM
N_KV_HEADS = N_Q_HEADS // 4
GROUP = N_Q_HEADS // N_KV_HEADS
D_CONV = D_MODEL
WINDOW = 128
BLOCK = 128
D_FF = ((8 * D_MODEL // 3 + 127) // 128) * 128
EPS = 1e-6
NEG_INF = -1e30

OFF_CB = 0
OFF_CC = OFF_CB + D_CONV
OFF_CU = OFF_CC + D_CONV
OFF_Q = OFF_CU + D_CONV
OFF_K = OFF_Q + N_Q_HEADS * HEAD_DIM
OFF_V = OFF_K + N_KV_HEADS * HEAD_DIM
OFF_GA = OFF_V + N_KV_HEADS * HEAD_DIM
OFF_GB = OFF_GA + D_MODEL
D_IN = OFF_GB + D_MODEL

KV_W = N_KV_HEADS * HEAD_DIM
SUBLANES = 8
SEQ_TILE = 512
FF_CHUNK = 256
VMEM_LIMIT_BYTES = 56 * 1024 * 1024

ALIBI_SLOPES = [2.0 ** (-8.0 * (h + 1) / N_Q_HEADS) for h in range(N_Q_HEADS)]


def _rms_mod(xv, gain, scale, shift):
    ms = jnp.mean(xv * xv, axis=-1, keepdims=True)
    y = xv * lax.rsqrt(ms + EPS) * gain
    return y * (1.0 + scale) + shift


def _shift_rows(ref, start, rows):
    return ref[pl.ds(start, rows), :]


def _ada_kernel(c_ref, w_ref, b_ref, o_ref):
    c = c_ref[...]
    act = c * jax.nn.sigmoid(c)
    o_ref[...] = jnp.dot(act.astype(jnp.bfloat16), w_ref[...].astype(jnp.bfloat16),
                         preferred_element_type=jnp.float32) + b_ref[...]


def _modulation(c, w_ada, b_ada):
    b, d = c.shape
    n = w_ada.shape[1]
    rows = SUBLANES
    c_pad = jnp.zeros((rows, d), c.dtype).at[:b].set(c)
    tn = d
    out = pl.pallas_call(
        _ada_kernel,
        grid=(n // tn,),
        in_specs=[
            pl.BlockSpec((rows, d), lambda j: (0, 0)),
            pl.BlockSpec((d, tn), lambda j: (0, j)),
            pl.BlockSpec((1, tn), lambda j: (0, j)),
        ],
        out_specs=pl.BlockSpec((rows, tn), lambda j: (0, j)),
        out_shape=jax.ShapeDtypeStruct((rows, n), jnp.float32),
        compiler_params=pltpu.CompilerParams(dimension_semantics=("arbitrary",)),
        name="ada_modulation",
    )(c_pad, w_ada, b_ada.reshape(1, n))
    return out[:b]


def _mixer_kernel(sink_ref,
                  x_ref, xp_ref, xn_ref,
                  mod_ref,
                  g1n_ref, win_ref, convw_ref, qg_ref, kg_ref, wo_ref,
                  o_ref,
                  h_ref,
                  k_ref, v_ref,
                  u_ref,
                  q_ref,
                  yb_ref,
                  mrg_ref,
                  bias_ref,
                  *, seq_len):
    t = x_ref.shape[0]
    i = pl.program_id(1)
    n_tiles = pl.num_programs(1)
    first = jnp.logical_and(pl.program_id(0) == 0, i == 0)

    @pl.when(first)
    def _():
        qi = lax.broadcasted_iota(jnp.int32, (BLOCK, 3 * BLOCK), 0)
        kj = lax.broadcasted_iota(jnp.int32, (BLOCK, 3 * BLOCK), 1)
        dist = jnp.abs(qi + BLOCK - kj)
        distf = dist.astype(jnp.float32)
        for h in range(N_Q_HEADS):
            bias_ref[h] = jnp.where(dist <= WINDOW, -ALIBI_SLOPES[h] * distf, NEG_INF)

    shift1 = mod_ref[0:1, :]
    scale1 = mod_ref[1:2, :]
    gate1 = mod_ref[2:3, :]
    gain = g1n_ref[...]

    h_prev = _rms_mod(xp_ref[...], gain, scale1, shift1)
    h_next = _rms_mod(xn_ref[...], gain, scale1, shift1)
    h_ref[0:BLOCK, :] = jnp.where(i > 0, h_prev, 0.0).astype(jnp.bfloat16)
    h_ref[BLOCK:BLOCK + t, :] = _rms_mod(x_ref[...], gain, scale1, shift1).astype(jnp.bfloat16)
    h_ref[BLOCK + t:, :] = jnp.where(i < n_tiles - 1, h_next, 0.0).astype(jnp.bfloat16)

    kv = jnp.dot(h_ref[...], win_ref[:, OFF_K:OFF_K + 2 * KV_W], preferred_element_type=jnp.float32)
    kgain = kg_ref[...]
    for g in range(N_KV_HEADS):
        kh = kv[:, g * HEAD_DIM:(g + 1) * HEAD_DIM]
        r = lax.rsqrt(jnp.mean(kh * kh, axis=-1, keepdims=True) + EPS)
        k_ref[:, g * HEAD_DIM:(g + 1) * HEAD_DIM] = (kh * r * kgain).astype(jnp.bfloat16)
    v_ref[...] = kv[:, KV_W:].astype(jnp.bfloat16)

    lo = BLOCK - SUBLANES
    hc = h_ref[lo:lo + t + 2 * SUBLANES, :]
    cc = jnp.dot(hc, win_ref[:, OFF_CC:OFF_CC + D_CONV], preferred_element_type=jnp.float32)
    cu = jnp.dot(hc, win_ref[:, OFF_CU:OFF_CU + D_CONV], preferred_element_type=jnp.float32)
    u_ref[...] = cc * cu
    conv = (convw_ref[0:1, :] * _shift_rows(u_ref, SUBLANES - 1, t)
            + convw_ref[1:2, :] * _shift_rows(u_ref, SUBLANES, t)
            + convw_ref[2:3, :] * _shift_rows(u_ref, SUBLANES + 1, t))
    ht = h_ref[BLOCK:BLOCK + t, :]
    cb = jnp.dot(ht, win_ref[:, OFF_CB:OFF_CB + D_CONV], preferred_element_type=jnp.float32)
    ga = jnp.dot(ht, win_ref[:, OFF_GA:OFF_GA + D_MODEL], preferred_element_type=jnp.float32)
    q_ref[...] = jnp.dot(ht, win_ref[:, OFF_Q:OFF_Q + D_MODEL], preferred_element_type=jnp.float32)
    ya = jax.nn.sigmoid(ga) * (cb * conv)

    qgain = qg_ref[...] * (1.0 / math.sqrt(HEAD_DIM))
    kcol = lax.broadcasted_iota(jnp.int32, (1, 3 * BLOCK), 1)
    tile_start = i * t

    def q_block(qb, carry):
        r0 = pl.multiple_of(qb * BLOCK, BLOCK)
        kpos = kcol + (tile_start + r0 - BLOCK)
        kvalid = jnp.logical_and(kpos >= 0, kpos < seq_len)
        for g in range(N_KV_HEADS):
            kb = k_ref[pl.ds(r0, 3 * BLOCK), g * HEAD_DIM:(g + 1) * HEAD_DIM]
            vb = v_ref[pl.ds(r0, 3 * BLOCK), g * HEAD_DIM:(g + 1) * HEAD_DIM]
            qs = []
            for j in range(GROUP):
                hh = g * GROUP + j
                qh = q_ref[pl.ds(r0, BLOCK), hh * HEAD_DIM:(hh + 1) * HEAD_DIM]
                r = lax.rsqrt(jnp.mean(qh * qh, axis=-1, keepdims=True) + EPS)
                qs.append((qh * r * qgain).astype(jnp.bfloat16))
            q4 = jnp.concatenate(qs, axis=0)
            s4 = lax.dot_general(q4, kb, (((1,), (1,)), ((), ())),
                                 preferred_element_type=jnp.float32)
            ps = []
            dens = []
            for j in range(GROUP):
                hh = g * GROUP + j
                s = s4[j * BLOCK:(j + 1) * BLOCK, :] + bias_ref[hh]
                s = jnp.where(kvalid, s, NEG_INF)
                sink = sink_ref[hh]
                m = jnp.maximum(jnp.max(s, axis=-1, keepdims=True), sink)
                p = jnp.exp(s - m)
                dens.append(jnp.sum(p, axis=-1, keepdims=True) + jnp.exp(sink - m))
                ps.append(p.astype(jnp.bfloat16))
            p4 = jnp.concatenate(ps, axis=0)
            o4 = jnp.dot(p4, vb, preferred_element_type=jnp.float32)
            for j in range(GROUP):
                hh = g * GROUP + j
                yb_ref[pl.ds(r0, BLOCK), hh * HEAD_DIM:(hh + 1) * HEAD_DIM] = (
                    o4[j * BLOCK:(j + 1) * BLOCK, :] / dens[j])
        return carry

    lax.fori_loop(0, t // BLOCK, q_block, 0)

    gb = jnp.dot(ht, win_ref[:, OFF_GB:OFF_GB + D_MODEL], preferred_element_type=jnp.float32)
    mrg_ref[...] = (ya + jax.nn.sigmoid(gb) * yb_ref[...]).astype(jnp.bfloat16)
    out = jnp.dot(mrg_ref[...], wo_ref[...], preferred_element_type=jnp.float32)
    o_ref[...] = x_ref[...] + gate1 * out


def _token_mixer(x, mod, norm_gain, w_in, conv_w, q_gain, k_gain, sink, w_o):
    b, s, d = x.shape
    t = SEQ_TILE
    assert s % t == 0 and t % BLOCK == 0
    n_tiles = s // t
    bpt = t // BLOCK
    n_blocks = s // BLOCK

    const = lambda shape: pl.BlockSpec(shape, lambda bi, i, *_: (0,) * len(shape),
                                       pipeline_mode=pl.Buffered(1))
    grid_spec = pltpu.PrefetchScalarGridSpec(
        num_scalar_prefetch=1,
        grid=(b, n_tiles),
        in_specs=[
            pl.BlockSpec((None, t, d), lambda bi, i, *_: (bi, i, 0)),
            pl.BlockSpec((None, BLOCK, d), lambda bi, i, *_: (bi, jnp.maximum(i * bpt - 1, 0), 0)),
            pl.BlockSpec((None, BLOCK, d),
                         lambda bi, i, *_: (bi, jnp.minimum((i + 1) * bpt, n_blocks - 1), 0)),
            pl.BlockSpec((None, 6, d), lambda bi, i, *_: (bi, 0, 0)),
            const((1, d)),
            const((d, D_IN)),
            const((3, D_CONV)),
            const((1, HEAD_DIM)),
            const((1, HEAD_DIM)),
            const((d, d)),
        ],
        out_specs=pl.BlockSpec((None, t, d), lambda bi, i, *_: (bi, i, 0)),
        scratch_shapes=[
            pltpu.VMEM((t + 2 * BLOCK, d), jnp.bfloat16),
            pltpu.VMEM((t + 2 * BLOCK, KV_W), jnp.bfloat16),
            pltpu.VMEM((t + 2 * BLOCK, KV_W), jnp.bfloat16),
            pltpu.VMEM((t + 2 * SUBLANES, D_CONV), jnp.float32),
            pltpu.VMEM((t, d), jnp.float32),
            pltpu.VMEM((t, d), jnp.float32),
            pltpu.VMEM((t, d), jnp.bfloat16),
            pltpu.VMEM((N_Q_HEADS, BLOCK, 3 * BLOCK), jnp.float32),
        ],
    )
    return pl.pallas_call(
        functools.partial(_mixer_kernel, seq_len=s),
        grid_spec=grid_spec,
        out_shape=jax.ShapeDtypeStruct((b, s, d), jnp.float32),
        compiler_params=pltpu.CompilerParams(
            dimension_semantics=("arbitrary", "arbitrary"),
            vmem_limit_bytes=VMEM_LIMIT_BYTES),
        name="token_mixer",
    )(sink, x, x, x, mod, norm_gain.reshape(1, d), w_in.astype(jnp.bfloat16), conv_w,
      q_gain.reshape(1, HEAD_DIM), k_gain.reshape(1, HEAD_DIM), w_o.astype(jnp.bfloat16))


def _ffn_kernel(x_ref, xp_ref, xn_ref,
                mod_ref, g2n_ref, wup_ref, fcw_ref, fcb_ref, wdn_ref,
                o_ref,
                h_ref,
                a_ref,
                acc_ref):
    t = x_ref.shape[0]
    i = pl.program_id(1)
    n_tiles = pl.num_programs(1)
    shift2 = mod_ref[3:4, :]
    scale2 = mod_ref[4:5, :]
    gate2 = mod_ref[5:6, :]
    gain = g2n_ref[...]

    h_prev = _rms_mod(xp_ref[...], gain, scale2, shift2)
    h_next = _rms_mod(xn_ref[...], gain, scale2, shift2)
    h_ref[0:SUBLANES, :] = jnp.where(i > 0, h_prev, 0.0).astype(jnp.bfloat16)
    h_ref[SUBLANES:SUBLANES + t, :] = _rms_mod(x_ref[...], gain, scale2, shift2).astype(jnp.bfloat16)
    h_ref[SUBLANES + t:, :] = jnp.where(i < n_tiles - 1, h_next, 0.0).astype(jnp.bfloat16)

    acc_ref[...] = jnp.zeros_like(acc_ref)

    def chunk(c, carry):
        c0 = pl.multiple_of(c * FF_CHUNK, FF_CHUNK)
        a_ref[...] = jnp.dot(h_ref[...], wup_ref[:, pl.ds(c0, FF_CHUNK)],
                             preferred_element_type=jnp.float32)
        gbr = jnp.dot(h_ref[SUBLANES:SUBLANES + t, :], wup_ref[:, pl.ds(D_FF + c0, FF_CHUNK)],
                      preferred_element_type=jnp.float32)
        w = fcw_ref[:, pl.ds(c0, FF_CHUNK)]
        a = (w[0:1, :] * _shift_rows(a_ref, SUBLANES - 1, t)
             + w[1:2, :] * _shift_rows(a_ref, SUBLANES, t)
             + w[2:3, :] * _shift_rows(a_ref, SUBLANES + 1, t)
             + fcb_ref[:, pl.ds(c0, FF_CHUNK)])
        gelu = 0.5 * a * (1.0 + jnp.tanh(math.sqrt(2.0 / math.pi) * (a + 0.044715 * (a * a * a))))
        act = (gelu * gbr).astype(jnp.bfloat16)
        acc_ref[...] += jnp.dot(act, wdn_ref[pl.ds(c0, FF_CHUNK), :],
                                preferred_element_type=jnp.float32)
        return carry

    lax.fori_loop(0, D_FF // FF_CHUNK, chunk, 0)
    o_ref[...] = x_ref[...] + gate2 * acc_ref[...]


def _conv_ffn(x, mod, norm_gain, w_up, ffn_conv_w, ffn_conv_b, w_down):
    b, s, d = x.shape
    t = SEQ_TILE
    assert s % t == 0 and D_FF % FF_CHUNK == 0
    n_tiles = s // t
    rpt = t // SUBLANES
    n_groups = s // SUBLANES

    const = lambda shape: pl.BlockSpec(shape, lambda bi, i: (0,) * len(shape),
                                       pipeline_mode=pl.Buffered(1))
    return pl.pallas_call(
        _ffn_kernel,
        grid=(b, n_tiles),
        in_specs=[
            pl.BlockSpec((None, t, d), lambda bi, i: (bi, i, 0)),
            pl.BlockSpec((None, SUBLANES, d), lambda bi, i: (bi, jnp.maximum(i * rpt - 1, 0), 0)),
            pl.BlockSpec((None, SUBLANES, d),
                         lambda bi, i: (bi, jnp.minimum((i + 1) * rpt, n_groups - 1), 0)),
            pl.BlockSpec((None, 6, d), lambda bi, i: (bi, 0, 0)),
            const((1, d)),
            const((d, 2 * D_FF)),
            const((3, D_FF)),
            const((1, D_FF)),
            const((D_FF, d)),
        ],
        out_specs=pl.BlockSpec((None, t, d), lambda bi, i: (bi, i, 0)),
        out_shape=jax.ShapeDtypeStruct((b, s, d), jnp.float32),
        scratch_shapes=[
            pltpu.VMEM((t + 2 * SUBLANES, d), jnp.bfloat16),
            pltpu.VMEM((t + 2 * SUBLANES, FF_CHUNK), jnp.float32),
            pltpu.VMEM((t, d), jnp.float32),
        ],
        compiler_params=pltpu.CompilerParams(
            dimension_semantics=("arbitrary", "arbitrary"),
            vmem_limit_bytes=VMEM_LIMIT_BYTES),
        name="conv_ffn",
    )(x, x, x, mod, norm_gain.reshape(1, d), w_up.astype(jnp.bfloat16), ffn_conv_w,
      ffn_conv_b.reshape(1, D_FF), w_down.astype(jnp.bfloat16))


def kernel(x, c, w_ada, b_ada, norm1_gain, w_in, conv_w, q_gain, k_gain, sink, w_o, norm2_gain,
           w_up, ffn_conv_w, ffn_conv_b, w_down):
    depth = w_ada.shape[0]
    b, s, d = x.shape
    for l in range(depth):
        mod = _modulation(c, w_ada[l], b_ada[l]).reshape(b, 6, d)
        x = _token_mixer(x, mod, norm1_gain[l], w_in[l], conv_w[l], q_gain[l], k_gain[l],
                         sink[l], w_o[l])
        x = _conv_ffn(x, mod, norm2_gain[l], w_up[l], ffn_conv_w[l], ffn_conv_b[l], w_down[l])
    return x
```

```python
import math

import jax
import jax.numpy as jnp
from jax import lax
from jax.experimental import pallas as pl
from jax.experimental.pallas import tpu as pltpu

D_MODEL = 1024
HEAD_DIM = 64
N_Q_HEADS = D_MODEL // HEAD_DIM
N_KV_HEADS = N_Q_HEADS // 4
GROUP = N_Q_HEADS // N_KV_HEADS
D_CONV = D_MODEL
WINDOW = 128
BLOCK = 128
D_FF = ((8 * D_MODEL // 3 + 127) // 128) * 128
EPS = 1e-6
NEG_INF = -1e30
LOG2_E = math.log2(math.e)

OFF_CB = 0
OFF_CC = OFF_CB + D_CONV
OFF_CU = OFF_CC + D_CONV
OFF_Q = OFF_CU + D_CONV
OFF_K = OFF_Q + N_Q_HEADS * HEAD_DIM
OFF_V = OFF_K + N_KV_HEADS * HEAD_DIM
OFF_GA = OFF_V + N_KV_HEADS * HEAD_DIM
OFF_GB = OFF_GA + D_MODEL
D_IN = OFF_GB + D_MODEL

KV_W = N_KV_HEADS * HEAD_DIM
SUBLANES = 8
BF16_ROWS = 16
VT_ROWS = HEAD_DIM + BF16_ROWS
SEQ_TILE = 512
FF_CHUNK = 256
VMEM_LIMIT_BYTES = 56 * 1024 * 1024

ALIBI_SLOPES = [2.0 ** (-8.0 * (h + 1) / N_Q_HEADS) for h in range(N_Q_HEADS)]


def _rms_mod(xv, gain, scale, shift):
    ms = jnp.mean(xv * xv, axis=-1, keepdims=True)
    y = xv * lax.rsqrt(ms + EPS) * gain
    return y * (1.0 + scale) + shift


def _shift_rows(ref, start, rows):
    return ref[pl.ds(start, rows), :]


def _ada_kernel(c_ref, w_ref, b_ref, o_ref):
    c = c_ref[...]
    act = c * jax.nn.sigmoid(c)
    o_ref[...] = jnp.dot(act.astype(jnp.bfloat16), w_ref[...].astype(jnp.bfloat16),
                         preferred_element_type=jnp.float32) + b_ref[...]


def _modulation(c, w_ada, b_ada):
    b, d = c.shape
    n = w_ada.shape[1]
    rows = SUBLANES
    c_pad = jnp.zeros((rows, d), c.dtype).at[:b].set(c)
    tn = d
    out = pl.pallas_call(
        _ada_kernel,
        grid=(n // tn,),
        in_specs=[
            pl.BlockSpec((rows, d), lambda j: (0, 0)),
            pl.BlockSpec((d, tn), lambda j: (0, j)),
            pl.BlockSpec((1, tn), lambda j: (0, j)),
        ],
        out_specs=pl.BlockSpec((rows, tn), lambda j: (0, j)),
        out_shape=jax.ShapeDtypeStruct((rows, n), jnp.float32),
        compiler_params=pltpu.CompilerParams(dimension_semantics=("arbitrary",)),
        name="ada_modulation",
    )(c_pad, w_ada, b_ada.reshape(1, n))
    return out[:b]


def _group_mean_sq(v, e_ref):
    sq = (v * v).astype(jnp.bfloat16)
    return jnp.dot(sq, e_ref[...], preferred_element_type=jnp.float32) * (1.0 / HEAD_DIM)


def _mixer_kernel(sink_ref,
                  x_ref, xp_ref, xn_ref,
                  mod_ref,
                  g1n_ref, win_ref, wvt_ref, convw_ref, qg_ref, kg_ref, wo_ref,
                  o_ref,
                  h_ref,
                  k2_ref,
                  vt_ref,
                  u_ref,
                  qn_ref,
                  ya_ref,
                  ybt_ref,
                  mrg_ref,
                  bias_ref,
                  e_ref):
    t = x_ref.shape[0]
    i = pl.program_id(1)
    n_tiles = pl.num_programs(1)
    first = jnp.logical_and(pl.program_id(0) == 0, i == 0)

    @pl.when(first)
    def _():
        kj = lax.broadcasted_iota(jnp.int32, (BLOCK, BLOCK), 0)
        qi = lax.broadcasted_iota(jnp.int32, (BLOCK, BLOCK), 1)
        for blk in range(3):
            dist = jnp.abs(qi + BLOCK - (kj + blk * BLOCK))
            distf = dist.astype(jnp.float32)
            for h in range(N_Q_HEADS):
                bias_ref[3 * h + blk] = jnp.where(dist <= WINDOW,
                                                  -(ALIBI_SLOPES[h] * LOG2_E) * distf, NEG_INF)
        bias_ref[3 * N_Q_HEADS] = jnp.full((BLOCK, BLOCK), NEG_INF, jnp.float32)
        er = lax.broadcasted_iota(jnp.int32, (KV_W, KV_W), 0) // HEAD_DIM
        ec = lax.broadcasted_iota(jnp.int32, (KV_W, KV_W), 1) // HEAD_DIM
        e_ref[...] = jnp.where(er == ec, 1.0, 0.0).astype(jnp.bfloat16)

    shift1 = mod_ref[0:1, :]
    scale1 = mod_ref[1:2, :]
    gate1 = mod_ref[2:3, :]
    gain = g1n_ref[...]

    h_prev = _rms_mod(xp_ref[...], gain, scale1, shift1)
    h_next = _rms_mod(xn_ref[...], gain, scale1, shift1)
    h_ref[0:BLOCK, :] = jnp.where(i > 0, h_prev, 0.0).astype(jnp.bfloat16)
    h_ref[BLOCK:BLOCK + t, :] = _rms_mod(x_ref[...], gain, scale1, shift1).astype(jnp.bfloat16)
    h_ref[BLOCK + t:, :] = jnp.where(i < n_tiles - 1, h_next, 0.0).astype(jnp.bfloat16)

    k = jnp.dot(h_ref[...], win_ref[:, OFF_K:OFF_K + KV_W], preferred_element_type=jnp.float32)
    kn = k * lax.rsqrt(_group_mean_sq(k, e_ref) + EPS) * kg_ref[...]
    lane = lax.broadcasted_iota(jnp.int32, (1, 2 * HEAD_DIM), 1)
    lower = lane < HEAD_DIM
    for pair in range(N_KV_HEADS // 2):
        tile = kn[:, pair * 2 * HEAD_DIM:(pair + 1) * 2 * HEAD_DIM]
        swapped = pltpu.roll(tile, HEAD_DIM, 1)
        k2_ref[2 * pair] = jnp.where(lower, tile, swapped).astype(jnp.bfloat16)
        k2_ref[2 * pair + 1] = jnp.where(lower, swapped, tile).astype(jnp.bfloat16)

    vt = lax.dot_general(wvt_ref[...], h_ref[...], (((1,), (1,)), ((), ())),
                         preferred_element_type=jnp.float32)
    ones_rows = jnp.where(
        lax.broadcasted_iota(jnp.int32, (VT_ROWS - HEAD_DIM, t + 2 * BLOCK), 0) == 0, 1.0, 0.0)
    for g in range(N_KV_HEADS):
        vt_ref[g, 0:HEAD_DIM, :] = vt[g * HEAD_DIM:(g + 1) * HEAD_DIM, :].astype(jnp.bfloat16)
        vt_ref[g, HEAD_DIM:, :] = ones_rows.astype(jnp.bfloat16)

    lo = BLOCK - SUBLANES
    hc = h_ref[lo:lo + t + 2 * SUBLANES, :]
    cc = jnp.dot(hc, win_ref[:, OFF_CC:OFF_CC + D_CONV], preferred_element_type=jnp.float32)
    cu = jnp.dot(hc, win_ref[:, OFF_CU:OFF_CU + D_CONV], preferred_element_type=jnp.float32)
    u_ref[...] = cc * cu
    conv = (convw_ref[0:1, :] * _shift_rows(u_ref, SUBLANES - 1, t)
            + convw_ref[1:2, :] * _shift_rows(u_ref, SUBLANES, t)
            + convw_ref[2:3, :] * _shift_rows(u_ref, SUBLANES + 1, t))
    ht = h_ref[BLOCK:BLOCK + t, :]
    cb = jnp.dot(ht, win_ref[:, OFF_CB:OFF_CB + D_CONV], preferred_element_type=jnp.float32)
    ga = jnp.dot(ht, win_ref[:, OFF_GA:OFF_GA + D_MODEL], preferred_element_type=jnp.float32)
    ya_ref[...] = jax.nn.sigmoid(ga) * (cb * conv)

    q = jnp.dot(ht, win_ref[:, OFF_Q:OFF_Q + D_MODEL], preferred_element_type=jnp.float32)
    qscale = qg_ref[...] * (LOG2_E / math.sqrt(HEAD_DIM))
    for c in range(D_MODEL // KV_W):
        qc = q[:, c * KV_W:(c + 1) * KV_W]
        qn_ref[:, c * KV_W:(c + 1) * KV_W] = (
            qc * lax.rsqrt(_group_mean_sq(qc, e_ref) + EPS) * qscale[:, c * KV_W:(c + 1) * KV_W]
        ).astype(jnp.bfloat16)

    qlower = lax.broadcasted_iota(jnp.int32, (BLOCK, 2 * HEAD_DIM), 1) < HEAD_DIM
    zero_q = jnp.zeros((BLOCK, 2 * HEAD_DIM), jnp.bfloat16)
    mask_blk = 3 * N_Q_HEADS

    def scores_t(qb, g):
        r0 = qb * BLOCK
        kb = k2_ref[g, r0:r0 + 3 * BLOCK, :]
        qs = []
        for half in range(GROUP // 2):
            c0 = (g * GROUP + 2 * half) * HEAD_DIM
            pair = qn_ref[r0:r0 + BLOCK, c0:c0 + 2 * HEAD_DIM]
            qs.append(jnp.where(qlower, pair, zero_q))
            qs.append(jnp.where(qlower, zero_q, pair))
        q4 = jnp.concatenate(qs, axis=0)
        return lax.dot_general(kb, q4, (((1,), (1,)), ((), ())),
                               preferred_element_type=jnp.float32)

    def attend(qb, g, st):
        r0 = qb * BLOCK
        ps = []
        sink_terms = []
        for j in range(GROUP):
            hh = g * GROUP + j
            idx = [3 * hh, 3 * hh + 1, 3 * hh + 2]
            if qb == 0:
                idx[0] = jnp.where(i == 0, mask_blk, idx[0])
            if qb == t // BLOCK - 1:
                idx[2] = jnp.where(i == n_tiles - 1, mask_blk, idx[2])
            s = jnp.concatenate(
                [st[blk * BLOCK:(blk + 1) * BLOCK, j * BLOCK:(j + 1) * BLOCK] + bias_ref[idx[blk]]
                 for blk in range(3)], axis=0)
            sink = sink_ref[hh] * LOG2_E
            m = jnp.maximum(jnp.max(s, axis=0, keepdims=True), sink)
            ps.append(jnp.exp2(s - m).astype(jnp.bfloat16))
            sink_terms.append(jnp.exp2(sink - m))
        p4 = jnp.concatenate(ps, axis=1)
        ot = jnp.dot(vt_ref[g, :, r0:r0 + 3 * BLOCK], p4,
                     preferred_element_type=jnp.float32)
        den = ot[HEAD_DIM:HEAD_DIM + 1, :] + jnp.concatenate(sink_terms, axis=1)
        y = ot[0:HEAD_DIM, :] / den
        for j in range(GROUP):
            hh = g * GROUP + j
            ybt_ref[hh * HEAD_DIM:(hh + 1) * HEAD_DIM, r0:r0 + BLOCK] = y[:, j * BLOCK:(j + 1) * BLOCK]

    stages = [(qb, g) for qb in range(t // BLOCK) for g in range(N_KV_HEADS)]
    st_next = scores_t(*stages[0])
    for n, (qb, g) in enumerate(stages):
        st = st_next
        if n + 1 < len(stages):
            st_next = scores_t(*stages[n + 1])
        attend(qb, g, st)

    gb = jnp.dot(ht, win_ref[:, OFF_GB:OFF_GB + D_MODEL], preferred_element_type=jnp.float32)
    yb = ybt_ref[...].T
    mrg_ref[...] = (ya_ref[...] + jax.nn.sigmoid(gb) * yb).astype(jnp.bfloat16)
    out = jnp.dot(mrg_ref[...], wo_ref[...], preferred_element_type=jnp.float32)
    o_ref[...] = x_ref[...] + gate1 * out


def _token_mixer(x, mod, norm_gain, w_in, conv_w, q_gain, k_gain, sink, w_o):
    b, s, d = x.shape
    t = SEQ_TILE
    assert s % t == 0 and t % BLOCK == 0
    n_tiles = s // t
    bpt = t // BLOCK
    n_blocks = s // BLOCK
    w_in_bf = w_in.astype(jnp.bfloat16)
    w_v_t = w_in_bf[:, OFF_V:OFF_V + KV_W].T

    const = lambda shape: pl.BlockSpec(shape, lambda bi, i, *_: (0,) * len(shape),
                                       pipeline_mode=pl.Buffered(1))
    grid_spec = pltpu.PrefetchScalarGridSpec(
        num_scalar_prefetch=1,
        grid=(b, n_tiles),
        in_specs=[
            pl.BlockSpec((None, t, d), lambda bi, i, *_: (bi, i, 0)),
            pl.BlockSpec((None, BLOCK, d), lambda bi, i, *_: (bi, jnp.maximum(i * bpt - 1, 0), 0)),
            pl.BlockSpec((None, BLOCK, d),
                         lambda bi, i, *_: (bi, jnp.minimum((i + 1) * bpt, n_blocks - 1), 0)),
            pl.BlockSpec((None, 6, d), lambda bi, i, *_: (bi, 0, 0)),
            const((1, d)),
            const((d, D_IN)),
            const((KV_W, d)),
            const((3, D_CONV)),
            const((1, d)),
            const((1, KV_W)),
            const((d, d)),
        ],
        out_specs=pl.BlockSpec((None, t, d), lambda bi, i, *_: (bi, i, 0)),
        scratch_shapes=[
            pltpu.VMEM((t + 2 * BLOCK, d), jnp.bfloat16),
            pltpu.VMEM((N_KV_HEADS, t + 2 * BLOCK, 2 * HEAD_DIM), jnp.bfloat16),
            pltpu.VMEM((N_KV_HEADS, VT_ROWS, t + 2 * BLOCK), jnp.bfloat16),
            pltpu.VMEM((t + 2 * SUBLANES, D_CONV), jnp.float32),
            pltpu.VMEM((t, d), jnp.bfloat16),
            pltpu.VMEM((t, d), jnp.float32),
            pltpu.VMEM((d, t), jnp.float32),
            pltpu.VMEM((t, d), jnp.bfloat16),
            pltpu.VMEM((3 * N_Q_HEADS + 1, BLOCK, BLOCK), jnp.float32),
            pltpu.VMEM((KV_W, KV_W), jnp.bfloat16),
        ],
    )
    return pl.pallas_call(
        _mixer_kernel,
        grid_spec=grid_spec,
        out_shape=jax.ShapeDtypeStruct((b, s, d), jnp.float32),
        compiler_params=pltpu.CompilerParams(
            dimension_semantics=("arbitrary", "arbitrary"),
            vmem_limit_bytes=VMEM_LIMIT_BYTES),
        name="token_mixer",
    )(sink, x, x, x, mod, norm_gain.reshape(1, d), w_in_bf, w_v_t, conv_w,
      jnp.tile(q_gain, N_Q_HEADS).reshape(1, d), jnp.tile(k_gain, N_KV_HEADS).reshape(1, KV_W),
      w_o.astype(jnp.bfloat16))


def _ffn_kernel(x_ref, xp_ref, xn_ref,
                mod_ref, g2n_ref, wup_ref, fcw_ref, fcb_ref, wdn_ref,
                o_ref,
                h_ref,
                a_ref,
                act_ref):
    t = x_ref.shape[0]
    i = pl.program_id(1)
    n_tiles = pl.num_programs(1)
    shift2 = mod_ref[3:4, :]
    scale2 = mod_ref[4:5, :]
    gate2 = mod_ref[5:6, :]
    gain = g2n_ref[...]

    h_prev = _rms_mod(xp_ref[...], gain, scale2, shift2)
    h_next = _rms_mod(xn_ref[...], gain, scale2, shift2)
    h_ref[0:SUBLANES, :] = jnp.where(i > 0, h_prev, 0.0).astype(jnp.bfloat16)
    h_ref[SUBLANES:SUBLANES + t, :] = _rms_mod(x_ref[...], gain, scale2, shift2).astype(jnp.bfloat16)
    h_ref[SUBLANES + t:, :] = jnp.where(i < n_tiles - 1, h_next, 0.0).astype(jnp.bfloat16)

    for c in range(D_FF // FF_CHUNK):
        c0 = c * FF_CHUNK
        a_ref[c % 2] = jnp.dot(h_ref[...], wup_ref[:, c0:c0 + FF_CHUNK],
                               preferred_element_type=jnp.float32)
        gbr = jnp.dot(h_ref[SUBLANES:SUBLANES + t, :], wup_ref[:, D_FF + c0:D_FF + c0 + FF_CHUNK],
                      preferred_element_type=jnp.float32)
        w = fcw_ref[:, c0:c0 + FF_CHUNK]
        av = a_ref.at[c % 2]
        a = (w[0:1, :] * _shift_rows(av, SUBLANES - 1, t)
             + w[1:2, :] * _shift_rows(av, SUBLANES, t)
             + w[2:3, :] * _shift_rows(av, SUBLANES + 1, t)
             + fcb_ref[:, c0:c0 + FF_CHUNK])
        gelu = 0.5 * a * (1.0 + jnp.tanh(math.sqrt(2.0 / math.pi) * (a + 0.044715 * (a * a * a))))
        act_ref[:, c0:c0 + FF_CHUNK] = (gelu * gbr).astype(jnp.bfloat16)

    y = jnp.dot(act_ref[...], wdn_ref[...], preferred_element_type=jnp.float32)
    o_ref[...] = x_ref[...] + gate2 * y


def _conv_ffn(x, mod, norm_gain, w_up, ffn_conv_w, ffn_conv_b, w_down):
    b, s, d = x.shape
    t = SEQ_TILE
    assert s % t == 0 and D_FF % FF_CHUNK == 0
    n_tiles = s // t
    rpt = t // SUBLANES
    n_groups = s // SUBLANES

    const = lambda shape: pl.BlockSpec(shape, lambda bi, i: (0,) * len(shape),
                                       pipeline_mode=pl.Buffered(1))
    return pl.pallas_call(
        _ffn_kernel,
        grid=(b, n_tiles),
        in_specs=[
            pl.BlockSpec((None, t, d), lambda bi, i: (bi, i, 0)),
            pl.BlockSpec((None, SUBLANES, d), lambda bi, i: (bi, jnp.maximum(i * rpt - 1, 0), 0)),
            pl.BlockSpec((None, SUBLANES, d),
                         lambda bi, i: (bi, jnp.minimum((i + 1) * rpt, n_groups - 1), 0)),
            pl.BlockSpec((None, 6, d), lambda bi, i: (bi, 0, 0)),
            const((1, d)),
            const((d, 2 * D_FF)),
            const((3, D_FF)),
            const((1, D_FF)),
            const((D_FF, d)),
        ],
        out_specs=pl.BlockSpec((None, t, d), lambda bi, i: (bi, i, 0)),
        out_shape=jax.ShapeDtypeStruct((b, s, d), jnp.float32),
        scratch_shapes=[
            pltpu.VMEM((t + 2 * SUBLANES, d), jnp.bfloat16),
            pltpu.VMEM((2, t + 2 * SUBLANES, FF_CHUNK), jnp.float32),
            pltpu.VMEM((t, D_FF), jnp.bfloat16),
        ],
        compiler_params=pltpu.CompilerParams(
            dimension_semantics=("arbitrary", "arbitrary"),
            vmem_limit_bytes=VMEM_LIMIT_BYTES),
        name="conv_ffn",
    )(x, x, x, mod, norm_gain.reshape(1, d), w_up.astype(jnp.bfloat16), ffn_conv_w,
      ffn_conv_b.reshape(1, D_FF), w_down.astype(jnp.bfloat16))


def kernel(x, c, w_ada, b_ada, norm1_gain, w_in, conv_w, q_gain, k_gain, sink, w_o, norm2_gain,
           w_up, ffn_conv_w, ffn_conv_b, w_down):
    depth = w_ada.shape[0]
    b, s, d = x.shape
    for l in range(depth):
        mod = _modulation(c, w_ada[l], b_ada[l]).reshape(b, 6, d)
        x = _token_mixer(x, mod, norm1_gain[l], w_in[l], conv_w[l], q_gain[l], k_gain[l],
                         sink[l], w_o[l])
        x = _conv_ffn(x, mod, norm2_gain[l], w_up[l], ffn_conv_w[l], ffn_conv_b[l], w_down[l])
    return x
```

```python
import math

import jax
import jax.numpy as jnp
from jax import lax
from jax.experimental import pallas as pl
from jax.experimental.pallas import tpu as pltpu

D_MODEL = 1024
HEAD_DIM = 64
N_Q_HEADS = D_MODEL // HEAD_DIM
N_KV_HEADS = N_Q_HEADS // 4
GROUP = N_Q_HEADS // N_KV_HEADS
D_CONV = D_MODEL
WINDOW = 128
BLOCK = 128
D_FF = ((8 * D_MODEL // 3 + 127) // 128) * 128
EPS = 1e-6
NEG_INF = -1e30
LOG2_E = math.log2(math.e)

OFF_CB = 0
OFF_CC = OFF_CB + D_CONV
OFF_CU = OFF_CC + D_CONV
OFF_Q = OFF_CU + D_CONV
OFF_K = OFF_Q + N_Q_HEADS * HEAD_DIM
OFF_V = OFF_K + N_KV_HEADS * HEAD_DIM
OFF_GA = OFF_V + N_KV_HEADS * HEAD_DIM
OFF_GB = OFF_GA + D_MODEL
D_IN = OFF_GB + D_MODEL

KV_W = N_KV_HEADS * HEAD_DIM
SUBLANES = 8
BF16_ROWS = 16
VT_ROWS = HEAD_DIM + BF16_ROWS
SEQ_TILE = 512
FF_CHUNK = 256
WEIGHT_CHUNK_ROWS = 64
VMEM_LIMIT_BYTES = 56 * 1024 * 1024

ALIBI_SLOPES = [2.0 ** (-8.0 * (h + 1) / N_Q_HEADS) for h in range(N_Q_HEADS)]


def _rms_mod(xv, gain, scale, shift):
    ms = jnp.mean(xv * xv, axis=-1, keepdims=True)
    y = xv * lax.rsqrt(ms + EPS) * gain
    return y * (1.0 + scale) + shift


def _dwconv3(v_ext, w):
    rows = v_ext.shape[0]
    t = rows - 2 * SUBLANES
    prev = pltpu.roll(v_ext, 1, 0)[SUBLANES:SUBLANES + t]
    nxt = pltpu.roll(v_ext, rows - 1, 0)[SUBLANES:SUBLANES + t]
    return w[0:1, :] * prev + w[1:2, :] * v_ext[SUBLANES:SUBLANES + t] + w[2:3, :] * nxt


def _stage_weights(src_hbm, dst_ref, chunk_rows):
    rows, cols = src_hbm.shape
    n_chunks = rows // chunk_rows
    assert n_chunks * chunk_rows == rows

    def body(stage_ref, sem_ref):
        def copy(c, slot):
            r0 = pl.multiple_of(c * chunk_rows, chunk_rows)
            return pltpu.make_async_copy(src_hbm.at[pl.ds(r0, chunk_rows), :],
                                         stage_ref.at[slot], sem_ref.at[slot])

        copy(0, 0).start()

        def step(c, carry):
            slot = lax.rem(c, 2)

            @pl.when(c + 1 < n_chunks)
            def _():
                copy(c + 1, 1 - slot).start()

            copy(c, slot).wait()
            r0 = pl.multiple_of(c * chunk_rows, chunk_rows)
            dst_ref[pl.ds(r0, chunk_rows), :] = stage_ref[slot].astype(jnp.bfloat16)
            return carry

        lax.fori_loop(0, n_chunks, step, 0)

    pl.run_scoped(body, pltpu.VMEM((2, chunk_rows, cols), jnp.float32),
                  pltpu.SemaphoreType.DMA((2,)))


def _ada_kernel(c_ref, w_ref, b_ref, o_ref):
    c = c_ref[...]
    act = c * jax.nn.sigmoid(c)
    o_ref[...] = jnp.dot(act.astype(jnp.bfloat16), w_ref[...].astype(jnp.bfloat16),
                         preferred_element_type=jnp.float32) + b_ref[...]


def _modulation(c, w_ada, b_ada):
    b, d = c.shape
    n = w_ada.shape[1]
    rows = SUBLANES
    c_pad = jnp.zeros((rows, d), c.dtype).at[:b].set(c)
    tn = d
    out = pl.pallas_call(
        _ada_kernel,
        grid=(n // tn,),
        in_specs=[
            pl.BlockSpec((rows, d), lambda j: (0, 0)),
            pl.BlockSpec((d, tn), lambda j: (0, j)),
            pl.BlockSpec((1, tn), lambda j: (0, j)),
        ],
        out_specs=pl.BlockSpec((rows, tn), lambda j: (0, j)),
        out_shape=jax.ShapeDtypeStruct((rows, n), jnp.float32),
        compiler_params=pltpu.CompilerParams(dimension_semantics=("arbitrary",)),
        name="ada_modulation",
    )(c_pad, w_ada, b_ada.reshape(1, n))
    return out[:b]


def _group_mean_sq(v, e_ref):
    sq = (v * v).astype(jnp.bfloat16)
    return jnp.dot(sq, e_ref[...], preferred_element_type=jnp.float32) * (1.0 / HEAD_DIM)


def _mixer_kernel(sink_ref,
                  x_ref, xp_ref, xn_ref,
                  mod_ref,
                  g1n_ref, win_hbm, convw_ref, qg_ref, kg_ref, wo_hbm,
                  o_ref,
                  win_ref,
                  wo_ref,
                  wvt_ref,
                  h_ref,
                  hc_ref,
                  k2_ref,
                  vt_ref,
                  qn_ref,
                  ya_ref,
                  ybt_ref,
                  mrg_ref,
                  bias_ref,
                  e_ref):
    t = x_ref.shape[0]
    i = pl.program_id(1)
    n_tiles = pl.num_programs(1)
    first = jnp.logical_and(pl.program_id(0) == 0, i == 0)

    @pl.when(first)
    def _():
        _stage_weights(win_hbm, win_ref, WEIGHT_CHUNK_ROWS)
        _stage_weights(wo_hbm, wo_ref, WEIGHT_CHUNK_ROWS)
        wvt_ref[...] = win_ref[:, OFF_V:OFF_V + KV_W].astype(jnp.float32).T.astype(jnp.bfloat16)
        kj = lax.broadcasted_iota(jnp.int32, (BLOCK, BLOCK), 0)
        qi = lax.broadcasted_iota(jnp.int32, (BLOCK, BLOCK), 1)
        for blk in range(3):
            dist = jnp.abs(qi + BLOCK - (kj + blk * BLOCK))
            distf = dist.astype(jnp.float32)
            for h in range(N_Q_HEADS):
                bias_ref[3 * h + blk] = jnp.where(dist <= WINDOW,
                                                  -(ALIBI_SLOPES[h] * LOG2_E) * distf, NEG_INF)
        bias_ref[3 * N_Q_HEADS] = jnp.full((BLOCK, BLOCK), NEG_INF, jnp.float32)
        er = lax.broadcasted_iota(jnp.int32, (KV_W, KV_W), 0) // HEAD_DIM
        ec = lax.broadcasted_iota(jnp.int32, (KV_W, KV_W), 1) // HEAD_DIM
        e_ref[...] = jnp.where(er == ec, 1.0, 0.0).astype(jnp.bfloat16)

    shift1 = mod_ref[0:1, :]
    scale1 = mod_ref[1:2, :]
    gate1 = mod_ref[2:3, :]
    gain = g1n_ref[...]

    h_prev = _rms_mod(xp_ref[...], gain, scale1, shift1)
    h_next = _rms_mod(xn_ref[...], gain, scale1, shift1)
    h_prev = jnp.where(i > 0, h_prev, 0.0)
    h_next = jnp.where(i < n_tiles - 1, h_next, 0.0)
    h_tile = _rms_mod(x_ref[...], gain, scale1, shift1)
    h_ref[0:BLOCK, :] = h_prev.astype(jnp.bfloat16)
    h_ref[BLOCK:BLOCK + t, :] = h_tile.astype(jnp.bfloat16)
    h_ref[BLOCK + t:, :] = h_next.astype(jnp.bfloat16)
    hc_ref[...] = jnp.concatenate(
        [h_prev[BLOCK - SUBLANES:], h_tile, h_next[:SUBLANES]], axis=0).astype(jnp.bfloat16)

    k = jnp.dot(h_ref[...], win_ref[:, OFF_K:OFF_K + KV_W], preferred_element_type=jnp.float32)
    kn = k * lax.rsqrt(_group_mean_sq(k, e_ref) + EPS) * kg_ref[...]
    lane = lax.broadcasted_iota(jnp.int32, (1, 2 * HEAD_DIM), 1)
    lower = lane < HEAD_DIM
    for pair in range(N_KV_HEADS // 2):
        tile = kn[:, pair * 2 * HEAD_DIM:(pair + 1) * 2 * HEAD_DIM]
        swapped = pltpu.roll(tile, HEAD_DIM, 1)
        k2_ref[2 * pair] = jnp.where(lower, tile, swapped).astype(jnp.bfloat16)
        k2_ref[2 * pair + 1] = jnp.where(lower, swapped, tile).astype(jnp.bfloat16)

    vt = lax.dot_general(wvt_ref[...], h_ref[...], (((1,), (1,)), ((), ())),
                         preferred_element_type=jnp.float32)
    ones_rows = jnp.where(
        lax.broadcasted_iota(jnp.int32, (VT_ROWS - HEAD_DIM, t + 2 * BLOCK), 0) == 0, 1.0, 0.0)
    for g in range(N_KV_HEADS):
        vt_ref[g, 0:HEAD_DIM, :] = vt[g * HEAD_DIM:(g + 1) * HEAD_DIM, :].astype(jnp.bfloat16)
        vt_ref[g, HEAD_DIM:, :] = ones_rows.astype(jnp.bfloat16)

    hc = hc_ref[...]
    cc = jnp.dot(hc, win_ref[:, OFF_CC:OFF_CC + D_CONV], preferred_element_type=jnp.float32)
    cu = jnp.dot(hc, win_ref[:, OFF_CU:OFF_CU + D_CONV], preferred_element_type=jnp.float32)
    conv = _dwconv3(cc * cu, convw_ref[...])
    ht = h_ref[BLOCK:BLOCK + t, :]
    cb = jnp.dot(ht, win_ref[:, OFF_CB:OFF_CB + D_CONV], preferred_element_type=jnp.float32)
    ga = jnp.dot(ht, win_ref[:, OFF_GA:OFF_GA + D_MODEL], preferred_element_type=jnp.float32)
    ya_ref[...] = jax.nn.sigmoid(ga) * (cb * conv)

    q = jnp.dot(ht, win_ref[:, OFF_Q:OFF_Q + D_MODEL], preferred_element_type=jnp.float32)
    qscale = qg_ref[...] * (LOG2_E / math.sqrt(HEAD_DIM))
    for c in range(D_MODEL // KV_W):
        qc = q[:, c * KV_W:(c + 1) * KV_W]
        qn_ref[:, c * KV_W:(c + 1) * KV_W] = (
            qc * lax.rsqrt(_group_mean_sq(qc, e_ref) + EPS) * qscale[:, c * KV_W:(c + 1) * KV_W]
        ).astype(jnp.bfloat16)

    qlower = lax.broadcasted_iota(jnp.int32, (BLOCK, 2 * HEAD_DIM), 1) < HEAD_DIM
    zero_q = jnp.zeros((BLOCK, 2 * HEAD_DIM), jnp.bfloat16)
    mask_blk = 3 * N_Q_HEADS

    def scores_t(qb, g):
        r0 = qb * BLOCK
        kb = k2_ref[g, r0:r0 + 3 * BLOCK, :]
        qs = []
        for half in range(GROUP // 2):
            c0 = (g * GROUP + 2 * half) * HEAD_DIM
            pair = qn_ref[r0:r0 + BLOCK, c0:c0 + 2 * HEAD_DIM]
            qs.append(jnp.where(qlower, pair, zero_q))
            qs.append(jnp.where(qlower, zero_q, pair))
        q4 = jnp.concatenate(qs, axis=0)
        return lax.dot_general(kb, q4, (((1,), (1,)), ((), ())),
                               preferred_element_type=jnp.float32)

    def attend(qb, g, st):
        r0 = qb * BLOCK
        ps = []
        sink_terms = []
        for j in range(GROUP):
            hh = g * GROUP + j
            idx = [3 * hh, 3 * hh + 1, 3 * hh + 2]
            if qb == 0:
                idx[0] = jnp.where(i == 0, mask_blk, idx[0])
            if qb == t // BLOCK - 1:
                idx[2] = jnp.where(i == n_tiles - 1, mask_blk, idx[2])
            s = jnp.concatenate(
                [st[blk * BLOCK:(blk + 1) * BLOCK, j * BLOCK:(j + 1) * BLOCK] + bias_ref[idx[blk]]
                 for blk in range(3)], axis=0)
            sink = sink_ref[hh] * LOG2_E
            m = jnp.maximum(jnp.max(s, axis=0, keepdims=True), sink)
            ps.append(jnp.exp2(s - m).astype(jnp.bfloat16))
            sink_terms.append(jnp.exp2(sink - m))
        p4 = jnp.concatenate(ps, axis=1)
        ot = jnp.dot(vt_ref[g, :, r0:r0 + 3 * BLOCK], p4,
                     preferred_element_type=jnp.float32)
        den = ot[HEAD_DIM:HEAD_DIM + 1, :] + jnp.concatenate(sink_terms, axis=1)
        y = ot[0:HEAD_DIM, :] / den
        for j in range(GROUP):
            hh = g * GROUP + j
            ybt_ref[hh * HEAD_DIM:(hh + 1) * HEAD_DIM, r0:r0 + BLOCK] = y[:, j * BLOCK:(j + 1) * BLOCK]

    stages = [(qb, g) for qb in range(t // BLOCK) for g in range(N_KV_HEADS)]
    st_next = scores_t(*stages[0])
    for n, (qb, g) in enumerate(stages):
        st = st_next
        if n + 1 < len(stages):
            st_next = scores_t(*stages[n + 1])
        attend(qb, g, st)

    gb = jnp.dot(ht, win_ref[:, OFF_GB:OFF_GB + D_MODEL], preferred_element_type=jnp.float32)
    yb = ybt_ref[...].T
    mrg_ref[...] = (ya_ref[...] + jax.nn.sigmoid(gb) * yb).astype(jnp.bfloat16)
    out = jnp.dot(mrg_ref[...], wo_ref[...], preferred_element_type=jnp.float32)
    o_ref[...] = x_ref[...] + gate1 * out


def _token_mixer(x, mod, norm_gain, w_in, conv_w, q_gain, k_gain, sink, w_o):
    b, s, d = x.shape
    t = SEQ_TILE
    assert s % t == 0 and t % BLOCK == 0
    n_tiles = s // t
    bpt = t // BLOCK
    n_blocks = s // BLOCK

    const = lambda shape: pl.BlockSpec(shape, lambda bi, i, *_: (0,) * len(shape),
                                       pipeline_mode=pl.Buffered(1))
    grid_spec = pltpu.PrefetchScalarGridSpec(
        num_scalar_prefetch=1,
        grid=(b, n_tiles),
        in_specs=[
            pl.BlockSpec((None, t, d), lambda bi, i, *_: (bi, i, 0)),
            pl.BlockSpec((None, BLOCK, d), lambda bi, i, *_: (bi, jnp.maximum(i * bpt - 1, 0), 0)),
            pl.BlockSpec((None, BLOCK, d),
                         lambda bi, i, *_: (bi, jnp.minimum((i + 1) * bpt, n_blocks - 1), 0)),
            pl.BlockSpec((None, 6, d), lambda bi, i, *_: (bi, 0, 0)),
            const((1, d)),
            pl.BlockSpec(memory_space=pl.ANY),
            const((3, D_CONV)),
            const((1, d)),
            const((1, KV_W)),
            pl.BlockSpec(memory_space=pl.ANY),
        ],
        out_specs=pl.BlockSpec((None, t, d), lambda bi, i, *_: (bi, i, 0)),
        scratch_shapes=[
            pltpu.VMEM((d, D_IN), jnp.bfloat16),
            pltpu.VMEM((d, d), jnp.bfloat16),
            pltpu.VMEM((KV_W, d), jnp.bfloat16),
            pltpu.VMEM((t + 2 * BLOCK, d), jnp.bfloat16),
            pltpu.VMEM((t + 2 * SUBLANES, d), jnp.bfloat16),
            pltpu.VMEM((N_KV_HEADS, t + 2 * BLOCK, 2 * HEAD_DIM), jnp.bfloat16),
            pltpu.VMEM((N_KV_HEADS, VT_ROWS, t + 2 * BLOCK), jnp.bfloat16),
            pltpu.VMEM((t, d), jnp.bfloat16),
            pltpu.VMEM((t, d), jnp.float32),
            pltpu.VMEM((d, t), jnp.float32),
            pltpu.VMEM((t, d), jnp.bfloat16),
            pltpu.VMEM((3 * N_Q_HEADS + 1, BLOCK, BLOCK), jnp.float32),
            pltpu.VMEM((KV_W, KV_W), jnp.bfloat16),
        ],
    )
    return pl.pallas_call(
        _mixer_kernel,
        grid_spec=grid_spec,
        out_shape=jax.ShapeDtypeStruct((b, s, d), jnp.float32),
        compiler_params=pltpu.CompilerParams(
            dimension_semantics=("arbitrary", "arbitrary"),
            vmem_limit_bytes=VMEM_LIMIT_BYTES),
        name="token_mixer",
    )(sink, x, x, x, mod, norm_gain.reshape(1, d), w_in, conv_w,
      jnp.tile(q_gain, N_Q_HEADS).reshape(1, d), jnp.tile(k_gain, N_KV_HEADS).reshape(1, KV_W), w_o)


def _ffn_kernel(x_ref, xp_ref, xn_ref,
                mod_ref, g2n_ref, wup_hbm, fcw_ref, fcb_ref, wdn_hbm,
                o_ref,
                wup_ref,
                wdn_ref,
                ht_ref,
                he_ref,
                act_ref):
    t = x_ref.shape[0]
    i = pl.program_id(1)
    n_tiles = pl.num_programs(1)

    @pl.when(jnp.logical_and(pl.program_id(0) == 0, i == 0))
    def _():
        _stage_weights(wup_hbm, wup_ref, WEIGHT_CHUNK_ROWS)
        _stage_weights(wdn_hbm, wdn_ref, WEIGHT_CHUNK_ROWS)

    shift2 = mod_ref[3:4, :]
    scale2 = mod_ref[4:5, :]
    gate2 = mod_ref[5:6, :]
    gain = g2n_ref[...]

    h_prev = _rms_mod(xp_ref[...], gain, scale2, shift2)
    h_next = _rms_mod(xn_ref[...], gain, scale2, shift2)
    h_prev = jnp.where(i > 0, h_prev, 0.0)
    h_next = jnp.where(i < n_tiles - 1, h_next, 0.0)
    h_tile = _rms_mod(x_ref[...], gain, scale2, shift2)
    ht_ref[...] = h_tile.astype(jnp.bfloat16)
    he_ref[...] = jnp.concatenate([h_prev, h_tile, h_next], axis=0).astype(jnp.bfloat16)

    for c in range(D_FF // FF_CHUNK):
        c0 = c * FF_CHUNK
        a_ext = jnp.dot(he_ref[...], wup_ref[:, c0:c0 + FF_CHUNK],
                        preferred_element_type=jnp.float32)
        gbr = jnp.dot(ht_ref[...], wup_ref[:, D_FF + c0:D_FF + c0 + FF_CHUNK],
                      preferred_element_type=jnp.float32)
        a = _dwconv3(a_ext, fcw_ref[:, c0:c0 + FF_CHUNK]) + fcb_ref[:, c0:c0 + FF_CHUNK]
        gelu = 0.5 * a * (1.0 + jnp.tanh(math.sqrt(2.0 / math.pi) * (a + 0.044715 * (a * a * a))))
        act_ref[:, c0:c0 + FF_CHUNK] = (gelu * gbr).astype(jnp.bfloat16)

    y = jnp.dot(act_ref[...], wdn_ref[...], preferred_element_type=jnp.float32)
    o_ref[...] = x_ref[...] + gate2 * y


def _conv_ffn(x, mod, norm_gain, w_up, ffn_conv_w, ffn_conv_b, w_down):
    b, s, d = x.shape
    t = SEQ_TILE
    assert s % t == 0 and D_FF % FF_CHUNK == 0
    n_tiles = s // t
    rpt = t // SUBLANES
    n_groups = s // SUBLANES

    const = lambda shape: pl.BlockSpec(shape, lambda bi, i: (0,) * len(shape),
                                       pipeline_mode=pl.Buffered(1))
    return pl.pallas_call(
        _ffn_kernel,
        grid=(b, n_tiles),
        in_specs=[
            pl.BlockSpec((None, t, d), lambda bi, i: (bi, i, 0)),
            pl.BlockSpec((None, SUBLANES, d), lambda bi, i: (bi, jnp.maximum(i * rpt - 1, 0), 0)),
            pl.BlockSpec((None, SUBLANES, d),
                         lambda bi, i: (bi, jnp.minimum((i + 1) * rpt, n_groups - 1), 0)),
            pl.BlockSpec((None, 6, d), lambda bi, i: (bi, 0, 0)),
            const((1, d)),
            pl.BlockSpec(memory_space=pl.ANY),
            const((3, D_FF)),
            const((1, D_FF)),
            pl.BlockSpec(memory_space=pl.ANY),
        ],
        out_specs=pl.BlockSpec((None, t, d), lambda bi, i: (bi, i, 0)),
        out_shape=jax.ShapeDtypeStruct((b, s, d), jnp.float32),
        scratch_shapes=[
            pltpu.VMEM((d, 2 * D_FF), jnp.bfloat16),
            pltpu.VMEM((D_FF, d), jnp.bfloat16),
            pltpu.VMEM((t, d), jnp.bfloat16),
            pltpu.VMEM((t + 2 * SUBLANES, d), jnp.bfloat16),
            pltpu.VMEM((t, D_FF), jnp.bfloat16),
        ],
        compiler_params=pltpu.CompilerParams(
            dimension_semantics=("arbitrary", "arbitrary"),
            vmem_limit_bytes=VMEM_LIMIT_BYTES),
        name="conv_ffn",
    )(x, x, x, mod, norm_gain.reshape(1, d), w_up, ffn_conv_w, ffn_conv_b.reshape(1, D_FF), w_down)


def kernel(x, c, w_ada, b_ada, norm1_gain, w_in, conv_w, q_gain, k_gain, sink, w_o, norm2_gain,
           w_up, ffn_conv_w, ffn_conv_b, w_down):
    depth = w_ada.shape[0]
    b, s, d = x.shape
    for l in range(depth):
        mod = _modulation(c, w_ada[l], b_ada[l]).reshape(b, 6, d)
        x = _token_mixer(x, mod, norm1_gain[l], w_in[l], conv_w[l], q_gain[l], k_gain[l],
                         sink[l], w_o[l])
        x = _conv_ffn(x, mod, norm2_gain[l], w_up[l], ffn_conv_w[l], ffn_conv_b[l], w_down[l])
    return x
```

```python
import math

import jax
import jax.numpy as jnp
from jax import lax
from jax.experimental import pallas as pl
from jax.experimental.pallas import tpu as pltpu

D_MODEL = 1024
HEAD_DIM = 64
N_Q_HEADS = D_MODEL // HEAD_DIM
N_KV_HEADS = N_Q_HEADS // 4
GROUP = N_Q_HEADS // N_KV_HEADS
D_CONV = D_MODEL
WINDOW = 128
BLOCK = 128
D_FF = ((8 * D_MODEL // 3 + 127) // 128) * 128
EPS = 1e-6
NEG_INF = -1e30
LOG2_E = math.log2(math.e)

OFF_CB = 0
OFF_CC = OFF_CB + D_CONV
OFF_CU = OFF_CC + D_CONV
OFF_Q = OFF_CU + D_CONV
OFF_K = OFF_Q + N_Q_HEADS * HEAD_DIM
OFF_V = OFF_K + N_KV_HEADS * HEAD_DIM
OFF_GA = OFF_V + N_KV_HEADS * HEAD_DIM
OFF_GB = OFF_GA + D_MODEL
D_IN = OFF_GB + D_MODEL

KV_W = N_KV_HEADS * HEAD_DIM
SUBLANES = 8
BF16_ROWS = 16
VT_ROWS = HEAD_DIM + BF16_ROWS
SEQ_TILE = 512
FF_CHUNK = 256
SIDE_COLS = 256
WEIGHT_CHUNK_ROWS = 64
WEIGHT_DMA_DEPTH = 4
VMEM_LIMIT_BYTES = 56 * 1024 * 1024

ALIBI_SLOPES = [2.0 ** (-8.0 * (h + 1) / N_Q_HEADS) for h in range(N_Q_HEADS)]


def _rms_mod(xv, gain, scale, shift):
    ms = jnp.mean(xv * xv, axis=-1, keepdims=True)
    y = xv * lax.rsqrt(ms + EPS) * gain
    return y * (1.0 + scale) + shift


def _dwconv3(v_ext, w):
    rows = v_ext.shape[0]
    t = rows - 2 * SUBLANES
    prev = pltpu.roll(v_ext, 1, 0)[SUBLANES:SUBLANES + t]
    nxt = pltpu.roll(v_ext, rows - 1, 0)[SUBLANES:SUBLANES + t]
    return w[0:1, :] * prev + w[1:2, :] * v_ext[SUBLANES:SUBLANES + t] + w[2:3, :] * nxt


def _stage_weights(src_hbm, dst_ref, chunk_rows):
    rows, cols = src_hbm.shape
    n_chunks = rows // chunk_rows
    assert n_chunks * chunk_rows == rows and n_chunks >= WEIGHT_DMA_DEPTH

    def body(stage_ref, sem_ref):
        def copy(c):
            slot = lax.rem(c, WEIGHT_DMA_DEPTH)
            r0 = pl.multiple_of(c * chunk_rows, chunk_rows)
            return pltpu.make_async_copy(src_hbm.at[pl.ds(r0, chunk_rows), :],
                                         stage_ref.at[slot], sem_ref.at[slot])

        for c in range(WEIGHT_DMA_DEPTH - 1):
            copy(c).start()

        def step(c, carry):
            @pl.when(c + WEIGHT_DMA_DEPTH - 1 < n_chunks)
            def _():
                copy(c + WEIGHT_DMA_DEPTH - 1).start()

            copy(c).wait()
            r0 = pl.multiple_of(c * chunk_rows, chunk_rows)
            dst_ref[pl.ds(r0, chunk_rows), :] = (
                stage_ref[lax.rem(c, WEIGHT_DMA_DEPTH)].astype(jnp.bfloat16))
            return carry

        lax.fori_loop(0, n_chunks, step, 0)

    pl.run_scoped(body, pltpu.VMEM((WEIGHT_DMA_DEPTH, chunk_rows, cols), jnp.float32),
                  pltpu.SemaphoreType.DMA((WEIGHT_DMA_DEPTH,)))


def _ada_kernel(c_ref, w_ref, b_ref, o_ref):
    c = c_ref[...]
    act = c * jax.nn.sigmoid(c)
    o_ref[...] = jnp.dot(act.astype(jnp.bfloat16), w_ref[...].astype(jnp.bfloat16),
                         preferred_element_type=jnp.float32) + b_ref[...]


def _modulation(c, w_ada, b_ada):
    b, d = c.shape
    n = w_ada.shape[1]
    rows = SUBLANES
    c_pad = jnp.zeros((rows, d), c.dtype).at[:b].set(c)
    tn = d
    out = pl.pallas_call(
        _ada_kernel,
        grid=(n // tn,),
        in_specs=[
            pl.BlockSpec((rows, d), lambda j: (0, 0)),
            pl.BlockSpec((d, tn), lambda j: (0, j)),
            pl.BlockSpec((1, tn), lambda j: (0, j)),
        ],
        out_specs=pl.BlockSpec((rows, tn), lambda j: (0, j)),
        out_shape=jax.ShapeDtypeStruct((rows, n), jnp.float32),
        compiler_params=pltpu.CompilerParams(dimension_semantics=("arbitrary",)),
        name="ada_modulation",
    )(c_pad, w_ada, b_ada.reshape(1, n))
    return out[:b]


def _group_mean_sq(v, e_ref):
    sq = (v * v).astype(jnp.bfloat16)
    return jnp.dot(sq, e_ref[...], preferred_element_type=jnp.float32) * (1.0 / HEAD_DIM)


def _mixer_kernel(sink_ref,
                  x_ref, xp_ref, xn_ref,
                  mod_ref,
                  g1n_ref, win_hbm, convw_ref, qg_ref, kg_ref, wo_hbm,
                  o_ref,
                  win_ref,
                  wo_ref,
                  wvt_ref,
                  h_ref,
                  hc_ref,
                  k2_ref,
                  vt_ref,
                  qn_ref,
                  ya_ref,
                  sgb_ref,
                  ybt_ref,
                  mrg_ref,
                  bias_ref,
                  e_ref):
    t = x_ref.shape[0]
    i = pl.program_id(1)
    n_tiles = pl.num_programs(1)
    first = jnp.logical_and(pl.program_id(0) == 0, i == 0)

    @pl.when(first)
    def _():
        _stage_weights(win_hbm, win_ref, WEIGHT_CHUNK_ROWS)
        _stage_weights(wo_hbm, wo_ref, WEIGHT_CHUNK_ROWS)
        wvt_ref[...] = win_ref[:, OFF_V:OFF_V + KV_W].astype(jnp.float32).T.astype(jnp.bfloat16)
        kj = lax.broadcasted_iota(jnp.int32, (BLOCK, BLOCK), 0)
        qi = lax.broadcasted_iota(jnp.int32, (BLOCK, BLOCK), 1)
        for blk in range(3):
            dist = jnp.abs(qi + BLOCK - (kj + blk * BLOCK))
            distf = dist.astype(jnp.float32)
            for h in range(N_Q_HEADS):
                bias_ref[3 * h + blk] = jnp.where(dist <= WINDOW,
                                                  -(ALIBI_SLOPES[h] * LOG2_E) * distf, NEG_INF)
        bias_ref[3 * N_Q_HEADS] = jnp.full((BLOCK, BLOCK), NEG_INF, jnp.float32)
        er = lax.broadcasted_iota(jnp.int32, (KV_W, KV_W), 0) // HEAD_DIM
        ec = lax.broadcasted_iota(jnp.int32, (KV_W, KV_W), 1) // HEAD_DIM
        e_ref[...] = jnp.where(er == ec, 1.0, 0.0).astype(jnp.bfloat16)

    shift1 = mod_ref[0:1, :]
    scale1 = mod_ref[1:2, :]
    gate1 = mod_ref[2:3, :]
    gain = g1n_ref[...]

    h_tile = _rms_mod(x_ref[...], gain, scale1, shift1)
    h_ref[BLOCK:BLOCK + t, :] = h_tile.astype(jnp.bfloat16)
    ht = h_ref[BLOCK:BLOCK + t, :]
    q = jnp.dot(ht, win_ref[:, OFF_Q:OFF_Q + D_MODEL], preferred_element_type=jnp.float32)

    h_prev = _rms_mod(xp_ref[...], gain, scale1, shift1)
    h_next = _rms_mod(xn_ref[...], gain, scale1, shift1)
    h_prev = jnp.where(i > 0, h_prev, 0.0)
    h_next = jnp.where(i < n_tiles - 1, h_next, 0.0)
    h_ref[0:BLOCK, :] = h_prev.astype(jnp.bfloat16)
    h_ref[BLOCK + t:, :] = h_next.astype(jnp.bfloat16)
    hc_ref[...] = jnp.concatenate(
        [h_prev[BLOCK - SUBLANES:], h_tile, h_next[:SUBLANES]], axis=0).astype(jnp.bfloat16)

    k = jnp.dot(h_ref[...], win_ref[:, OFF_K:OFF_K + KV_W], preferred_element_type=jnp.float32)
    vt = lax.dot_general(wvt_ref[...], h_ref[...], (((1,), (1,)), ((), ())),
                         preferred_element_type=jnp.float32)

    cb = jnp.dot(ht, win_ref[:, OFF_CB:OFF_CB + D_CONV], preferred_element_type=jnp.float32)
    ga = jnp.dot(ht, win_ref[:, OFF_GA:OFF_GA + D_MODEL], preferred_element_type=jnp.float32)
    ya_ref[...] = jax.nn.sigmoid(ga) * cb

    kn = k * lax.rsqrt(_group_mean_sq(k, e_ref) + EPS) * kg_ref[...]
    lane = lax.broadcasted_iota(jnp.int32, (1, 2 * HEAD_DIM), 1)
    lower = lane < HEAD_DIM
    for pair in range(N_KV_HEADS // 2):
        tile = kn[:, pair * 2 * HEAD_DIM:(pair + 1) * 2 * HEAD_DIM]
        swapped = pltpu.roll(tile, HEAD_DIM, 1)
        k2_ref[2 * pair] = jnp.where(lower, tile, swapped).astype(jnp.bfloat16)
        k2_ref[2 * pair + 1] = jnp.where(lower, swapped, tile).astype(jnp.bfloat16)

    ones_rows = jnp.where(
        lax.broadcasted_iota(jnp.int32, (VT_ROWS - HEAD_DIM, t + 2 * BLOCK), 0) == 0, 1.0, 0.0)
    for g in range(N_KV_HEADS):
        vt_ref[g, 0:HEAD_DIM, :] = vt[g * HEAD_DIM:(g + 1) * HEAD_DIM, :].astype(jnp.bfloat16)
        vt_ref[g, HEAD_DIM:, :] = ones_rows.astype(jnp.bfloat16)

    qscale = qg_ref[...] * (LOG2_E / math.sqrt(HEAD_DIM))
    for c in range(D_MODEL // KV_W):
        qc = q[:, c * KV_W:(c + 1) * KV_W]
        qn_ref[:, c * KV_W:(c + 1) * KV_W] = (
            qc * lax.rsqrt(_group_mean_sq(qc, e_ref) + EPS) * qscale[:, c * KV_W:(c + 1) * KV_W]
        ).astype(jnp.bfloat16)

    def conv_block(c):
        cs = slice(c * SIDE_COLS, (c + 1) * SIDE_COLS)
        hc = hc_ref[...]
        cc = jnp.dot(hc, win_ref[:, OFF_CC + c * SIDE_COLS:OFF_CC + (c + 1) * SIDE_COLS],
                     preferred_element_type=jnp.float32)
        cu = jnp.dot(hc, win_ref[:, OFF_CU + c * SIDE_COLS:OFF_CU + (c + 1) * SIDE_COLS],
                     preferred_element_type=jnp.float32)
        ya_ref[:, cs] = ya_ref[:, cs] * _dwconv3(cc * cu, convw_ref[:, cs])

    def gate_b_block(c):
        cs = slice(c * SIDE_COLS, (c + 1) * SIDE_COLS)
        gb = jnp.dot(ht, win_ref[:, OFF_GB + c * SIDE_COLS:OFF_GB + (c + 1) * SIDE_COLS],
                     preferred_element_type=jnp.float32)
        sgb_ref[:, cs] = jax.nn.sigmoid(gb)

    side_jobs = []
    for c in range(D_MODEL // SIDE_COLS):
        side_jobs.append(lambda c=c: conv_block(c))
        side_jobs.append(lambda c=c: gate_b_block(c))

    qlower = lax.broadcasted_iota(jnp.int32, (BLOCK, 2 * HEAD_DIM), 1) < HEAD_DIM
    zero_q = jnp.zeros((BLOCK, 2 * HEAD_DIM), jnp.bfloat16)
    mask_blk = 3 * N_Q_HEADS

    def scores_t(qb, g):
        r0 = qb * BLOCK
        kb = k2_ref[g, r0:r0 + 3 * BLOCK, :]
        qs = []
        for half in range(GROUP // 2):
            c0 = (g * GROUP + 2 * half) * HEAD_DIM
            pair = qn_ref[r0:r0 + BLOCK, c0:c0 + 2 * HEAD_DIM]
            qs.append(jnp.where(qlower, pair, zero_q))
            qs.append(jnp.where(qlower, zero_q, pair))
        q4 = jnp.concatenate(qs, axis=0)
        return lax.dot_general(kb, q4, (((1,), (1,)), ((), ())),
                               preferred_element_type=jnp.float32)

    def attend(qb, g, st):
        r0 = qb * BLOCK
        ps = []
        sink_terms = []
        for j in range(GROUP):
            hh = g * GROUP + j
            idx = [3 * hh, 3 * hh + 1, 3 * hh + 2]
            if qb == 0:
                idx[0] = jnp.where(i == 0, mask_blk, idx[0])
            if qb == t // BLOCK - 1:
                idx[2] = jnp.where(i == n_tiles - 1, mask_blk, idx[2])
            s = jnp.concatenate(
                [st[blk * BLOCK:(blk + 1) * BLOCK, j * BLOCK:(j + 1) * BLOCK] + bias_ref[idx[blk]]
                 for blk in range(3)], axis=0)
            sink = sink_ref[hh] * LOG2_E
            m = jnp.maximum(jnp.max(s, axis=0, keepdims=True), sink)
            ps.append(jnp.exp2(s - m).astype(jnp.bfloat16))
            sink_terms.append(jnp.exp2(sink - m))
        p4 = jnp.concatenate(ps, axis=1)
        ot = jnp.dot(vt_ref[g, :, r0:r0 + 3 * BLOCK], p4,
                     preferred_element_type=jnp.float32)
        den = ot[HEAD_DIM:HEAD_DIM + 1, :] + jnp.concatenate(sink_terms, axis=1)
        y = ot[0:HEAD_DIM, :] / den
        for j in range(GROUP):
            hh = g * GROUP + j
            ybt_ref[hh * HEAD_DIM:(hh + 1) * HEAD_DIM, r0:r0 + BLOCK] = y[:, j * BLOCK:(j + 1) * BLOCK]

    stages = [(qb, g) for qb in range(t // BLOCK) for g in range(N_KV_HEADS)]
    st_next = scores_t(*stages[0])
    for n, (qb, g) in enumerate(stages):
        st = st_next
        if n + 1 < len(stages):
            st_next = scores_t(*stages[n + 1])
        if n % 2 == 0 and side_jobs:
            side_jobs.pop(0)()
        attend(qb, g, st)
    for job in side_jobs:
        job()

    yb = ybt_ref[...].T
    mrg_ref[...] = (ya_ref[...] + sgb_ref[...] * yb).astype(jnp.bfloat16)
    out = jnp.dot(mrg_ref[...], wo_ref[...], preferred_element_type=jnp.float32)
    o_ref[...] = x_ref[...] + gate1 * out


def _token_mixer(x, mod, norm_gain, w_in, conv_w, q_gain, k_gain, sink, w_o):
    b, s, d = x.shape
    t = SEQ_TILE
    assert s % t == 0 and t % BLOCK == 0
    n_tiles = s // t
    bpt = t // BLOCK
    n_blocks = s // BLOCK

    const = lambda shape: pl.BlockSpec(shape, lambda bi, i, *_: (0,) * len(shape),
                                       pipeline_mode=pl.Buffered(1))
    grid_spec = pltpu.PrefetchScalarGridSpec(
        num_scalar_prefetch=1,
        grid=(b, n_tiles),
        in_specs=[
            pl.BlockSpec((None, t, d), lambda bi, i, *_: (bi, i, 0)),
            pl.BlockSpec((None, BLOCK, d), lambda bi, i, *_: (bi, jnp.maximum(i * bpt - 1, 0), 0)),
            pl.BlockSpec((None, BLOCK, d),
                         lambda bi, i, *_: (bi, jnp.minimum((i + 1) * bpt, n_blocks - 1), 0)),
            pl.BlockSpec((None, 6, d), lambda bi, i, *_: (bi, 0, 0)),
            const((1, d)),
            pl.BlockSpec(memory_space=pl.ANY),
            const((3, D_CONV)),
            const((1, d)),
            const((1, KV_W)),
            pl.BlockSpec(memory_space=pl.ANY),
        ],
        out_specs=pl.BlockSpec((None, t, d), lambda bi, i, *_: (bi, i, 0)),
        scratch_shapes=[
            pltpu.VMEM((d, D_IN), jnp.bfloat16),
            pltpu.VMEM((d, d), jnp.bfloat16),
            pltpu.VMEM((KV_W, d), jnp.bfloat16),
            pltpu.VMEM((t + 2 * BLOCK, d), jnp.bfloat16),
            pltpu.VMEM((t + 2 * SUBLANES, d), jnp.bfloat16),
            pltpu.VMEM((N_KV_HEADS, t + 2 * BLOCK, 2 * HEAD_DIM), jnp.bfloat16),
            pltpu.VMEM((N_KV_HEADS, VT_ROWS, t + 2 * BLOCK), jnp.bfloat16),
            pltpu.VMEM((t, d), jnp.bfloat16),
            pltpu.VMEM((t, d), jnp.float32),
            pltpu.VMEM((t, d), jnp.float32),
            pltpu.VMEM((d, t), jnp.float32),
            pltpu.VMEM((t, d), jnp.bfloat16),
            pltpu.VMEM((3 * N_Q_HEADS + 1, BLOCK, BLOCK), jnp.float32),
            pltpu.VMEM((KV_W, KV_W), jnp.bfloat16),
        ],
    )
    return pl.pallas_call(
        _mixer_kernel,
        grid_spec=grid_spec,
        out_shape=jax.ShapeDtypeStruct((b, s, d), jnp.float32),
        compiler_params=pltpu.CompilerParams(
            dimension_semantics=("arbitrary", "arbitrary"),
            vmem_limit_bytes=VMEM_LIMIT_BYTES),
        name="token_mixer",
    )(sink, x, x, x, mod, norm_gain.reshape(1, d), w_in, conv_w,
      jnp.tile(q_gain, N_Q_HEADS).reshape(1, d), jnp.tile(k_gain, N_KV_HEADS).reshape(1, KV_W), w_o)


def _ffn_kernel(x_ref, xp_ref, xn_ref,
                mod_ref, g2n_ref, wup_hbm, fcw_ref, fcb_ref, wdn_hbm,
                o_ref,
                wup_ref,
                wdn_ref,
                ht_ref,
                he_ref,
                act_ref):
    t = x_ref.shape[0]
    i = pl.program_id(1)
    n_tiles = pl.num_programs(1)

    @pl.when(jnp.logical_and(pl.program_id(0) == 0, i == 0))
    def _():
        _stage_weights(wup_hbm, wup_ref, WEIGHT_CHUNK_ROWS)
        _stage_weights(wdn_hbm, wdn_ref, WEIGHT_CHUNK_ROWS)

    shift2 = mod_ref[3:4, :]
    scale2 = mod_ref[4:5, :]
    gate2 = mod_ref[5:6, :]
    gain = g2n_ref[...]

    h_prev = _rms_mod(xp_ref[...], gain, scale2, shift2)
    h_next = _rms_mod(xn_ref[...], gain, scale2, shift2)
    h_prev = jnp.where(i > 0, h_prev, 0.0)
    h_next = jnp.where(i < n_tiles - 1, h_next, 0.0)
    h_tile = _rms_mod(x_ref[...], gain, scale2, shift2)
    ht_ref[...] = h_tile.astype(jnp.bfloat16)
    he_ref[...] = jnp.concatenate([h_prev, h_tile, h_next], axis=0).astype(jnp.bfloat16)

    for c in range(D_FF // FF_CHUNK):
        c0 = c * FF_CHUNK
        a_ext = jnp.dot(he_ref[...], wup_ref[:, c0:c0 + FF_CHUNK],
                        preferred_element_type=jnp.float32)
        gbr = jnp.dot(ht_ref[...], wup_ref[:, D_FF + c0:D_FF + c0 + FF_CHUNK],
                      preferred_element_type=jnp.float32)
        a = _dwconv3(a_ext, fcw_ref[:, c0:c0 + FF_CHUNK]) + fcb_ref[:, c0:c0 + FF_CHUNK]
        gelu = 0.5 * a * (1.0 + jnp.tanh(math.sqrt(2.0 / math.pi) * (a + 0.044715 * (a * a * a))))
        act_ref[:, c0:c0 + FF_CHUNK] = (gelu * gbr).astype(jnp.bfloat16)

    y = jnp.dot(act_ref[...], wdn_ref[...], preferred_element_type=jnp.float32)
    o_ref[...] = x_ref[...] + gate2 * y


def _conv_ffn(x, mod, norm_gain, w_up, ffn_conv_w, ffn_conv_b, w_down):
    b, s, d = x.shape
    t = SEQ_TILE
    assert s % t == 0 and D_FF % FF_CHUNK == 0
    n_tiles = s // t
    rpt = t // SUBLANES
    n_groups = s // SUBLANES

    const = lambda shape: pl.BlockSpec(shape, lambda bi, i: (0,) * len(shape),
                                       pipeline_mode=pl.Buffered(1))
    return pl.pallas_call(
        _ffn_kernel,
        grid=(b, n_tiles),
        in_specs=[
            pl.BlockSpec((None, t, d), lambda bi, i: (bi, i, 0)),
            pl.BlockSpec((None, SUBLANES, d), lambda bi, i: (bi, jnp.maximum(i * rpt - 1, 0), 0)),
            pl.BlockSpec((None, SUBLANES, d),
                         lambda bi, i: (bi, jnp.minimum((i + 1) * rpt, n_groups - 1), 0)),
            pl.BlockSpec((None, 6, d), lambda bi, i: (bi, 0, 0)),
            const((1, d)),
            pl.BlockSpec(memory_space=pl.ANY),
            const((3, D_FF)),
            const((1, D_FF)),
            pl.BlockSpec(memory_space=pl.ANY),
        ],
        out_specs=pl.BlockSpec((None, t, d), lambda bi, i: (bi, i, 0)),
        out_shape=jax.ShapeDtypeStruct((b, s, d), jnp.float32),
        scratch_shapes=[
            pltpu.VMEM((d, 2 * D_FF), jnp.bfloat16),
            pltpu.VMEM((D_FF, d), jnp.bfloat16),
            pltpu.VMEM((t, d), jnp.bfloat16),
            pltpu.VMEM((t + 2 * SUBLANES, d), jnp.bfloat16),
            pltpu.VMEM((t, D_FF), jnp.bfloat16),
        ],
        compiler_params=pltpu.CompilerParams(
            dimension_semantics=("arbitrary", "arbitrary"),
            vmem_limit_bytes=VMEM_LIMIT_BYTES),
        name="conv_ffn",
    )(x, x, x, mod, norm_gain.reshape(1, d), w_up, ffn_conv_w, ffn_conv_b.reshape(1, D_FF), w_down)


def kernel(x, c, w_ada, b_ada, norm1_gain, w_in, conv_w, q_gain, k_gain, sink, w_o, norm2_gain,
           w_up, ffn_conv_w, ffn_conv_b, w_down):
    depth = w_ada.shape[0]
    b, s, d = x.shape
    for l in range(depth):
        mod = _modulation(c, w_ada[l], b_ada[l]).reshape(b, 6, d)
        x = _token_mixer(x, mod, norm1_gain[l], w_in[l], conv_w[l], q_gain[l], k_gain[l],
                         sink[l], w_o[l])
        x = _conv_ffn(x, mod, norm2_gain[l], w_up[l], ffn_conv_w[l], ffn_conv_b[l], w_down[l])
    return x
```

```python
import math

import jax
import jax.numpy as jnp
from jax import lax
from jax.experimental import pallas as pl
from jax.experimental.pallas import tpu as pltpu

D_MODEL = 1024
HEAD_DIM = 64
N_Q_HEADS = D_MODEL // HEAD_DIM
N_KV_HEADS = N_Q_HEADS // 4
GROUP = N_Q_HEADS // N_KV_HEADS
D_CONV = D_MODEL
WINDOW = 128
BLOCK = 128
D_FF = ((8 * D_MODEL // 3 + 127) // 128) * 128
EPS = 1e-6
NEG_INF = -1e30
LOG2_E = math.log2(math.e)

OFF_CB = 0
OFF_CC = OFF_CB + D_CONV
OFF_CU = OFF_CC + D_CONV
OFF_Q = OFF_CU + D_CONV
OFF_K = OFF_Q + N_Q_HEADS * HEAD_DIM
OFF_V = OFF_K + N_KV_HEADS * HEAD_DIM
OFF_GA = OFF_V + N_KV_HEADS * HEAD_DIM
OFF_GB = OFF_GA + D_MODEL
D_IN = OFF_GB + D_MODEL

KV_W = N_KV_HEADS * HEAD_DIM
SUBLANES = 8
BF16_ROWS = 16
VT_ROWS = HEAD_DIM + BF16_ROWS
SEQ_TILE = 512
FF_CHUNK = 256
SIDE_COLS = 256
WEIGHT_CHUNK_BYTES = 2 * 1024 * 1024
WEIGHT_DMA_DEPTH = 4
VMEM_LIMIT_BYTES = 56 * 1024 * 1024

ALIBI_SLOPES = [2.0 ** (-8.0 * (h + 1) / N_Q_HEADS) for h in range(N_Q_HEADS)]


def _rms_mod(xv, gain, scale, shift):
    ms = jnp.mean(xv * xv, axis=-1, keepdims=True)
    return xv * lax.rsqrt(ms + EPS) * (gain * (1.0 + scale)) + shift


def _dwconv3(v_ext, w):
    rows = v_ext.shape[0]
    t = rows - 2 * SUBLANES
    prev = pltpu.roll(v_ext, 1, 0)[SUBLANES:SUBLANES + t]
    nxt = pltpu.roll(v_ext, rows - 1, 0)[SUBLANES:SUBLANES + t]
    return w[0:1, :] * prev + w[1:2, :] * v_ext[SUBLANES:SUBLANES + t] + w[2:3, :] * nxt


def _stage_weights(src_hbm, dst_ref):
    rows, cols = src_hbm.shape
    chunk_rows = max(r for r in range(BF16_ROWS, rows + 1, BF16_ROWS)
                     if rows % r == 0 and r * cols * 4 <= WEIGHT_CHUNK_BYTES
                     and rows // r >= WEIGHT_DMA_DEPTH)
    n_chunks = rows // chunk_rows

    def body(stage_ref, sem_ref):
        def copy(c):
            slot = lax.rem(c, WEIGHT_DMA_DEPTH)
            r0 = pl.multiple_of(c * chunk_rows, chunk_rows)
            return pltpu.make_async_copy(src_hbm.at[pl.ds(r0, chunk_rows), :],
                                         stage_ref.at[slot], sem_ref.at[slot])

        for c in range(WEIGHT_DMA_DEPTH - 1):
            copy(c).start()

        def step(c, carry):
            @pl.when(c + WEIGHT_DMA_DEPTH - 1 < n_chunks)
            def _():
                copy(c + WEIGHT_DMA_DEPTH - 1).start()

            copy(c).wait()
            r0 = pl.multiple_of(c * chunk_rows, chunk_rows)
            dst_ref[pl.ds(r0, chunk_rows), :] = (
                stage_ref[lax.rem(c, WEIGHT_DMA_DEPTH)].astype(jnp.bfloat16))
            return carry

        lax.fori_loop(0, n_chunks, step, 0)

    pl.run_scoped(body, pltpu.VMEM((WEIGHT_DMA_DEPTH, chunk_rows, cols), jnp.float32),
                  pltpu.SemaphoreType.DMA((WEIGHT_DMA_DEPTH,)))


def _ada_kernel(c_ref, w_ref, b_ref, o_ref):
    c = c_ref[...]
    act = c * jax.nn.sigmoid(c)
    o_ref[...] = jnp.dot(act.astype(jnp.bfloat16), w_ref[...].astype(jnp.bfloat16),
                         preferred_element_type=jnp.float32) + b_ref[...]


def _modulation(c, w_ada, b_ada):
    b, d = c.shape
    n = w_ada.shape[1]
    rows = SUBLANES
    c_pad = jnp.zeros((rows, d), c.dtype).at[:b].set(c)
    tn = d
    out = pl.pallas_call(
        _ada_kernel,
        grid=(n // tn,),
        in_specs=[
            pl.BlockSpec((rows, d), lambda j: (0, 0)),
            pl.BlockSpec((d, tn), lambda j: (0, j)),
            pl.BlockSpec((1, tn), lambda j: (0, j)),
        ],
        out_specs=pl.BlockSpec((rows, tn), lambda j: (0, j)),
        out_shape=jax.ShapeDtypeStruct((rows, n), jnp.float32),
        compiler_params=pltpu.CompilerParams(dimension_semantics=("arbitrary",)),
        name="ada_modulation",
    )(c_pad, w_ada, b_ada.reshape(1, n))
    return out[:b]


def _group_mean_sq(v, e_ref):
    sq = (v * v).astype(jnp.bfloat16)
    return jnp.dot(sq, e_ref[...], preferred_element_type=jnp.float32) * (1.0 / HEAD_DIM)


def _mixer_kernel(sink_ref,
                  x_ref, xp_ref, xn_ref,
                  mod_ref,
                  g1n_ref, win_hbm, convw_ref, qg_ref, kg_ref, wo_hbm,
                  o_ref,
                  win_ref,
                  wo_ref,
                  wvt_ref,
                  h_ref,
                  hc_ref,
                  k2_ref,
                  vt_ref,
                  qn_ref,
                  ya_ref,
                  sgb_ref,
                  ybt_ref,
                  mrg_ref,
                  bias_ref,
                  e_ref):
    t = x_ref.shape[0]
    i = pl.program_id(1)
    n_tiles = pl.num_programs(1)
    first = jnp.logical_and(pl.program_id(0) == 0, i == 0)

    @pl.when(first)
    def _():
        _stage_weights(win_hbm, win_ref)
        _stage_weights(wo_hbm, wo_ref)
        wvt_ref[...] = win_ref[:, OFF_V:OFF_V + KV_W].astype(jnp.float32).T.astype(jnp.bfloat16)
        kj = lax.broadcasted_iota(jnp.int32, (BLOCK, BLOCK), 0)
        qi = lax.broadcasted_iota(jnp.int32, (BLOCK, BLOCK), 1)
        for blk in range(3):
            dist = jnp.abs(qi + BLOCK - (kj + blk * BLOCK))
            distf = dist.astype(jnp.float32)
            for h in range(N_Q_HEADS):
                bias_ref[3 * h + blk] = jnp.where(dist <= WINDOW,
                                                  -(ALIBI_SLOPES[h] * LOG2_E) * distf, NEG_INF)
        bias_ref[3 * N_Q_HEADS] = jnp.full((BLOCK, BLOCK), NEG_INF, jnp.float32)
        er = lax.broadcasted_iota(jnp.int32, (KV_W, KV_W), 0) // HEAD_DIM
        ec = lax.broadcasted_iota(jnp.int32, (KV_W, KV_W), 1) // HEAD_DIM
        e_ref[...] = jnp.where(er == ec, 1.0, 0.0).astype(jnp.bfloat16)

    shift1 = mod_ref[0:1, :]
    scale1 = mod_ref[1:2, :]
    gate1 = mod_ref[2:3, :]
    gain = g1n_ref[...]

    h_tile = _rms_mod(x_ref[...], gain, scale1, shift1)
    h_ref[BLOCK:BLOCK + t, :] = h_tile.astype(jnp.bfloat16)
    ht = h_ref[BLOCK:BLOCK + t, :]
    q = jnp.dot(ht, win_ref[:, OFF_Q:OFF_Q + D_MODEL], preferred_element_type=jnp.float32)

    h_prev = _rms_mod(xp_ref[...], gain, scale1, shift1)
    h_next = _rms_mod(xn_ref[...], gain, scale1, shift1)
    h_prev = jnp.where(i > 0, h_prev, 0.0)
    h_next = jnp.where(i < n_tiles - 1, h_next, 0.0)
    h_ref[0:BLOCK, :] = h_prev.astype(jnp.bfloat16)
    h_ref[BLOCK + t:, :] = h_next.astype(jnp.bfloat16)
    hc_ref[...] = jnp.concatenate(
        [h_prev[BLOCK - SUBLANES:], h_tile, h_next[:SUBLANES]], axis=0).astype(jnp.bfloat16)

    k = jnp.dot(h_ref[...], win_ref[:, OFF_K:OFF_K + KV_W], preferred_element_type=jnp.float32)
    vt = lax.dot_general(wvt_ref[...], h_ref[...], (((1,), (1,)), ((), ())),
                         preferred_element_type=jnp.float32)

    cb = jnp.dot(ht, win_ref[:, OFF_CB:OFF_CB + D_CONV], preferred_element_type=jnp.float32)
    ga = jnp.dot(ht, win_ref[:, OFF_GA:OFF_GA + D_MODEL], preferred_element_type=jnp.float32)
    ya_ref[...] = jax.nn.sigmoid(ga) * cb

    kn = k * lax.rsqrt(_group_mean_sq(k, e_ref) + EPS) * kg_ref[...]
    lane = lax.broadcasted_iota(jnp.int32, (1, 2 * HEAD_DIM), 1)
    lower = lane < HEAD_DIM
    for pair in range(N_KV_HEADS // 2):
        tile = kn[:, pair * 2 * HEAD_DIM:(pair + 1) * 2 * HEAD_DIM]
        swapped = pltpu.roll(tile, HEAD_DIM, 1)
        k2_ref[2 * pair] = jnp.where(lower, tile, swapped).astype(jnp.bfloat16)
        k2_ref[2 * pair + 1] = jnp.where(lower, swapped, tile).astype(jnp.bfloat16)

    ones_rows = jnp.where(
        lax.broadcasted_iota(jnp.int32, (VT_ROWS - HEAD_DIM, t + 2 * BLOCK), 0) == 0, 1.0, 0.0)
    for g in range(N_KV_HEADS):
        vt_ref[g, 0:HEAD_DIM, :] = vt[g * HEAD_DIM:(g + 1) * HEAD_DIM, :].astype(jnp.bfloat16)
        vt_ref[g, HEAD_DIM:, :] = ones_rows.astype(jnp.bfloat16)

    qscale = qg_ref[...] * (LOG2_E / math.sqrt(HEAD_DIM))
    for c in range(D_MODEL // KV_W):
        qc = q[:, c * KV_W:(c + 1) * KV_W]
        qn_ref[:, c * KV_W:(c + 1) * KV_W] = (
            qc * lax.rsqrt(_group_mean_sq(qc, e_ref) + EPS) * qscale[:, c * KV_W:(c + 1) * KV_W]
        ).astype(jnp.bfloat16)

    def conv_block(c):
        cs = slice(c * SIDE_COLS, (c + 1) * SIDE_COLS)
        hc = hc_ref[...]
        cc = jnp.dot(hc, win_ref[:, OFF_CC + c * SIDE_COLS:OFF_CC + (c + 1) * SIDE_COLS],
                     preferred_element_type=jnp.float32)
        cu = jnp.dot(hc, win_ref[:, OFF_CU + c * SIDE_COLS:OFF_CU + (c + 1) * SIDE_COLS],
                     preferred_element_type=jnp.float32)
        ya_ref[:, cs] = ya_ref[:, cs] * _dwconv3(cc * cu, convw_ref[:, cs])

    def gate_b_block(c):
        cs = slice(c * SIDE_COLS, (c + 1) * SIDE_COLS)
        gb = jnp.dot(ht, win_ref[:, OFF_GB + c * SIDE_COLS:OFF_GB + (c + 1) * SIDE_COLS],
                     preferred_element_type=jnp.float32)
        sgb_ref[:, cs] = jax.nn.sigmoid(gb)

    side_jobs = []
    for c in range(D_MODEL // SIDE_COLS):
        side_jobs.append(lambda c=c: conv_block(c))
        side_jobs.append(lambda c=c: gate_b_block(c))

    qlower = lax.broadcasted_iota(jnp.int32, (BLOCK, 2 * HEAD_DIM), 1) < HEAD_DIM
    zero_q = jnp.zeros((BLOCK, 2 * HEAD_DIM), jnp.bfloat16)
    mask_blk = 3 * N_Q_HEADS

    def scores_t(qb, g):
        r0 = qb * BLOCK
        kb = k2_ref[g, r0:r0 + 3 * BLOCK, :]
        qs = []
        for half in range(GROUP // 2):
            c0 = (g * GROUP + 2 * half) * HEAD_DIM
            pair = qn_ref[r0:r0 + BLOCK, c0:c0 + 2 * HEAD_DIM]
            qs.append(jnp.where(qlower, pair, zero_q))
            qs.append(jnp.where(qlower, zero_q, pair))
        q4 = jnp.concatenate(qs, axis=0)
        return lax.dot_general(kb, q4, (((1,), (1,)), ((), ())),
                               preferred_element_type=jnp.float32)

    def attend(qb, g, st):
        r0 = qb * BLOCK
        ps = []
        sink_terms = []
        for j in range(GROUP):
            hh = g * GROUP + j
            idx = [3 * hh, 3 * hh + 1, 3 * hh + 2]
            if qb == 0:
                idx[0] = jnp.where(i == 0, mask_blk, idx[0])
            if qb == t // BLOCK - 1:
                idx[2] = jnp.where(i == n_tiles - 1, mask_blk, idx[2])
            s = jnp.concatenate(
                [st[blk * BLOCK:(blk + 1) * BLOCK, j * BLOCK:(j + 1) * BLOCK] + bias_ref[idx[blk]]
                 for blk in range(3)], axis=0)
            sink = sink_ref[hh] * LOG2_E
            m = jnp.maximum(jnp.max(s, axis=0, keepdims=True), sink)
            ps.append(jnp.exp2(s - m).astype(jnp.bfloat16))
            sink_terms.append(jnp.exp2(sink - m))
        p4 = jnp.concatenate(ps, axis=1)
        ot = jnp.dot(vt_ref[g, :, r0:r0 + 3 * BLOCK], p4,
                     preferred_element_type=jnp.float32)
        den = ot[HEAD_DIM:HEAD_DIM + 1, :] + jnp.concatenate(sink_terms, axis=1)
        y = ot[0:HEAD_DIM, :] / den
        for j in range(GROUP):
            hh = g * GROUP + j
            ybt_ref[hh * HEAD_DIM:(hh + 1) * HEAD_DIM, r0:r0 + BLOCK] = y[:, j * BLOCK:(j + 1) * BLOCK]

    stages = [(qb, g) for qb in range(t // BLOCK) for g in range(N_KV_HEADS)]
    st_next = scores_t(*stages[0])
    for n, (qb, g) in enumerate(stages):
        st = st_next
        if n + 1 < len(stages):
            st_next = scores_t(*stages[n + 1])
        if n % 2 == 0 and side_jobs:
            side_jobs.pop(0)()
        attend(qb, g, st)
    for job in side_jobs:
        job()

    yb = ybt_ref[...].T
    mrg_ref[...] = (ya_ref[...] + sgb_ref[...] * yb).astype(jnp.bfloat16)
    out = jnp.dot(mrg_ref[...], wo_ref[...], preferred_element_type=jnp.float32)
    o_ref[...] = x_ref[...] + gate1 * out


def _token_mixer(x, mod, norm_gain, w_in, conv_w, q_gain, k_gain, sink, w_o):
    b, s, d = x.shape
    t = SEQ_TILE
    assert s % t == 0 and t % BLOCK == 0
    n_tiles = s // t
    bpt = t // BLOCK
    n_blocks = s // BLOCK

    const = lambda shape: pl.BlockSpec(shape, lambda bi, i, *_: (0,) * len(shape),
                                       pipeline_mode=pl.Buffered(1))
    grid_spec = pltpu.PrefetchScalarGridSpec(
        num_scalar_prefetch=1,
        grid=(b, n_tiles),
        in_specs=[
            pl.BlockSpec((None, t, d), lambda bi, i, *_: (bi, i, 0)),
            pl.BlockSpec((None, BLOCK, d), lambda bi, i, *_: (bi, jnp.maximum(i * bpt - 1, 0), 0)),
            pl.BlockSpec((None, BLOCK, d),
                         lambda bi, i, *_: (bi, jnp.minimum((i + 1) * bpt, n_blocks - 1), 0)),
            pl.BlockSpec((None, 6, d), lambda bi, i, *_: (bi, 0, 0)),
            const((1, d)),
            pl.BlockSpec(memory_space=pl.ANY),
            const((3, D_CONV)),
            const((1, d)),
            const((1, KV_W)),
            pl.BlockSpec(memory_space=pl.ANY),
        ],
        out_specs=pl.BlockSpec((None, t, d), lambda bi, i, *_: (bi, i, 0)),
        scratch_shapes=[
            pltpu.VMEM((d, D_IN), jnp.bfloat16),
            pltpu.VMEM((d, d), jnp.bfloat16),
            pltpu.VMEM((KV_W, d), jnp.bfloat16),
            pltpu.VMEM((t + 2 * BLOCK, d), jnp.bfloat16),
            pltpu.VMEM((t + 2 * SUBLANES, d), jnp.bfloat16),
            pltpu.VMEM((N_KV_HEADS, t + 2 * BLOCK, 2 * HEAD_DIM), jnp.bfloat16),
            pltpu.VMEM((N_KV_HEADS, VT_ROWS, t + 2 * BLOCK), jnp.bfloat16),
            pltpu.VMEM((t, d), jnp.bfloat16),
            pltpu.VMEM((t, d), jnp.float32),
            pltpu.VMEM((t, d), jnp.float32),
            pltpu.VMEM((d, t), jnp.float32),
            pltpu.VMEM((t, d), jnp.bfloat16),
            pltpu.VMEM((3 * N_Q_HEADS + 1, BLOCK, BLOCK), jnp.float32),
            pltpu.VMEM((KV_W, KV_W), jnp.bfloat16),
        ],
    )
    return pl.pallas_call(
        _mixer_kernel,
        grid_spec=grid_spec,
        out_shape=jax.ShapeDtypeStruct((b, s, d), jnp.float32),
        compiler_params=pltpu.CompilerParams(
            dimension_semantics=("arbitrary", "arbitrary"),
            vmem_limit_bytes=VMEM_LIMIT_BYTES),
        name="token_mixer",
    )(sink, x, x, x, mod, norm_gain.reshape(1, d), w_in, conv_w,
      jnp.tile(q_gain, N_Q_HEADS).reshape(1, d), jnp.tile(k_gain, N_KV_HEADS).reshape(1, KV_W), w_o)


def _ffn_kernel(x_ref, xp_ref, xn_ref,
                mod_ref, g2n_ref, wup_hbm, fcw_ref, fcb_ref, wdn_hbm,
                o_ref,
                wup_ref,
                wdn_ref,
                ht_ref,
                he_ref,
                act_ref):
    t = x_ref.shape[0]
    i = pl.program_id(1)
    n_tiles = pl.num_programs(1)

    @pl.when(jnp.logical_and(pl.program_id(0) == 0, i == 0))
    def _():
        _stage_weights(wup_hbm, wup_ref)
        _stage_weights(wdn_hbm, wdn_ref)

    shift2 = mod_ref[3:4, :]
    scale2 = mod_ref[4:5, :]
    gate2 = mod_ref[5:6, :]
    gain = g2n_ref[...]

    h_prev = _rms_mod(xp_ref[...], gain, scale2, shift2)
    h_next = _rms_mod(xn_ref[...], gain, scale2, shift2)
    h_prev = jnp.where(i > 0, h_prev, 0.0)
    h_next = jnp.where(i < n_tiles - 1, h_next, 0.0)
    h_tile = _rms_mod(x_ref[...], gain, scale2, shift2)
    ht_ref[...] = h_tile.astype(jnp.bfloat16)
    he_ref[...] = jnp.concatenate([h_prev, h_tile, h_next], axis=0).astype(jnp.bfloat16)

    for c in range(D_FF // FF_CHUNK):
        c0 = c * FF_CHUNK
        a_ext = jnp.dot(he_ref[...], wup_ref[:, c0:c0 + FF_CHUNK],
                        preferred_element_type=jnp.float32)
        gbr = jnp.dot(ht_ref[...], wup_ref[:, D_FF + c0:D_FF + c0 + FF_CHUNK],
                      preferred_element_type=jnp.float32)
        a = _dwconv3(a_ext, fcw_ref[:, c0:c0 + FF_CHUNK]) + fcb_ref[:, c0:c0 + FF_CHUNK]
        gelu = 0.5 * a * (1.0 + jnp.tanh(math.sqrt(2.0 / math.pi) * (a + 0.044715 * (a * a * a))))
        act_ref[:, c0:c0 + FF_CHUNK] = (gelu * gbr).astype(jnp.bfloat16)

    y = jnp.dot(act_ref[...], wdn_ref[...], preferred_element_type=jnp.float32)
    o_ref[...] = x_ref[...] + gate2 * y


def _conv_ffn(x, mod, norm_gain, w_up, ffn_conv_w, ffn_conv_b, w_down):
    b, s, d = x.shape
    t = SEQ_TILE
    assert s % t == 0 and D_FF % FF_CHUNK == 0
    n_tiles = s // t
    rpt = t // SUBLANES
    n_groups = s // SUBLANES

    const = lambda shape: pl.BlockSpec(shape, lambda bi, i: (0,) * len(shape),
                                       pipeline_mode=pl.Buffered(1))
    return pl.pallas_call(
        _ffn_kernel,
        grid=(b, n_tiles),
        in_specs=[
            pl.BlockSpec((None, t, d), lambda bi, i: (bi, i, 0)),
            pl.BlockSpec((None, SUBLANES, d), lambda bi, i: (bi, jnp.maximum(i * rpt - 1, 0), 0)),
            pl.BlockSpec((None, SUBLANES, d),
                         lambda bi, i: (bi, jnp.minimum((i + 1) * rpt, n_groups - 1), 0)),
            pl.BlockSpec((None, 6, d), lambda bi, i: (bi, 0, 0)),
            const((1, d)),
            pl.BlockSpec(memory_space=pl.ANY),
            const((3, D_FF)),
            const((1, D_FF)),
            pl.BlockSpec(memory_space=pl.ANY),
        ],
        out_specs=pl.BlockSpec((None, t, d), lambda bi, i: (bi, i, 0)),
        out_shape=jax.ShapeDtypeStruct((b, s, d), jnp.float32),
        scratch_shapes=[
            pltpu.VMEM((d, 2 * D_FF), jnp.bfloat16),
            pltpu.VMEM((D_FF, d), jnp.bfloat16),
            pltpu.VMEM((t, d), jnp.bfloat16),
            pltpu.VMEM((t + 2 * SUBLANES, d), jnp.bfloat16),
            pltpu.VMEM((t, D_FF), jnp.bfloat16),
        ],
        compiler_params=pltpu.CompilerParams(
            dimension_semantics=("arbitrary", "arbitrary"),
            vmem_limit_bytes=VMEM_LIMIT_BYTES),
        name="conv_ffn",
    )(x, x, x, mod, norm_gain.reshape(1, d), w_up, ffn_conv_w, ffn_conv_b.reshape(1, D_FF), w_down)


def kernel(x, c, w_ada, b_ada, norm1_gain, w_in, conv_w, q_gain, k_gain, sink, w_o, norm2_gain,
           w_up, ffn_conv_w, ffn_conv_b, w_down):
    depth = w_ada.shape[0]
    b, s, d = x.shape
    for l in range(depth):
        mod = _modulation(c, w_ada[l], b_ada[l]).reshape(b, 6, d)
        x = _token_mixer(x, mod, norm1_gain[l], w_in[l], conv_w[l], q_gain[l], k_gain[l],
                         sink[l], w_o[l])
        x = _conv_ffn(x, mod, norm2_gain[l], w_up[l], ffn_conv_w[l], ffn_conv_b[l], w_down[l])
    return x
```

```python
import math

import jax
import jax.numpy as jnp
from jax import lax
from jax.experimental import pallas as pl
from jax.experimental.pallas import tpu as pltpu

D_MODEL = 1024
HEAD_DIM = 64
N_Q_HEADS = D_MODEL // HEAD_DIM
N_KV_HEADS = N_Q_HEADS // 4
GROUP = N_Q_HEADS // N_KV_HEADS
D_CONV = D_MODEL
WINDOW = 128
BLOCK = 128
D_FF = ((8 * D_MODEL // 3 + 127) // 128) * 128
EPS = 1e-6
NEG_INF = -1e30
LOG2_E = math.log2(math.e)

OFF_CB = 0
OFF_CC = OFF_CB + D_CONV
OFF_CU = OFF_CC + D_CONV
OFF_Q = OFF_CU + D_CONV
OFF_K = OFF_Q + N_Q_HEADS * HEAD_DIM
OFF_V = OFF_K + N_KV_HEADS * HEAD_DIM
OFF_GA = OFF_V + N_KV_HEADS * HEAD_DIM
OFF_GB = OFF_GA + D_MODEL
D_IN = OFF_GB + D_MODEL

KV_W = N_KV_HEADS * HEAD_DIM
SUBLANES = 8
BF16_ROWS = 16
VT_ROWS = HEAD_DIM + BF16_ROWS
SEQ_TILE = 512
FFN_TILE = 1024
FF_CHUNK = 256
SIDE_COLS = 256
WEIGHT_CHUNK_BYTES = 2 * 1024 * 1024
WEIGHT_DMA_DEPTH = 4
VMEM_LIMIT_BYTES = 56 * 1024 * 1024

ALIBI_SLOPES = [2.0 ** (-8.0 * (h + 1) / N_Q_HEADS) for h in range(N_Q_HEADS)]


def _rms_mod(xv, gain, scale, shift):
    ms = jnp.mean(xv * xv, axis=-1, keepdims=True)
    return xv * lax.rsqrt(ms + EPS) * (gain * (1.0 + scale)) + shift


def _dwconv3(v_ext, w):
    rows = v_ext.shape[0]
    t = rows - 2 * SUBLANES
    prev = pltpu.roll(v_ext, 1, 0)[SUBLANES:SUBLANES + t]
    nxt = pltpu.roll(v_ext, rows - 1, 0)[SUBLANES:SUBLANES + t]
    return w[0:1, :] * prev + w[1:2, :] * v_ext[SUBLANES:SUBLANES + t] + w[2:3, :] * nxt


def _stage_weights(src_hbm, dst_ref):
    rows, cols = src_hbm.shape
    chunk_rows = max(r for r in range(BF16_ROWS, rows + 1, BF16_ROWS)
                     if rows % r == 0 and r * cols * 4 <= WEIGHT_CHUNK_BYTES
                     and rows // r >= WEIGHT_DMA_DEPTH)
    n_chunks = rows // chunk_rows

    def body(stage_ref, sem_ref):
        def copy(c):
            slot = lax.rem(c, WEIGHT_DMA_DEPTH)
            r0 = pl.multiple_of(c * chunk_rows, chunk_rows)
            return pltpu.make_async_copy(src_hbm.at[pl.ds(r0, chunk_rows), :],
                                         stage_ref.at[slot], sem_ref.at[slot])

        for c in range(WEIGHT_DMA_DEPTH - 1):
            copy(c).start()

        def step(c, carry):
            @pl.when(c + WEIGHT_DMA_DEPTH - 1 < n_chunks)
            def _():
                copy(c + WEIGHT_DMA_DEPTH - 1).start()

            copy(c).wait()
            r0 = pl.multiple_of(c * chunk_rows, chunk_rows)
            dst_ref[pl.ds(r0, chunk_rows), :] = (
                stage_ref[lax.rem(c, WEIGHT_DMA_DEPTH)].astype(jnp.bfloat16))
            return carry

        lax.fori_loop(0, n_chunks, step, 0)

    pl.run_scoped(body, pltpu.VMEM((WEIGHT_DMA_DEPTH, chunk_rows, cols), jnp.float32),
                  pltpu.SemaphoreType.DMA((WEIGHT_DMA_DEPTH,)))


def _ada_kernel(c_ref, w_ref, b_ref, o_ref):
    c = c_ref[...]
    act = c * jax.nn.sigmoid(c)
    o_ref[...] = jnp.dot(act.astype(jnp.bfloat16), w_ref[...].astype(jnp.bfloat16),
                         preferred_element_type=jnp.float32) + b_ref[...]


def _modulation(c, w_ada, b_ada):
    b, d = c.shape
    n = w_ada.shape[1]
    rows = SUBLANES
    c_pad = jnp.zeros((rows, d), c.dtype).at[:b].set(c)
    tn = d
    out = pl.pallas_call(
        _ada_kernel,
        grid=(n // tn,),
        in_specs=[
            pl.BlockSpec((rows, d), lambda j: (0, 0)),
            pl.BlockSpec((d, tn), lambda j: (0, j)),
            pl.BlockSpec((1, tn), lambda j: (0, j)),
        ],
        out_specs=pl.BlockSpec((rows, tn), lambda j: (0, j)),
        out_shape=jax.ShapeDtypeStruct((rows, n), jnp.float32),
        compiler_params=pltpu.CompilerParams(dimension_semantics=("arbitrary",)),
        name="ada_modulation",
    )(c_pad, w_ada, b_ada.reshape(1, n))
    return out[:b]


def _group_mean_sq(v, e_ref):
    sq = (v * v).astype(jnp.bfloat16)
    return jnp.dot(sq, e_ref[...], preferred_element_type=jnp.float32) * (1.0 / HEAD_DIM)


def _mixer_kernel(sink_ref,
                  x_ref, xp_ref, xn_ref,
                  mod_ref,
                  g1n_ref, win_hbm, convw_ref, qg_ref, kg_ref, wo_hbm,
                  o_ref,
                  win_ref,
                  wo_ref,
                  wvt_ref,
                  h_ref,
                  hc_ref,
                  k2_ref,
                  vt_ref,
                  qn_ref,
                  ya_ref,
                  sgb_ref,
                  ybt_ref,
                  mrg_ref,
                  bias_ref,
                  e_ref):
    t = x_ref.shape[0]
    i = pl.program_id(1)
    n_tiles = pl.num_programs(1)
    first = jnp.logical_and(pl.program_id(0) == 0, i == 0)

    @pl.when(first)
    def _():
        _stage_weights(win_hbm, win_ref)
        _stage_weights(wo_hbm, wo_ref)
        wvt_ref[...] = win_ref[:, OFF_V:OFF_V + KV_W].astype(jnp.float32).T.astype(jnp.bfloat16)
        kj = lax.broadcasted_iota(jnp.int32, (BLOCK, BLOCK), 0)
        qi = lax.broadcasted_iota(jnp.int32, (BLOCK, BLOCK), 1)
        for blk in range(3):
            dist = jnp.abs(qi + BLOCK - (kj + blk * BLOCK))
            distf = dist.astype(jnp.float32)
            for h in range(N_Q_HEADS):
                bias_ref[3 * h + blk] = jnp.where(dist <= WINDOW,
                                                  -(ALIBI_SLOPES[h] * LOG2_E) * distf, NEG_INF)
        bias_ref[3 * N_Q_HEADS] = jnp.full((BLOCK, BLOCK), NEG_INF, jnp.float32)
        er = lax.broadcasted_iota(jnp.int32, (KV_W, KV_W), 0) // HEAD_DIM
        ec = lax.broadcasted_iota(jnp.int32, (KV_W, KV_W), 1) // HEAD_DIM
        e_ref[...] = jnp.where(er == ec, 1.0, 0.0).astype(jnp.bfloat16)

    shift1 = mod_ref[0:1, :]
    scale1 = mod_ref[1:2, :]
    gate1 = mod_ref[2:3, :]
    gain = g1n_ref[...]

    h_tile = _rms_mod(x_ref[...], gain, scale1, shift1)
    h_ref[BLOCK:BLOCK + t, :] = h_tile.astype(jnp.bfloat16)
    ht = h_ref[BLOCK:BLOCK + t, :]
    q = jnp.dot(ht, win_ref[:, OFF_Q:OFF_Q + D_MODEL], preferred_element_type=jnp.float32)

    h_prev = _rms_mod(xp_ref[...], gain, scale1, shift1)
    h_next = _rms_mod(xn_ref[...], gain, scale1, shift1)
    h_prev = jnp.where(i > 0, h_prev, 0.0)
    h_next = jnp.where(i < n_tiles - 1, h_next, 0.0)
    h_ref[0:BLOCK, :] = h_prev.astype(jnp.bfloat16)
    h_ref[BLOCK + t:, :] = h_next.astype(jnp.bfloat16)
    hc_ref[...] = jnp.concatenate(
        [h_prev[BLOCK - SUBLANES:], h_tile, h_next[:SUBLANES]], axis=0).astype(jnp.bfloat16)

    k = jnp.dot(h_ref[...], win_ref[:, OFF_K:OFF_K + KV_W], preferred_element_type=jnp.float32)
    vt = lax.dot_general(wvt_ref[...], h_ref[...], (((1,), (1,)), ((), ())),
                         preferred_element_type=jnp.float32)

    cb = jnp.dot(ht, win_ref[:, OFF_CB:OFF_CB + D_CONV], preferred_element_type=jnp.float32)
    ga = jnp.dot(ht, win_ref[:, OFF_GA:OFF_GA + D_MODEL], preferred_element_type=jnp.float32)
    ya_ref[...] = jax.nn.sigmoid(ga) * cb

    kn = k * lax.rsqrt(_group_mean_sq(k, e_ref) + EPS) * kg_ref[...]
    lane = lax.broadcasted_iota(jnp.int32, (1, 2 * HEAD_DIM), 1)
    lower = lane < HEAD_DIM
    for pair in range(N_KV_HEADS // 2):
        tile = kn[:, pair * 2 * HEAD_DIM:(pair + 1) * 2 * HEAD_DIM]
        swapped = pltpu.roll(tile, HEAD_DIM, 1)
        k2_ref[2 * pair] = jnp.where(lower, tile, swapped).astype(jnp.bfloat16)
        k2_ref[2 * pair + 1] = jnp.where(lower, swapped, tile).astype(jnp.bfloat16)

    ones_rows = jnp.where(
        lax.broadcasted_iota(jnp.int32, (VT_ROWS - HEAD_DIM, t + 2 * BLOCK), 0) == 0, 1.0, 0.0)
    for g in range(N_KV_HEADS):
        vt_ref[g, 0:HEAD_DIM, :] = vt[g * HEAD_DIM:(g + 1) * HEAD_DIM, :].astype(jnp.bfloat16)
        vt_ref[g, HEAD_DIM:, :] = ones_rows.astype(jnp.bfloat16)

    qscale = qg_ref[...] * (LOG2_E / math.sqrt(HEAD_DIM))
    for c in range(D_MODEL // KV_W):
        qc = q[:, c * KV_W:(c + 1) * KV_W]
        qn_ref[:, c * KV_W:(c + 1) * KV_W] = (
            qc * lax.rsqrt(_group_mean_sq(qc, e_ref) + EPS) * qscale[:, c * KV_W:(c + 1) * KV_W]
        ).astype(jnp.bfloat16)

    def conv_block(c):
        cs = slice(c * SIDE_COLS, (c + 1) * SIDE_COLS)
        hc = hc_ref[...]
        cc = jnp.dot(hc, win_ref[:, OFF_CC + c * SIDE_COLS:OFF_CC + (c + 1) * SIDE_COLS],
                     preferred_element_type=jnp.float32)
        cu = jnp.dot(hc, win_ref[:, OFF_CU + c * SIDE_COLS:OFF_CU + (c + 1) * SIDE_COLS],
                     preferred_element_type=jnp.float32)
        ya_ref[:, cs] = ya_ref[:, cs] * _dwconv3(cc * cu, convw_ref[:, cs])

    def gate_b_block(c):
        cs = slice(c * SIDE_COLS, (c + 1) * SIDE_COLS)
        gb = jnp.dot(ht, win_ref[:, OFF_GB + c * SIDE_COLS:OFF_GB + (c + 1) * SIDE_COLS],
                     preferred_element_type=jnp.float32)
        sgb_ref[:, cs] = jax.nn.sigmoid(gb)

    side_jobs = []
    for c in range(D_MODEL // SIDE_COLS):
        side_jobs.append(lambda c=c: conv_block(c))
        side_jobs.append(lambda c=c: gate_b_block(c))

    qlower = lax.broadcasted_iota(jnp.int32, (BLOCK, 2 * HEAD_DIM), 1) < HEAD_DIM
    zero_q = jnp.zeros((BLOCK, 2 * HEAD_DIM), jnp.bfloat16)
    mask_blk = 3 * N_Q_HEADS

    def scores_t(qb, g):
        r0 = qb * BLOCK
        kb = k2_ref[g, r0:r0 + 3 * BLOCK, :]
        qs = []
        for half in range(GROUP // 2):
            c0 = (g * GROUP + 2 * half) * HEAD_DIM
            pair = qn_ref[r0:r0 + BLOCK, c0:c0 + 2 * HEAD_DIM]
            qs.append(jnp.where(qlower, pair, zero_q))
            qs.append(jnp.where(qlower, zero_q, pair))
        q4 = jnp.concatenate(qs, axis=0)
        return lax.dot_general(kb, q4, (((1,), (1,)), ((), ())),
                               preferred_element_type=jnp.float32)

    def attend(qb, g, st):
        r0 = qb * BLOCK
        ps = []
        sink_terms = []
        for j in range(GROUP):
            hh = g * GROUP + j
            idx = [3 * hh, 3 * hh + 1, 3 * hh + 2]
            if qb == 0:
                idx[0] = jnp.where(i == 0, mask_blk, idx[0])
            if qb == t // BLOCK - 1:
                idx[2] = jnp.where(i == n_tiles - 1, mask_blk, idx[2])
            s = jnp.concatenate(
                [st[blk * BLOCK:(blk + 1) * BLOCK, j * BLOCK:(j + 1) * BLOCK] + bias_ref[idx[blk]]
                 for blk in range(3)], axis=0)
            sink = sink_ref[hh] * LOG2_E
            m = jnp.maximum(jnp.max(s, axis=0, keepdims=True), sink)
            ps.append(jnp.exp2(s - m).astype(jnp.bfloat16))
            sink_terms.append(jnp.exp2(sink - m))
        p4 = jnp.concatenate(ps, axis=1)
        ot = jnp.dot(vt_ref[g, :, r0:r0 + 3 * BLOCK], p4,
                     preferred_element_type=jnp.float32)
        den = ot[HEAD_DIM:HEAD_DIM + 1, :] + jnp.concatenate(sink_terms, axis=1)
        y = ot[0:HEAD_DIM, :] / den
        for j in range(GROUP):
            hh = g * GROUP + j
            ybt_ref[hh * HEAD_DIM:(hh + 1) * HEAD_DIM, r0:r0 + BLOCK] = y[:, j * BLOCK:(j + 1) * BLOCK]

    stages = [(qb, g) for qb in range(t // BLOCK) for g in range(N_KV_HEADS)]
    st_next = scores_t(*stages[0])
    for n, (qb, g) in enumerate(stages):
        st = st_next
        if n + 1 < len(stages):
            st_next = scores_t(*stages[n + 1])
        if n % 2 == 0 and side_jobs:
            side_jobs.pop(0)()
        attend(qb, g, st)
    for job in side_jobs:
        job()

    yb = ybt_ref[...].T
    mrg_ref[...] = (ya_ref[...] + sgb_ref[...] * yb).astype(jnp.bfloat16)
    out = jnp.dot(mrg_ref[...], wo_ref[...], preferred_element_type=jnp.float32)
    o_ref[...] = x_ref[...] + gate1 * out


def _token_mixer(x, mod, norm_gain, w_in, conv_w, q_gain, k_gain, sink, w_o):
    b, s, d = x.shape
    t = SEQ_TILE
    assert s % t == 0 and t % BLOCK == 0
    n_tiles = s // t
    bpt = t // BLOCK
    n_blocks = s // BLOCK

    const = lambda shape: pl.BlockSpec(shape, lambda bi, i, *_: (0,) * len(shape),
                                       pipeline_mode=pl.Buffered(1))
    grid_spec = pltpu.PrefetchScalarGridSpec(
        num_scalar_prefetch=1,
        grid=(b, n_tiles),
        in_specs=[
            pl.BlockSpec((None, t, d), lambda bi, i, *_: (bi, i, 0)),
            pl.BlockSpec((None, BLOCK, d), lambda bi, i, *_: (bi, jnp.maximum(i * bpt - 1, 0), 0)),
            pl.BlockSpec((None, BLOCK, d),
                         lambda bi, i, *_: (bi, jnp.minimum((i + 1) * bpt, n_blocks - 1), 0)),
            pl.BlockSpec((None, 6, d), lambda bi, i, *_: (bi, 0, 0)),
            const((1, d)),
            pl.BlockSpec(memory_space=pl.ANY),
            const((3, D_CONV)),
            const((1, d)),
            const((1, KV_W)),
            pl.BlockSpec(memory_space=pl.ANY),
        ],
        out_specs=pl.BlockSpec((None, t, d), lambda bi, i, *_: (bi, i, 0)),
        scratch_shapes=[
            pltpu.VMEM((d, D_IN), jnp.bfloat16),
            pltpu.VMEM((d, d), jnp.bfloat16),
            pltpu.VMEM((KV_W, d), jnp.bfloat16),
            pltpu.VMEM((t + 2 * BLOCK, d), jnp.bfloat16),
            pltpu.VMEM((t + 2 * SUBLANES, d), jnp.bfloat16),
            pltpu.VMEM((N_KV_HEADS, t + 2 * BLOCK, 2 * HEAD_DIM), jnp.bfloat16),
            pltpu.VMEM((N_KV_HEADS, VT_ROWS, t + 2 * BLOCK), jnp.bfloat16),
            pltpu.VMEM((t, d), jnp.bfloat16),
            pltpu.VMEM((t, d), jnp.float32),
            pltpu.VMEM((t, d), jnp.float32),
            pltpu.VMEM((d, t), jnp.float32),
            pltpu.VMEM((t, d), jnp.bfloat16),
            pltpu.VMEM((3 * N_Q_HEADS + 1, BLOCK, BLOCK), jnp.float32),
            pltpu.VMEM((KV_W, KV_W), jnp.bfloat16),
        ],
    )
    return pl.pallas_call(
        _mixer_kernel,
        grid_spec=grid_spec,
        out_shape=jax.ShapeDtypeStruct((b, s, d), jnp.float32),
        compiler_params=pltpu.CompilerParams(
            dimension_semantics=("arbitrary", "arbitrary"),
            vmem_limit_bytes=VMEM_LIMIT_BYTES),
        name="token_mixer",
    )(sink, x, x, x, mod, norm_gain.reshape(1, d), w_in, conv_w,
      jnp.tile(q_gain, N_Q_HEADS).reshape(1, d), jnp.tile(k_gain, N_KV_HEADS).reshape(1, KV_W), w_o)


def _ffn_kernel(x_ref, xp_ref, xn_ref,
                mod_ref, g2n_ref, wup_hbm, fcw_ref, fcb_ref, wdn_hbm,
                o_ref,
                wup_ref,
                wdn_ref,
                ht_ref,
                he_ref,
                act_ref):
    t = x_ref.shape[0]
    i = pl.program_id(1)
    n_tiles = pl.num_programs(1)

    @pl.when(jnp.logical_and(pl.program_id(0) == 0, i == 0))
    def _():
        _stage_weights(wup_hbm, wup_ref)
        _stage_weights(wdn_hbm, wdn_ref)

    shift2 = mod_ref[3:4, :]
    scale2 = mod_ref[4:5, :]
    gate2 = mod_ref[5:6, :]
    gain = g2n_ref[...]

    h_prev = _rms_mod(xp_ref[...], gain, scale2, shift2)
    h_next = _rms_mod(xn_ref[...], gain, scale2, shift2)
    h_prev = jnp.where(i > 0, h_prev, 0.0)
    h_next = jnp.where(i < n_tiles - 1, h_next, 0.0)
    h_tile = _rms_mod(x_ref[...], gain, scale2, shift2)
    ht_ref[...] = h_tile.astype(jnp.bfloat16)
    he_ref[...] = jnp.concatenate([h_prev, h_tile, h_next], axis=0).astype(jnp.bfloat16)

    for c in range(D_FF // FF_CHUNK):
        c0 = c * FF_CHUNK
        a_ext = jnp.dot(he_ref[...], wup_ref[:, c0:c0 + FF_CHUNK],
                        preferred_element_type=jnp.float32)
        gbr = jnp.dot(ht_ref[...], wup_ref[:, D_FF + c0:D_FF + c0 + FF_CHUNK],
                      preferred_element_type=jnp.float32)
        a = _dwconv3(a_ext, fcw_ref[:, c0:c0 + FF_CHUNK]) + fcb_ref[:, c0:c0 + FF_CHUNK]
        gelu = 0.5 * a * (1.0 + jnp.tanh(math.sqrt(2.0 / math.pi) * (a + 0.044715 * (a * a * a))))
        act_ref[:, c0:c0 + FF_CHUNK] = (gelu * gbr).astype(jnp.bfloat16)

    y = jnp.dot(act_ref[...], wdn_ref[...], preferred_element_type=jnp.float32)
    o_ref[...] = x_ref[...] + gate2 * y


def _conv_ffn(x, mod, norm_gain, w_up, ffn_conv_w, ffn_conv_b, w_down):
    b, s, d = x.shape
    t = FFN_TILE
    assert s % t == 0 and D_FF % FF_CHUNK == 0
    n_tiles = s // t
    rpt = t // SUBLANES
    n_groups = s // SUBLANES

    const = lambda shape: pl.BlockSpec(shape, lambda bi, i: (0,) * len(shape),
                                       pipeline_mode=pl.Buffered(1))
    return pl.pallas_call(
        _ffn_kernel,
        grid=(b, n_tiles),
        in_specs=[
            pl.BlockSpec((None, t, d), lambda bi, i: (bi, i, 0)),
            pl.BlockSpec((None, SUBLANES, d), lambda bi, i: (bi, jnp.maximum(i * rpt - 1, 0), 0)),
            pl.BlockSpec((None, SUBLANES, d),
                         lambda bi, i: (bi, jnp.minimum((i + 1) * rpt, n_groups - 1), 0)),
            pl.BlockSpec((None, 6, d), lambda bi, i: (bi, 0, 0)),
            const((1, d)),
            pl.BlockSpec(memory_space=pl.ANY),
            const((3, D_FF)),
            const((1, D_FF)),
            pl.BlockSpec(memory_space=pl.ANY),
        ],
        out_specs=pl.BlockSpec((None, t, d), lambda bi, i: (bi, i, 0)),
        out_shape=jax.ShapeDtypeStruct((b, s, d), jnp.float32),
        scratch_shapes=[
            pltpu.VMEM((d, 2 * D_FF), jnp.bfloat16),
            pltpu.VMEM((D_FF, d), jnp.bfloat16),
            pltpu.VMEM((t, d), jnp.bfloat16),
            pltpu.VMEM((t + 2 * SUBLANES, d), jnp.bfloat16),
            pltpu.VMEM((t, D_FF), jnp.bfloat16),
        ],
        compiler_params=pltpu.CompilerParams(
            dimension_semantics=("arbitrary", "arbitrary"),
            vmem_limit_bytes=VMEM_LIMIT_BYTES),
        name="conv_ffn",
    )(x, x, x, mod, norm_gain.reshape(1, d), w_up, ffn_conv_w, ffn_conv_b.reshape(1, D_FF), w_down)


def kernel(x, c, w_ada, b_ada, norm1_gain, w_in, conv_w, q_gain, k_gain, sink, w_o, norm2_gain,
           w_up, ffn_conv_w, ffn_conv_b, w_down):
    depth = w_ada.shape[0]
    b, s, d = x.shape
    for l in range(depth):
        mod = _modulation(c, w_ada[l], b_ada[l]).reshape(b, 6, d)
        x = _token_mixer(x, mod, norm1_gain[l], w_in[l], conv_w[l], q_gain[l], k_gain[l],
                         sink[l], w_o[l])
        x = _conv_ffn(x, mod, norm2_gain[l], w_up[l], ffn_conv_w[l], ffn_conv_b[l], w_down[l])
    return x
```

```python
import math

import jax
import jax.numpy as jnp
from jax import lax
from jax.experimental import pallas as pl
from jax.experimental.pallas import tpu as pltpu

D_MODEL = 1024
HEAD_DIM = 64
N_Q_HEADS = D_MODEL // HEAD_DIM
N_KV_HEADS = N_Q_HEADS // 4
GROUP = N_Q_HEADS // N_KV_HEADS
D_CONV = D_MODEL
WINDOW = 128
BLOCK = 128
D_FF = ((8 * D_MODEL // 3 + 127) // 128) * 128
EPS = 1e-6
NEG_INF = -1e30
LOG2_E = math.log2(math.e)

OFF_CB = 0
OFF_CC = OFF_CB + D_CONV
OFF_CU = OFF_CC + D_CONV
OFF_Q = OFF_CU + D_CONV
OFF_K = OFF_Q + N_Q_HEADS * HEAD_DIM
OFF_V = OFF_K + N_KV_HEADS * HEAD_DIM
OFF_GA = OFF_V + N_KV_HEADS * HEAD_DIM
OFF_GB = OFF_GA + D_MODEL
D_IN = OFF_GB + D_MODEL

KV_W = N_KV_HEADS * HEAD_DIM
SUBLANES = 8
BF16_ROWS = 16
VT_ROWS = HEAD_DIM + BF16_ROWS
SEQ_TILE = 512
FF_CHUNK = 256
SIDE_COLS = 256
WEIGHT_CHUNK_BYTES = 2 * 1024 * 1024
WEIGHT_DMA_DEPTH = 4
VMEM_LIMIT_BYTES = 56 * 1024 * 1024

ALIBI_SLOPES = [2.0 ** (-8.0 * (h + 1) / N_Q_HEADS) for h in range(N_Q_HEADS)]


def _rms_mod(xv, gain, scale, shift):
    ms = jnp.mean(xv * xv, axis=-1, keepdims=True)
    return xv * lax.rsqrt(ms + EPS) * (gain * (1.0 + scale)) + shift


def _dwconv3(v_ext, w):
    rows = v_ext.shape[0]
    t = rows - 2 * SUBLANES
    prev = pltpu.roll(v_ext, 1, 0)[SUBLANES:SUBLANES + t]
    nxt = pltpu.roll(v_ext, rows - 1, 0)[SUBLANES:SUBLANES + t]
    return w[0:1, :] * prev + w[1:2, :] * v_ext[SUBLANES:SUBLANES + t] + w[2:3, :] * nxt


def _stage_weights(src_hbm, dst_ref):
    rows, cols = src_hbm.shape
    chunk_rows = max(r for r in range(BF16_ROWS, rows + 1, BF16_ROWS)
                     if rows % r == 0 and r * cols * 4 <= WEIGHT_CHUNK_BYTES
                     and rows // r >= WEIGHT_DMA_DEPTH)
    n_chunks = rows // chunk_rows

    def body(stage_ref, sem_ref):
        def copy(c):
            slot = lax.rem(c, WEIGHT_DMA_DEPTH)
            r0 = pl.multiple_of(c * chunk_rows, chunk_rows)
            return pltpu.make_async_copy(src_hbm.at[pl.ds(r0, chunk_rows), :],
                                         stage_ref.at[slot], sem_ref.at[slot])

        for c in range(WEIGHT_DMA_DEPTH - 1):
            copy(c).start()

        def step(c, carry):
            @pl.when(c + WEIGHT_DMA_DEPTH - 1 < n_chunks)
            def _():
                copy(c + WEIGHT_DMA_DEPTH - 1).start()

            copy(c).wait()
            r0 = pl.multiple_of(c * chunk_rows, chunk_rows)
            dst_ref[pl.ds(r0, chunk_rows), :] = (
                stage_ref[lax.rem(c, WEIGHT_DMA_DEPTH)].astype(jnp.bfloat16))
            return carry

        lax.fori_loop(0, n_chunks, step, 0)

    pl.run_scoped(body, pltpu.VMEM((WEIGHT_DMA_DEPTH, chunk_rows, cols), jnp.float32),
                  pltpu.SemaphoreType.DMA((WEIGHT_DMA_DEPTH,)))


def _ada_kernel(c_ref, w_ref, b_ref, o_ref):
    c = c_ref[...]
    act = c * jax.nn.sigmoid(c)
    o_ref[...] = jnp.dot(act.astype(jnp.bfloat16), w_ref[...].astype(jnp.bfloat16),
                         preferred_element_type=jnp.float32) + b_ref[...]


def _modulation(c, w_ada, b_ada):
    b, d = c.shape
    n = w_ada.shape[1]
    rows = SUBLANES
    c_pad = jnp.zeros((rows, d), c.dtype).at[:b].set(c)
    tn = d
    out = pl.pallas_call(
        _ada_kernel,
        grid=(n // tn,),
        in_specs=[
            pl.BlockSpec((rows, d), lambda j: (0, 0)),
            pl.BlockSpec((d, tn), lambda j: (0, j)),
            pl.BlockSpec((1, tn), lambda j: (0, j)),
        ],
        out_specs=pl.BlockSpec((rows, tn), lambda j: (0, j)),
        out_shape=jax.ShapeDtypeStruct((rows, n), jnp.float32),
        compiler_params=pltpu.CompilerParams(dimension_semantics=("arbitrary",)),
        name="ada_modulation",
    )(c_pad, w_ada, b_ada.reshape(1, n))
    return out[:b]


def _group_mean_sq(v, e_ref):
    sq = (v * v).astype(jnp.bfloat16)
    return jnp.dot(sq, e_ref[...], preferred_element_type=jnp.float32) * (1.0 / HEAD_DIM)


def _mixer_kernel(sink_ref,
                  x_ref, xp_ref, xn_ref,
                  mod_ref, modn_ref,
                  g1n_ref, win_hbm, convw_ref, qg_ref, kg_ref, wo_hbm,
                  o_ref,
                  win_ref,
                  wo_ref,
                  wvt_ref,
                  h_ref,
                  hc_ref,
                  k2_ref,
                  vt_ref,
                  kc_ref,
                  vc_ref,
                  qn_ref,
                  ya_ref,
                  sgb_ref,
                  ybt_ref,
                  mrg_ref,
                  bias_ref,
                  e_ref):
    t = x_ref.shape[0]
    i = pl.program_id(1)
    n_tiles = pl.num_programs(1)
    first = jnp.logical_and(pl.program_id(0) == 0, i == 0)
    shift1 = mod_ref[0:1, :]
    scale1 = mod_ref[1:2, :]
    gate1 = mod_ref[2:3, :]
    gain = g1n_ref[...]
    lower = lax.broadcasted_iota(jnp.int32, (1, 2 * HEAD_DIM), 1) < HEAD_DIM

    def project_kv(h_rows):
        k = jnp.dot(h_rows, win_ref[:, OFF_K:OFF_K + KV_W], preferred_element_type=jnp.float32)
        vt = lax.dot_general(wvt_ref[...], h_rows, (((1,), (1,)), ((), ())),
                             preferred_element_type=jnp.float32)
        return k, vt

    def store_kv(k, vt, k_dst, v_dst, r0):
        rows = k.shape[0]
        kn = k * lax.rsqrt(_group_mean_sq(k, e_ref) + EPS) * kg_ref[...]
        for pair in range(N_KV_HEADS // 2):
            tile = kn[:, pair * 2 * HEAD_DIM:(pair + 1) * 2 * HEAD_DIM]
            swapped = pltpu.roll(tile, HEAD_DIM, 1)
            k_dst[2 * pair, r0:r0 + rows, :] = jnp.where(lower, tile, swapped).astype(jnp.bfloat16)
            k_dst[2 * pair + 1, r0:r0 + rows, :] = jnp.where(lower, swapped, tile).astype(jnp.bfloat16)
        for g in range(N_KV_HEADS):
            v_dst[g, 0:HEAD_DIM, r0:r0 + rows] = (
                vt[g * HEAD_DIM:(g + 1) * HEAD_DIM, :].astype(jnp.bfloat16))

    @pl.when(first)
    def _():
        _stage_weights(win_hbm, win_ref)
        _stage_weights(wo_hbm, wo_ref)
        wvt_ref[...] = win_ref[:, OFF_V:OFF_V + KV_W].astype(jnp.float32).T.astype(jnp.bfloat16)
        kj = lax.broadcasted_iota(jnp.int32, (BLOCK, BLOCK), 0)
        qi = lax.broadcasted_iota(jnp.int32, (BLOCK, BLOCK), 1)
        for blk in range(3):
            dist = jnp.abs(qi + BLOCK - (kj + blk * BLOCK))
            distf = dist.astype(jnp.float32)
            for h in range(N_Q_HEADS):
                bias_ref[3 * h + blk] = jnp.where(dist <= WINDOW,
                                                  -(ALIBI_SLOPES[h] * LOG2_E) * distf, NEG_INF)
        bias_ref[3 * N_Q_HEADS] = jnp.full((BLOCK, BLOCK), NEG_INF, jnp.float32)
        er = lax.broadcasted_iota(jnp.int32, (KV_W, KV_W), 0) // HEAD_DIM
        ec = lax.broadcasted_iota(jnp.int32, (KV_W, KV_W), 1) // HEAD_DIM
        e_ref[...] = jnp.where(er == ec, 1.0, 0.0).astype(jnp.bfloat16)
        def ones_rows(lanes):
            rows = lax.broadcasted_iota(jnp.int32, (VT_ROWS - HEAD_DIM, lanes), 0)
            return jnp.where(rows == 0, 1.0, 0.0).astype(jnp.bfloat16)

        for g in range(N_KV_HEADS):
            vt_ref[g, HEAD_DIM:, :] = ones_rows(t + 2 * BLOCK)
            vc_ref[g, HEAD_DIM:, :] = ones_rows(2 * BLOCK)
        h0 = _rms_mod(x_ref[0:BLOCK, :], gain, scale1, shift1).astype(jnp.bfloat16)
        store_kv(*project_kv(h0), kc_ref, vc_ref, BLOCK)
        zero_k = jnp.zeros((BLOCK, 2 * HEAD_DIM), jnp.bfloat16)
        zero_v = jnp.zeros((HEAD_DIM, BLOCK), jnp.bfloat16)
        for g in range(N_KV_HEADS):
            kc_ref[g, 0:BLOCK, :] = zero_k
            vc_ref[g, 0:HEAD_DIM, 0:BLOCK] = zero_v

    h_tile = _rms_mod(x_ref[...], gain, scale1, shift1)
    h_ref[0:t, :] = h_tile.astype(jnp.bfloat16)
    ht = h_ref[0:t, :]
    q = jnp.dot(ht, win_ref[:, OFF_Q:OFF_Q + D_MODEL], preferred_element_type=jnp.float32)

    h_after = _rms_mod(xn_ref[...], g1n_ref[...], modn_ref[1:2, :], modn_ref[0:1, :])
    h_ref[t:, :] = h_after.astype(jnp.bfloat16)
    h_before = jnp.where(i > 0, _rms_mod(xp_ref[...], gain, scale1, shift1), 0.0)
    hc_ref[...] = jnp.concatenate(
        [h_before, h_tile, jnp.where(i < n_tiles - 1, h_after[:SUBLANES], 0.0)],
        axis=0).astype(jnp.bfloat16)

    for g in range(N_KV_HEADS):
        k2_ref[g, 0:2 * BLOCK, :] = kc_ref[g]
        vt_ref[g, :, 0:2 * BLOCK] = vc_ref[g]
    k, vt = project_kv(h_ref[BLOCK:, :])

    cb = jnp.dot(ht, win_ref[:, OFF_CB:OFF_CB + D_CONV], preferred_element_type=jnp.float32)
    ga = jnp.dot(ht, win_ref[:, OFF_GA:OFF_GA + D_MODEL], preferred_element_type=jnp.float32)
    ya_ref[...] = jax.nn.sigmoid(ga) * cb

    store_kv(k, vt, k2_ref, vt_ref, 2 * BLOCK)

    qscale = qg_ref[...] * (LOG2_E / math.sqrt(HEAD_DIM))
    for c in range(D_MODEL // KV_W):
        qc = q[:, c * KV_W:(c + 1) * KV_W]
        qn_ref[:, c * KV_W:(c + 1) * KV_W] = (
            qc * lax.rsqrt(_group_mean_sq(qc, e_ref) + EPS) * qscale[:, c * KV_W:(c + 1) * KV_W]
        ).astype(jnp.bfloat16)

    def conv_block(c):
        cs = slice(c * SIDE_COLS, (c + 1) * SIDE_COLS)
        hc = hc_ref[...]
        cc = jnp.dot(hc, win_ref[:, OFF_CC + c * SIDE_COLS:OFF_CC + (c + 1) * SIDE_COLS],
                     preferred_element_type=jnp.float32)
        cu = jnp.dot(hc, win_ref[:, OFF_CU + c * SIDE_COLS:OFF_CU + (c + 1) * SIDE_COLS],
                     preferred_element_type=jnp.float32)
        ya_ref[:, cs] = ya_ref[:, cs] * _dwconv3(cc * cu, convw_ref[:, cs])

    def gate_b_block(c):
        cs = slice(c * SIDE_COLS, (c + 1) * SIDE_COLS)
        gb = jnp.dot(ht, win_ref[:, OFF_GB + c * SIDE_COLS:OFF_GB + (c + 1) * SIDE_COLS],
                     preferred_element_type=jnp.float32)
        sgb_ref[:, cs] = jax.nn.sigmoid(gb)

    side_jobs = []
    for c in range(D_MODEL // SIDE_COLS):
        side_jobs.append(lambda c=c: conv_block(c))
        side_jobs.append(lambda c=c: gate_b_block(c))

    qlower = lax.broadcasted_iota(jnp.int32, (BLOCK, 2 * HEAD_DIM), 1) < HEAD_DIM
    zero_q = jnp.zeros((BLOCK, 2 * HEAD_DIM), jnp.bfloat16)
    mask_blk = 3 * N_Q_HEADS

    def scores_t(qb, g):
        r0 = qb * BLOCK
        kb = k2_ref[g, r0:r0 + 3 * BLOCK, :]
        qs = []
        for half in range(GROUP // 2):
            c0 = (g * GROUP + 2 * half) * HEAD_DIM
            pair = qn_ref[r0:r0 + BLOCK, c0:c0 + 2 * HEAD_DIM]
            qs.append(jnp.where(qlower, pair, zero_q))
            qs.append(jnp.where(qlower, zero_q, pair))
        q4 = jnp.concatenate(qs, axis=0)
        return lax.dot_general(kb, q4, (((1,), (1,)), ((), ())),
                               preferred_element_type=jnp.float32)

    def attend(qb, g, st):
        r0 = qb * BLOCK
        ps = []
        sink_terms = []
        for j in range(GROUP):
            hh = g * GROUP + j
            idx = [3 * hh, 3 * hh + 1, 3 * hh + 2]
            if qb == 0:
                idx[0] = jnp.where(i == 0, mask_blk, idx[0])
            if qb == t // BLOCK - 1:
                idx[2] = jnp.where(i == n_tiles - 1, mask_blk, idx[2])
            s = jnp.concatenate(
                [st[blk * BLOCK:(blk + 1) * BLOCK, j * BLOCK:(j + 1) * BLOCK] + bias_ref[idx[blk]]
                 for blk in range(3)], axis=0)
            sink = sink_ref[hh] * LOG2_E
            m = jnp.maximum(jnp.max(s, axis=0, keepdims=True), sink)
            ps.append(jnp.exp2(s - m).astype(jnp.bfloat16))
            sink_terms.append(jnp.exp2(sink - m))
        p4 = jnp.concatenate(ps, axis=1)
        ot = jnp.dot(vt_ref[g, :, r0:r0 + 3 * BLOCK], p4,
                     preferred_element_type=jnp.float32)
        den = ot[HEAD_DIM:HEAD_DIM + 1, :] + jnp.concatenate(sink_terms, axis=1)
        y = ot[0:HEAD_DIM, :] / den
        for j in range(GROUP):
            hh = g * GROUP + j
            ybt_ref[hh * HEAD_DIM:(hh + 1) * HEAD_DIM, r0:r0 + BLOCK] = y[:, j * BLOCK:(j + 1) * BLOCK]

    stages = [(qb, g) for qb in range(t // BLOCK) for g in range(N_KV_HEADS)]
    st_next = scores_t(*stages[0])
    for n, (qb, g) in enumerate(stages):
        st = st_next
        if n + 1 < len(stages):
            st_next = scores_t(*stages[n + 1])
        if n % 2 == 0 and side_jobs:
            side_jobs.pop(0)()
        attend(qb, g, st)
    for job in side_jobs:
        job()

    for g in range(N_KV_HEADS):
        kc_ref[g] = k2_ref[g, t:t + 2 * BLOCK, :]
        vc_ref[g] = vt_ref[g, :, t:t + 2 * BLOCK]

    yb = ybt_ref[...].T
    mrg_ref[...] = (ya_ref[...] + sgb_ref[...] * yb).astype(jnp.bfloat16)
    out = jnp.dot(mrg_ref[...], wo_ref[...], preferred_element_type=jnp.float32)
    o_ref[...] = x_ref[...] + gate1 * out


def _token_mixer(x, mod, norm_gain, w_in, conv_w, q_gain, k_gain, sink, w_o):
    b, s, d = x.shape
    t = SEQ_TILE
    assert s % t == 0 and t % BLOCK == 0
    n_tiles = s // t
    bpt = t // BLOCK
    rpt = t // SUBLANES
    n_blocks = s // BLOCK

    def after(bi, i):
        blk = jnp.minimum(bi * n_blocks + (i + 1) * bpt, b * n_blocks - 1)
        return blk // n_blocks, blk % n_blocks

    const = lambda shape: pl.BlockSpec(shape, lambda bi, i, *_: (0,) * len(shape),
                                       pipeline_mode=pl.Buffered(1))
    grid_spec = pltpu.PrefetchScalarGridSpec(
        num_scalar_prefetch=1,
        grid=(b, n_tiles),
        in_specs=[
            pl.BlockSpec((None, t, d), lambda bi, i, *_: (bi, i, 0)),
            pl.BlockSpec((None, SUBLANES, d), lambda bi, i, *_: (bi, jnp.maximum(i * rpt - 1, 0), 0)),
            pl.BlockSpec((None, BLOCK, d), lambda bi, i, *_: (*after(bi, i), 0)),
            pl.BlockSpec((None, 6, d), lambda bi, i, *_: (bi, 0, 0)),
            pl.BlockSpec((None, 6, d), lambda bi, i, *_: (after(bi, i)[0], 0, 0)),
            const((1, d)),
            pl.BlockSpec(memory_space=pl.ANY),
            const((3, D_CONV)),
            const((1, d)),
            const((1, KV_W)),
            pl.BlockSpec(memory_space=pl.ANY),
        ],
        out_specs=pl.BlockSpec((None, t, d), lambda bi, i, *_: (bi, i, 0)),
        scratch_shapes=[
            pltpu.VMEM((d, D_IN), jnp.bfloat16),
            pltpu.VMEM((d, d), jnp.bfloat16),
            pltpu.VMEM((KV_W, d), jnp.bfloat16),
            pltpu.VMEM((t + BLOCK, d), jnp.bfloat16),
            pltpu.VMEM((t + 2 * SUBLANES, d), jnp.bfloat16),
            pltpu.VMEM((N_KV_HEADS, t + 2 * BLOCK, 2 * HEAD_DIM), jnp.bfloat16),
            pltpu.VMEM((N_KV_HEADS, VT_ROWS, t + 2 * BLOCK), jnp.bfloat16),
            pltpu.VMEM((N_KV_HEADS, 2 * BLOCK, 2 * HEAD_DIM), jnp.bfloat16),
            pltpu.VMEM((N_KV_HEADS, VT_ROWS, 2 * BLOCK), jnp.bfloat16),
            pltpu.VMEM((t, d), jnp.bfloat16),
            pltpu.VMEM((t, d), jnp.float32),
            pltpu.VMEM((t, d), jnp.float32),
            pltpu.VMEM((d, t), jnp.float32),
            pltpu.VMEM((t, d), jnp.bfloat16),
            pltpu.VMEM((3 * N_Q_HEADS + 1, BLOCK, BLOCK), jnp.float32),
            pltpu.VMEM((KV_W, KV_W), jnp.bfloat16),
        ],
    )
    return pl.pallas_call(
        _mixer_kernel,
        grid_spec=grid_spec,
        out_shape=jax.ShapeDtypeStruct((b, s, d), jnp.float32),
        compiler_params=pltpu.CompilerParams(
            dimension_semantics=("arbitrary", "arbitrary"),
            vmem_limit_bytes=VMEM_LIMIT_BYTES),
        name="token_mixer",
    )(sink, x, x, x, mod, mod, norm_gain.reshape(1, d), w_in, conv_w,
      jnp.tile(q_gain, N_Q_HEADS).reshape(1, d), jnp.tile(k_gain, N_KV_HEADS).reshape(1, KV_W), w_o)


def _ffn_kernel(x_ref, xp_ref, xn_ref,
                mod_ref, g2n_ref, wup_hbm, fcw_ref, fcb_ref, wdn_hbm,
                o_ref,
                wup_ref,
                wdn_ref,
                ht_ref,
                he_ref,
                act_ref):
    t = x_ref.shape[0]
    i = pl.program_id(1)
    n_tiles = pl.num_programs(1)

    @pl.when(jnp.logical_and(pl.program_id(0) == 0, i == 0))
    def _():
        _stage_weights(wup_hbm, wup_ref)
        _stage_weights(wdn_hbm, wdn_ref)

    shift2 = mod_ref[3:4, :]
    scale2 = mod_ref[4:5, :]
    gate2 = mod_ref[5:6, :]
    gain = g2n_ref[...]

    h_prev = _rms_mod(xp_ref[...], gain, scale2, shift2)
    h_next = _rms_mod(xn_ref[...], gain, scale2, shift2)
    h_prev = jnp.where(i > 0, h_prev, 0.0)
    h_next = jnp.where(i < n_tiles - 1, h_next, 0.0)
    h_tile = _rms_mod(x_ref[...], gain, scale2, shift2)
    ht_ref[...] = h_tile.astype(jnp.bfloat16)
    he_ref[...] = jnp.concatenate([h_prev, h_tile, h_next], axis=0).astype(jnp.bfloat16)

    for c in range(D_FF // FF_CHUNK):
        c0 = c * FF_CHUNK
        a_ext = jnp.dot(he_ref[...], wup_ref[:, c0:c0 + FF_CHUNK],
                        preferred_element_type=jnp.float32)
        gbr = jnp.dot(ht_ref[...], wup_ref[:, D_FF + c0:D_FF + c0 + FF_CHUNK],
                      preferred_element_type=jnp.float32)
        a = _dwconv3(a_ext, fcw_ref[:, c0:c0 + FF_CHUNK]) + fcb_ref[:, c0:c0 + FF_CHUNK]
        gelu = 0.5 * a * (1.0 + jnp.tanh(math.sqrt(2.0 / math.pi) * (a + 0.044715 * (a * a * a))))
        act_ref[:, c0:c0 + FF_CHUNK] = (gelu * gbr).astype(jnp.bfloat16)

    y = jnp.dot(act_ref[...], wdn_ref[...], preferred_element_type=jnp.float32)
    o_ref[...] = x_ref[...] + gate2 * y


def _conv_ffn(x, mod, norm_gain, w_up, ffn_conv_w, ffn_conv_b, w_down):
    b, s, d = x.shape
    t = SEQ_TILE
    assert s % t == 0 and D_FF % FF_CHUNK == 0
    n_tiles = s // t
    rpt = t // SUBLANES
    n_groups = s // SUBLANES

    const = lambda shape: pl.BlockSpec(shape, lambda bi, i: (0,) * len(shape),
                                       pipeline_mode=pl.Buffered(1))
    return pl.pallas_call(
        _ffn_kernel,
        grid=(b, n_tiles),
        in_specs=[
            pl.BlockSpec((None, t, d), lambda bi, i: (bi, i, 0)),
            pl.BlockSpec((None, SUBLANES, d), lambda bi, i: (bi, jnp.maximum(i * rpt - 1, 0), 0)),
            pl.BlockSpec((None, SUBLANES, d),
                         lambda bi, i: (bi, jnp.minimum((i + 1) * rpt, n_groups - 1), 0)),
            pl.BlockSpec((None, 6, d), lambda bi, i: (bi, 0, 0)),
            const((1, d)),
            pl.BlockSpec(memory_space=pl.ANY),
            const((3, D_FF)),
            const((1, D_FF)),
            pl.BlockSpec(memory_space=pl.ANY),
        ],
        out_specs=pl.BlockSpec((None, t, d), lambda bi, i: (bi, i, 0)),
        out_shape=jax.ShapeDtypeStruct((b, s, d), jnp.float32),
        scratch_shapes=[
            pltpu.VMEM((d, 2 * D_FF), jnp.bfloat16),
            pltpu.VMEM((D_FF, d), jnp.bfloat16),
            pltpu.VMEM((t, d), jnp.bfloat16),
            pltpu.VMEM((t + 2 * SUBLANES, d), jnp.bfloat16),
            pltpu.VMEM((t, D_FF), jnp.bfloat16),
        ],
        compiler_params=pltpu.CompilerParams(
            dimension_semantics=("arbitrary", "arbitrary"),
            vmem_limit_bytes=VMEM_LIMIT_BYTES),
        name="conv_ffn",
    )(x, x, x, mod, norm_gain.reshape(1, d), w_up, ffn_conv_w, ffn_conv_b.reshape(1, D_FF), w_down)


def kernel(x, c, w_ada, b_ada, norm1_gain, w_in, conv_w, q_gain, k_gain, sink, w_o, norm2_gain,
           w_up, ffn_conv_w, ffn_conv_b, w_down):
    depth = w_ada.shape[0]
    b, s, d = x.shape
    for l in range(depth):
        mod = _modulation(c, w_ada[l], b_ada[l]).reshape(b, 6, d)
        x = _token_mixer(x, mod, norm1_gain[l], w_in[l], conv_w[l], q_gain[l], k_gain[l],
                         sink[l], w_o[l])
        x = _conv_ffn(x, mod, norm2_gain[l], w_up[l], ffn_conv_w[l], ffn_conv_b[l], w_down[l])
    return x
```

```python
import math

import jax
import jax.numpy as jnp
from jax import lax
from jax.experimental import pallas as pl
from jax.experimental.pallas import tpu as pltpu

D_MODEL = 1024
HEAD_DIM = 64
N_Q_HEADS = D_MODEL // HEAD_DIM
N_KV_HEADS = N_Q_HEADS // 4
GROUP = N_Q_HEADS // N_KV_HEADS
D_CONV = D_MODEL
WINDOW = 128
BLOCK = 128
D_FF = ((8 * D_MODEL // 3 + 127) // 128) * 128
EPS = 1e-6
NEG_INF = -1e30
LOG2_E = math.log2(math.e)

OFF_CB = 0
OFF_CC = OFF_CB + D_CONV
OFF_CU = OFF_CC + D_CONV
OFF_Q = OFF_CU + D_CONV
OFF_K = OFF_Q + N_Q_HEADS * HEAD_DIM
OFF_V = OFF_K + N_KV_HEADS * HEAD_DIM
OFF_GA = OFF_V + N_KV_HEADS * HEAD_DIM
OFF_GB = OFF_GA + D_MODEL
D_IN = OFF_GB + D_MODEL

KV_W = N_KV_HEADS * HEAD_DIM
SUBLANES = 8
BF16_ROWS = 16
VT_ROWS = HEAD_DIM + BF16_ROWS
SEQ_TILE = 512
FF_CHUNK = 256
SIDE_COLS = 256
WEIGHT_CHUNK_BYTES = 2 * 1024 * 1024
WEIGHT_DMA_DEPTH = 4
VMEM_LIMIT_BYTES = 56 * 1024 * 1024

ALIBI_SLOPES = [2.0 ** (-8.0 * (h + 1) / N_Q_HEADS) for h in range(N_Q_HEADS)]


def _rms_mod(xv, gain, scale, shift):
    ms = jnp.mean(xv * xv, axis=-1, keepdims=True)
    return xv * lax.rsqrt(ms + EPS) * (gain * (1.0 + scale)) + shift


def _dwconv3(v_ext, w):
    rows = v_ext.shape[0]
    t = rows - 2 * SUBLANES
    prev = pltpu.roll(v_ext, 1, 0)[SUBLANES:SUBLANES + t]
    nxt = pltpu.roll(v_ext, rows - 1, 0)[SUBLANES:SUBLANES + t]
    return w[0:1, :] * prev + w[1:2, :] * v_ext[SUBLANES:SUBLANES + t] + w[2:3, :] * nxt


def _stage_weights(src_hbm, dst_ref):
    rows, cols = src_hbm.shape
    chunk_rows = max(r for r in range(BF16_ROWS, rows + 1, BF16_ROWS)
                     if rows % r == 0 and r * cols * 4 <= WEIGHT_CHUNK_BYTES
                     and rows // r >= WEIGHT_DMA_DEPTH)
    n_chunks = rows // chunk_rows

    def body(stage_ref, sem_ref):
        def copy(c):
            slot = lax.rem(c, WEIGHT_DMA_DEPTH)
            r0 = pl.multiple_of(c * chunk_rows, chunk_rows)
            return pltpu.make_async_copy(src_hbm.at[pl.ds(r0, chunk_rows), :],
                                         stage_ref.at[slot], sem_ref.at[slot])

        for c in range(WEIGHT_DMA_DEPTH - 1):
            copy(c).start()

        def step(c, carry):
            @pl.when(c + WEIGHT_DMA_DEPTH - 1 < n_chunks)
            def _():
                copy(c + WEIGHT_DMA_DEPTH - 1).start()

            copy(c).wait()
            r0 = pl.multiple_of(c * chunk_rows, chunk_rows)
            dst_ref[pl.ds(r0, chunk_rows), :] = (
                stage_ref[lax.rem(c, WEIGHT_DMA_DEPTH)].astype(jnp.bfloat16))
            return carry

        lax.fori_loop(0, n_chunks, step, 0)

    pl.run_scoped(body, pltpu.VMEM((WEIGHT_DMA_DEPTH, chunk_rows, cols), jnp.float32),
                  pltpu.SemaphoreType.DMA((WEIGHT_DMA_DEPTH,)))


def _ada_kernel(c_ref, w_ref, b_ref, o_ref):
    c = c_ref[...]
    act = c * jax.nn.sigmoid(c)
    o_ref[...] = jnp.dot(act.astype(jnp.bfloat16), w_ref[...].astype(jnp.bfloat16),
                         preferred_element_type=jnp.float32) + b_ref[...]


def _modulation(c, w_ada, b_ada):
    b, d = c.shape
    n = w_ada.shape[1]
    rows = SUBLANES
    c_pad = jnp.zeros((rows, d), c.dtype).at[:b].set(c)
    tn = d
    out = pl.pallas_call(
        _ada_kernel,
        grid=(n // tn,),
        in_specs=[
            pl.BlockSpec((rows, d), lambda j: (0, 0)),
            pl.BlockSpec((d, tn), lambda j: (0, j)),
            pl.BlockSpec((1, tn), lambda j: (0, j)),
        ],
        out_specs=pl.BlockSpec((rows, tn), lambda j: (0, j)),
        out_shape=jax.ShapeDtypeStruct((rows, n), jnp.float32),
        compiler_params=pltpu.CompilerParams(dimension_semantics=("arbitrary",)),
        name="ada_modulation",
    )(c_pad, w_ada, b_ada.reshape(1, n))
    return out[:b]


def _group_mean_sq(v, e_ref):
    sq = v * v
    lower = lax.broadcasted_iota(jnp.int32, (1, 2 * HEAD_DIM), 1) < HEAD_DIM
    sums = []
    for c in range(v.shape[1] // (2 * HEAD_DIM)):
        pair = sq[:, c * 2 * HEAD_DIM:(c + 1) * 2 * HEAD_DIM]
        both = jnp.sum(pair, axis=-1, keepdims=True)
        first = jnp.sum(jnp.where(lower, pair, 0.0), axis=-1, keepdims=True)
        sums.append(jnp.where(lower, first, both - first))
    return jnp.concatenate(sums, axis=1) * (1.0 / HEAD_DIM)


def _mixer_kernel(sink_ref,
                  x_ref, xp_ref, xn_ref,
                  mod_ref, modn_ref,
                  g1n_ref, win_hbm, convw_ref, qg_ref, kg_ref, wo_hbm,
                  o_ref,
                  win_ref,
                  wo_ref,
                  wvt_ref,
                  h_ref,
                  hc_ref,
                  k2_ref,
                  vt_ref,
                  kc_ref,
                  vc_ref,
                  qn_ref,
                  ya_ref,
                  sgb_ref,
                  ybt_ref,
                  mrg_ref,
                  bias_ref,
                  e_ref):
    t = x_ref.shape[0]
    i = pl.program_id(1)
    n_tiles = pl.num_programs(1)
    first = jnp.logical_and(pl.program_id(0) == 0, i == 0)
    shift1 = mod_ref[0:1, :]
    scale1 = mod_ref[1:2, :]
    gate1 = mod_ref[2:3, :]
    gain = g1n_ref[...]
    lower = lax.broadcasted_iota(jnp.int32, (1, 2 * HEAD_DIM), 1) < HEAD_DIM

    def project_kv(h_rows):
        k = jnp.dot(h_rows, win_ref[:, OFF_K:OFF_K + KV_W], preferred_element_type=jnp.float32)
        vt = lax.dot_general(wvt_ref[...], h_rows, (((1,), (1,)), ((), ())),
                             preferred_element_type=jnp.float32)
        return k, vt

    def store_kv(k, vt, k_dst, v_dst, r0):
        rows = k.shape[0]
        kn = k * lax.rsqrt(_group_mean_sq(k, e_ref) + EPS) * kg_ref[...]
        for pair in range(N_KV_HEADS // 2):
            tile = kn[:, pair * 2 * HEAD_DIM:(pair + 1) * 2 * HEAD_DIM]
            swapped = pltpu.roll(tile, HEAD_DIM, 1)
            k_dst[2 * pair, r0:r0 + rows, :] = jnp.where(lower, tile, swapped).astype(jnp.bfloat16)
            k_dst[2 * pair + 1, r0:r0 + rows, :] = jnp.where(lower, swapped, tile).astype(jnp.bfloat16)
        for g in range(N_KV_HEADS):
            v_dst[g, 0:HEAD_DIM, r0:r0 + rows] = (
                vt[g * HEAD_DIM:(g + 1) * HEAD_DIM, :].astype(jnp.bfloat16))

    @pl.when(first)
    def _():
        _stage_weights(win_hbm, win_ref)
        _stage_weights(wo_hbm, wo_ref)
        wvt_ref[...] = win_ref[:, OFF_V:OFF_V + KV_W].astype(jnp.float32).T.astype(jnp.bfloat16)
        kj = lax.broadcasted_iota(jnp.int32, (BLOCK, BLOCK), 0)
        qi = lax.broadcasted_iota(jnp.int32, (BLOCK, BLOCK), 1)
        for blk in range(3):
            dist = jnp.abs(qi + BLOCK - (kj + blk * BLOCK))
            distf = dist.astype(jnp.float32)
            for h in range(N_Q_HEADS):
                bias_ref[3 * h + blk] = jnp.where(dist <= WINDOW,
                                                  -(ALIBI_SLOPES[h] * LOG2_E) * distf, NEG_INF)
        bias_ref[3 * N_Q_HEADS] = jnp.full((BLOCK, BLOCK), NEG_INF, jnp.float32)
        er = lax.broadcasted_iota(jnp.int32, (KV_W, KV_W), 0) // HEAD_DIM
        ec = lax.broadcasted_iota(jnp.int32, (KV_W, KV_W), 1) // HEAD_DIM
        e_ref[...] = jnp.where(er == ec, 1.0, 0.0).astype(jnp.bfloat16)
        def ones_rows(lanes):
            rows = lax.broadcasted_iota(jnp.int32, (VT_ROWS - HEAD_DIM, lanes), 0)
            return jnp.where(rows == 0, 1.0, 0.0).astype(jnp.bfloat16)

        for g in range(N_KV_HEADS):
            vt_ref[g, HEAD_DIM:, :] = ones_rows(t + 2 * BLOCK)
            vc_ref[g, HEAD_DIM:, :] = ones_rows(2 * BLOCK)
        h0 = _rms_mod(x_ref[0:BLOCK, :], gain, scale1, shift1).astype(jnp.bfloat16)
        store_kv(*project_kv(h0), kc_ref, vc_ref, BLOCK)
        zero_k = jnp.zeros((BLOCK, 2 * HEAD_DIM), jnp.bfloat16)
        zero_v = jnp.zeros((HEAD_DIM, BLOCK), jnp.bfloat16)
        for g in range(N_KV_HEADS):
            kc_ref[g, 0:BLOCK, :] = zero_k
            vc_ref[g, 0:HEAD_DIM, 0:BLOCK] = zero_v

    h_tile = _rms_mod(x_ref[...], gain, scale1, shift1)
    h_ref[0:t, :] = h_tile.astype(jnp.bfloat16)
    ht = h_ref[0:t, :]
    q = jnp.dot(ht, win_ref[:, OFF_Q:OFF_Q + D_MODEL], preferred_element_type=jnp.float32)

    h_after = _rms_mod(xn_ref[...], g1n_ref[...], modn_ref[1:2, :], modn_ref[0:1, :])
    h_ref[t:, :] = h_after.astype(jnp.bfloat16)
    h_before = jnp.where(i > 0, _rms_mod(xp_ref[...], gain, scale1, shift1), 0.0)
    hc_ref[...] = jnp.concatenate(
        [h_before, h_tile, jnp.where(i < n_tiles - 1, h_after[:SUBLANES], 0.0)],
        axis=0).astype(jnp.bfloat16)

    for g in range(N_KV_HEADS):
        k2_ref[g, 0:2 * BLOCK, :] = kc_ref[g]
        vt_ref[g, :, 0:2 * BLOCK] = vc_ref[g]
    k, vt = project_kv(h_ref[BLOCK:, :])

    cb = jnp.dot(ht, win_ref[:, OFF_CB:OFF_CB + D_CONV], preferred_element_type=jnp.float32)
    ga = jnp.dot(ht, win_ref[:, OFF_GA:OFF_GA + D_MODEL], preferred_element_type=jnp.float32)
    ya_ref[...] = jax.nn.sigmoid(ga) * cb

    store_kv(k, vt, k2_ref, vt_ref, 2 * BLOCK)

    qscale = qg_ref[...] * (LOG2_E / math.sqrt(HEAD_DIM))
    for c in range(D_MODEL // KV_W):
        qc = q[:, c * KV_W:(c + 1) * KV_W]
        qn_ref[:, c * KV_W:(c + 1) * KV_W] = (
            qc * lax.rsqrt(_group_mean_sq(qc, e_ref) + EPS) * qscale[:, c * KV_W:(c + 1) * KV_W]
        ).astype(jnp.bfloat16)

    def conv_block(c):
        cs = slice(c * SIDE_COLS, (c + 1) * SIDE_COLS)
        hc = hc_ref[...]
        cc = jnp.dot(hc, win_ref[:, OFF_CC + c * SIDE_COLS:OFF_CC + (c + 1) * SIDE_COLS],
                     preferred_element_type=jnp.float32)
        cu = jnp.dot(hc, win_ref[:, OFF_CU + c * SIDE_COLS:OFF_CU + (c + 1) * SIDE_COLS],
                     preferred_element_type=jnp.float32)
        ya_ref[:, cs] = ya_ref[:, cs] * _dwconv3(cc * cu, convw_ref[:, cs])

    def gate_b_block(c):
        cs = slice(c * SIDE_COLS, (c + 1) * SIDE_COLS)
        gb = jnp.dot(ht, win_ref[:, OFF_GB + c * SIDE_COLS:OFF_GB + (c + 1) * SIDE_COLS],
                     preferred_element_type=jnp.float32)
        sgb_ref[:, cs] = jax.nn.sigmoid(gb)

    side_jobs = []
    for c in range(D_MODEL // SIDE_COLS):
        side_jobs.append(lambda c=c: conv_block(c))
        side_jobs.append(lambda c=c: gate_b_block(c))

    qlower = lax.broadcasted_iota(jnp.int32, (BLOCK, 2 * HEAD_DIM), 1) < HEAD_DIM
    zero_q = jnp.zeros((BLOCK, 2 * HEAD_DIM), jnp.bfloat16)
    mask_blk = 3 * N_Q_HEADS

    def scores_t(qb, g):
        r0 = qb * BLOCK
        kb = k2_ref[g, r0:r0 + 3 * BLOCK, :]
        qs = []
        for half in range(GROUP // 2):
            c0 = (g * GROUP + 2 * half) * HEAD_DIM
            pair = qn_ref[r0:r0 + BLOCK, c0:c0 + 2 * HEAD_DIM]
            qs.append(jnp.where(qlower, pair, zero_q))
            qs.append(jnp.where(qlower, zero_q, pair))
        q4 = jnp.concatenate(qs, axis=0)
        return lax.dot_general(kb, q4, (((1,), (1,)), ((), ())),
                               preferred_element_type=jnp.float32)

    def attend(qb, g, st):
        r0 = qb * BLOCK
        ps = []
        sink_terms = []
        for j in range(GROUP):
            hh = g * GROUP + j
            idx = [3 * hh, 3 * hh + 1, 3 * hh + 2]
            if qb == 0:
                idx[0] = jnp.where(i == 0, mask_blk, idx[0])
            if qb == t // BLOCK - 1:
                idx[2] = jnp.where(i == n_tiles - 1, mask_blk, idx[2])
            s = jnp.concatenate(
                [st[blk * BLOCK:(blk + 1) * BLOCK, j * BLOCK:(j + 1) * BLOCK] + bias_ref[idx[blk]]
                 for blk in range(3)], axis=0)
            sink = sink_ref[hh] * LOG2_E
            m = jnp.maximum(jnp.max(s, axis=0, keepdims=True), sink)
            ps.append(jnp.exp2(s - m).astype(jnp.bfloat16))
            sink_terms.append(jnp.exp2(sink - m))
        p4 = jnp.concatenate(ps, axis=1)
        ot = jnp.dot(vt_ref[g, :, r0:r0 + 3 * BLOCK], p4,
                     preferred_element_type=jnp.float32)
        den = ot[HEAD_DIM:HEAD_DIM + 1, :] + jnp.concatenate(sink_terms, axis=1)
        y = ot[0:HEAD_DIM, :] / den
        for j in range(GROUP):
            hh = g * GROUP + j
            ybt_ref[hh * HEAD_DIM:(hh + 1) * HEAD_DIM, r0:r0 + BLOCK] = y[:, j * BLOCK:(j + 1) * BLOCK]

    stages = [(qb, g) for qb in range(t // BLOCK) for g in range(N_KV_HEADS)]
    st_next = scores_t(*stages[0])
    for n, (qb, g) in enumerate(stages):
        st = st_next
        if n + 1 < len(stages):
            st_next = scores_t(*stages[n + 1])
        if n % 2 == 0 and side_jobs:
            side_jobs.pop(0)()
        attend(qb, g, st)
    for job in side_jobs:
        job()

    for g in range(N_KV_HEADS):
        kc_ref[g] = k2_ref[g, t:t + 2 * BLOCK, :]
        vc_ref[g] = vt_ref[g, :, t:t + 2 * BLOCK]

    yb = ybt_ref[...].T
    mrg_ref[...] = (ya_ref[...] + sgb_ref[...] * yb).astype(jnp.bfloat16)
    out = jnp.dot(mrg_ref[...], wo_ref[...], preferred_element_type=jnp.float32)
    o_ref[...] = x_ref[...] + gate1 * out


def _token_mixer(x, mod, norm_gain, w_in, conv_w, q_gain, k_gain, sink, w_o):
    b, s, d = x.shape
    t = SEQ_TILE
    assert s % t == 0 and t % BLOCK == 0
    n_tiles = s // t
    bpt = t // BLOCK
    rpt = t // SUBLANES
    n_blocks = s // BLOCK

    def after(bi, i):
        blk = jnp.minimum(bi * n_blocks + (i + 1) * bpt, b * n_blocks - 1)
        return blk // n_blocks, blk % n_blocks

    const = lambda shape: pl.BlockSpec(shape, lambda bi, i, *_: (0,) * len(shape),
                                       pipeline_mode=pl.Buffered(1))
    grid_spec = pltpu.PrefetchScalarGridSpec(
        num_scalar_prefetch=1,
        grid=(b, n_tiles),
        in_specs=[
            pl.BlockSpec((None, t, d), lambda bi, i, *_: (bi, i, 0)),
            pl.BlockSpec((None, SUBLANES, d), lambda bi, i, *_: (bi, jnp.maximum(i * rpt - 1, 0), 0)),
            pl.BlockSpec((None, BLOCK, d), lambda bi, i, *_: (*after(bi, i), 0)),
            pl.BlockSpec((None, 6, d), lambda bi, i, *_: (bi, 0, 0)),
            pl.BlockSpec((None, 6, d), lambda bi, i, *_: (after(bi, i)[0], 0, 0)),
            const((1, d)),
            pl.BlockSpec(memory_space=pl.ANY),
            const((3, D_CONV)),
            const((1, d)),
            const((1, KV_W)),
            pl.BlockSpec(memory_space=pl.ANY),
        ],
        out_specs=pl.BlockSpec((None, t, d), lambda bi, i, *_: (bi, i, 0)),
        scratch_shapes=[
            pltpu.VMEM((d, D_IN), jnp.bfloat16),
            pltpu.VMEM((d, d), jnp.bfloat16),
            pltpu.VMEM((KV_W, d), jnp.bfloat16),
            pltpu.VMEM((t + BLOCK, d), jnp.bfloat16),
            pltpu.VMEM((t + 2 * SUBLANES, d), jnp.bfloat16),
            pltpu.VMEM((N_KV_HEADS, t + 2 * BLOCK, 2 * HEAD_DIM), jnp.bfloat16),
            pltpu.VMEM((N_KV_HEADS, VT_ROWS, t + 2 * BLOCK), jnp.bfloat16),
            pltpu.VMEM((N_KV_HEADS, 2 * BLOCK, 2 * HEAD_DIM), jnp.bfloat16),
            pltpu.VMEM((N_KV_HEADS, VT_ROWS, 2 * BLOCK), jnp.bfloat16),
            pltpu.VMEM((t, d), jnp.bfloat16),
            pltpu.VMEM((t, d), jnp.float32),
            pltpu.VMEM((t, d), jnp.float32),
            pltpu.VMEM((d, t), jnp.float32),
            pltpu.VMEM((t, d), jnp.bfloat16),
            pltpu.VMEM((3 * N_Q_HEADS + 1, BLOCK, BLOCK), jnp.float32),
            pltpu.VMEM((KV_W, KV_W), jnp.bfloat16),
        ],
    )
    return pl.pallas_call(
        _mixer_kernel,
        grid_spec=grid_spec,
        out_shape=jax.ShapeDtypeStruct((b, s, d), jnp.float32),
        compiler_params=pltpu.CompilerParams(
            dimension_semantics=("arbitrary", "arbitrary"),
            vmem_limit_bytes=VMEM_LIMIT_BYTES),
        name="token_mixer",
    )(sink, x, x, x, mod, mod, norm_gain.reshape(1, d), w_in, conv_w,
      jnp.tile(q_gain, N_Q_HEADS).reshape(1, d), jnp.tile(k_gain, N_KV_HEADS).reshape(1, KV_W), w_o)


def _ffn_kernel(x_ref, xp_ref, xn_ref,
                mod_ref, g2n_ref, wup_hbm, fcw_ref, fcb_ref, wdn_hbm,
                o_ref,
                wup_ref,
                wdn_ref,
                ht_ref,
                he_ref,
                act_ref):
    t = x_ref.shape[0]
    i = pl.program_id(1)
    n_tiles = pl.num_programs(1)

    @pl.when(jnp.logical_and(pl.program_id(0) == 0, i == 0))
    def _():
        _stage_weights(wup_hbm, wup_ref)
        _stage_weights(wdn_hbm, wdn_ref)

    shift2 = mod_ref[3:4, :]
    scale2 = mod_ref[4:5, :]
    gate2 = mod_ref[5:6, :]
    gain = g2n_ref[...]

    h_prev = _rms_mod(xp_ref[...], gain, scale2, shift2)
    h_next = _rms_mod(xn_ref[...], gain, scale2, shift2)
    h_prev = jnp.where(i > 0, h_prev, 0.0)
    h_next = jnp.where(i < n_tiles - 1, h_next, 0.0)
    h_tile = _rms_mod(x_ref[...], gain, scale2, shift2)
    ht_ref[...] = h_tile.astype(jnp.bfloat16)
    he_ref[...] = jnp.concatenate([h_prev, h_tile, h_next], axis=0).astype(jnp.bfloat16)

    for c in range(D_FF // FF_CHUNK):
        c0 = c * FF_CHUNK
        a_ext = jnp.dot(he_ref[...], wup_ref[:, c0:c0 + FF_CHUNK],
                        preferred_element_type=jnp.float32)
        gbr = jnp.dot(ht_ref[...], wup_ref[:, D_FF + c0:D_FF + c0 + FF_CHUNK],
                      preferred_element_type=jnp.float32)
        a = _dwconv3(a_ext, fcw_ref[:, c0:c0 + FF_CHUNK]) + fcb_ref[:, c0:c0 + FF_CHUNK]
        gelu = 0.5 * a * (1.0 + jnp.tanh(math.sqrt(2.0 / math.pi) * (a + 0.044715 * (a * a * a))))
        act_ref[:, c0:c0 + FF_CHUNK] = (gelu * gbr).astype(jnp.bfloat16)

    y = jnp.dot(act_ref[...], wdn_ref[...], preferred_element_type=jnp.float32)
    o_ref[...] = x_ref[...] + gate2 * y


def _conv_ffn(x, mod, norm_gain, w_up, ffn_conv_w, ffn_conv_b, w_down):
    b, s, d = x.shape
    t = SEQ_TILE
    assert s % t == 0 and D_FF % FF_CHUNK == 0
    n_tiles = s // t
    rpt = t // SUBLANES
    n_groups = s // SUBLANES

    const = lambda shape: pl.BlockSpec(shape, lambda bi, i: (0,) * len(shape),
                                       pipeline_mode=pl.Buffered(1))
    return pl.pallas_call(
        _ffn_kernel,
        grid=(b, n_tiles),
        in_specs=[
            pl.BlockSpec((None, t, d), lambda bi, i: (bi, i, 0)),
            pl.BlockSpec((None, SUBLANES, d), lambda bi, i: (bi, jnp.maximum(i * rpt - 1, 0), 0)),
            pl.BlockSpec((None, SUBLANES, d),
                         lambda bi, i: (bi, jnp.minimum((i + 1) * rpt, n_groups - 1), 0)),
            pl.BlockSpec((None, 6, d), lambda bi, i: (bi, 0, 0)),
            const((1, d)),
            pl.BlockSpec(memory_space=pl.ANY),
            const((3, D_FF)),
            const((1, D_FF)),
            pl.BlockSpec(memory_space=pl.ANY),
        ],
        out_specs=pl.BlockSpec((None, t, d), lambda bi, i: (bi, i, 0)),
        out_shape=jax.ShapeDtypeStruct((b, s, d), jnp.float32),
        scratch_shapes=[
            pltpu.VMEM((d, 2 * D_FF), jnp.bfloat16),
            pltpu.VMEM((D_FF, d), jnp.bfloat16),
            pltpu.VMEM((t, d), jnp.bfloat16),
            pltpu.VMEM((t + 2 * SUBLANES, d), jnp.bfloat16),
            pltpu.VMEM((t, D_FF), jnp.bfloat16),
        ],
        compiler_params=pltpu.CompilerParams(
            dimension_semantics=("arbitrary", "arbitrary"),
            vmem_limit_bytes=VMEM_LIMIT_BYTES),
        name="conv_ffn",
    )(x, x, x, mod, norm_gain.reshape(1, d), w_up, ffn_conv_w, ffn_conv_b.reshape(1, D_FF), w_down)


def kernel(x, c, w_ada, b_ada, norm1_gain, w_in, conv_w, q_gain, k_gain, sink, w_o, norm2_gain,
           w_up, ffn_conv_w, ffn_conv_b, w_down):
    depth = w_ada.shape[0]
    b, s, d = x.shape
    for l in range(depth):
        mod = _modulation(c, w_ada[l], b_ada[l]).reshape(b, 6, d)
        x = _token_mixer(x, mod, norm1_gain[l], w_in[l], conv_w[l], q_gain[l], k_gain[l],
                         sink[l], w_o[l])
        x = _conv_ffn(x, mod, norm2_gain[l], w_up[l], ffn_conv_w[l], ffn_conv_b[l], w_down[l])
    return x
```

```python
import math

import jax
import jax.numpy as jnp
from jax import lax
from jax.experimental import pallas as pl
from jax.experimental.pallas import tpu as pltpu

D_MODEL = 1024
HEAD_DIM = 64
N_Q_HEADS = D_MODEL // HEAD_DIM
N_KV_HEADS = N_Q_HEADS // 4
GROUP = N_Q_HEADS // N_KV_HEADS
D_CONV = D_MODEL
WINDOW = 128
BLOCK = 128
D_FF = ((8 * D_MODEL // 3 + 127) // 128) * 128
EPS = 1e-6
NEG_INF = -1e30
LOG2_E = math.log2(math.e)

OFF_CB = 0
OFF_CC = OFF_CB + D_CONV
OFF_CU = OFF_CC + D_CONV
OFF_Q = OFF_CU + D_CONV
OFF_K = OFF_Q + N_Q_HEADS * HEAD_DIM
OFF_V = OFF_K + N_KV_HEADS * HEAD_DIM
OFF_GA = OFF_V + N_KV_HEADS * HEAD_DIM
OFF_GB = OFF_GA + D_MODEL
D_IN = OFF_GB + D_MODEL

KV_W = N_KV_HEADS * HEAD_DIM
SUBLANES = 8
BF16_ROWS = 16
VT_ROWS = HEAD_DIM + BF16_ROWS
SEQ_TILE = 512
FF_CHUNK = 256
SIDE_COLS = 256
WEIGHT_CHUNK_BYTES = 2 * 1024 * 1024
WEIGHT_DMA_DEPTH = 4
VMEM_LIMIT_BYTES = 56 * 1024 * 1024

ALIBI_SLOPES = [2.0 ** (-8.0 * (h + 1) / N_Q_HEADS) for h in range(N_Q_HEADS)]


def _rms_mod(xv, gain, scale, shift):
    ms = jnp.mean(xv * xv, axis=-1, keepdims=True)
    return xv * lax.rsqrt(ms + EPS) * (gain * (1.0 + scale)) + shift


def _dwconv3(v_ext, w):
    rows = v_ext.shape[0]
    t = rows - 2 * SUBLANES
    prev = pltpu.roll(v_ext, 1, 0)[SUBLANES:SUBLANES + t]
    nxt = pltpu.roll(v_ext, rows - 1, 0)[SUBLANES:SUBLANES + t]
    return w[0:1, :] * prev + w[1:2, :] * v_ext[SUBLANES:SUBLANES + t] + w[2:3, :] * nxt


def _stage_weights(src_hbm, dst_ref):
    rows, cols = src_hbm.shape
    chunk_rows = max(r for r in range(BF16_ROWS, rows + 1, BF16_ROWS)
                     if rows % r == 0 and r * cols * 4 <= WEIGHT_CHUNK_BYTES
                     and rows // r >= WEIGHT_DMA_DEPTH)
    n_chunks = rows // chunk_rows

    def body(stage_ref, sem_ref):
        def copy(c):
            slot = lax.rem(c, WEIGHT_DMA_DEPTH)
            r0 = pl.multiple_of(c * chunk_rows, chunk_rows)
            return pltpu.make_async_copy(src_hbm.at[pl.ds(r0, chunk_rows), :],
                                         stage_ref.at[slot], sem_ref.at[slot])

        for c in range(WEIGHT_DMA_DEPTH - 1):
            copy(c).start()

        def step(c, carry):
            @pl.when(c + WEIGHT_DMA_DEPTH - 1 < n_chunks)
            def _():
                copy(c + WEIGHT_DMA_DEPTH - 1).start()

            copy(c).wait()
            r0 = pl.multiple_of(c * chunk_rows, chunk_rows)
            dst_ref[pl.ds(r0, chunk_rows), :] = (
                stage_ref[lax.rem(c, WEIGHT_DMA_DEPTH)].astype(jnp.bfloat16))
            return carry

        lax.fori_loop(0, n_chunks, step, 0)

    pl.run_scoped(body, pltpu.VMEM((WEIGHT_DMA_DEPTH, chunk_rows, cols), jnp.float32),
                  pltpu.SemaphoreType.DMA((WEIGHT_DMA_DEPTH,)))


def _ada_kernel(c_ref, w_ref, b_ref, o_ref):
    c = c_ref[...]
    act = c * jax.nn.sigmoid(c)
    o_ref[...] = jnp.dot(act.astype(jnp.bfloat16), w_ref[...].astype(jnp.bfloat16),
                         preferred_element_type=jnp.float32) + b_ref[...]


def _modulation(c, w_ada, b_ada):
    b, d = c.shape
    n = w_ada.shape[1]
    rows = SUBLANES
    c_pad = jnp.zeros((rows, d), c.dtype).at[:b].set(c)
    tn = d
    out = pl.pallas_call(
        _ada_kernel,
        grid=(n // tn,),
        in_specs=[
            pl.BlockSpec((rows, d), lambda j: (0, 0)),
            pl.BlockSpec((d, tn), lambda j: (0, j)),
            pl.BlockSpec((1, tn), lambda j: (0, j)),
        ],
        out_specs=pl.BlockSpec((rows, tn), lambda j: (0, j)),
        out_shape=jax.ShapeDtypeStruct((rows, n), jnp.float32),
        compiler_params=pltpu.CompilerParams(dimension_semantics=("arbitrary",)),
        name="ada_modulation",
    )(c_pad, w_ada, b_ada.reshape(1, n))
    return out[:b]


def _group_mean_sq(v):
    sq = v * v
    lower = lax.broadcasted_iota(jnp.int32, (1, 2 * HEAD_DIM), 1) < HEAD_DIM
    sums = []
    for c in range(v.shape[1] // (2 * HEAD_DIM)):
        pair = sq[:, c * 2 * HEAD_DIM:(c + 1) * 2 * HEAD_DIM]
        both = jnp.sum(pair, axis=-1, keepdims=True)
        first = jnp.sum(jnp.where(lower, pair, 0.0), axis=-1, keepdims=True)
        sums.append(jnp.where(lower, first, both - first))
    return jnp.concatenate(sums, axis=1) * (1.0 / HEAD_DIM)


def _mixer_kernel(sink_ref,
                  x_ref, xp_ref, xn_ref,
                  mod_ref, modn_ref,
                  g1n_ref, win_hbm, convw_ref, qg_ref, kg_ref, wo_hbm,
                  o_ref,
                  win_ref,
                  wo_ref,
                  wvt_ref,
                  h_ref,
                  hc_ref,
                  k2_ref,
                  vt_ref,
                  kc_ref,
                  vc_ref,
                  qn_ref,
                  ya_ref,
                  sgb_ref,
                  ybt_ref,
                  mrg_ref,
                  bias_ref):
    t = x_ref.shape[0]
    i = pl.program_id(1)
    n_tiles = pl.num_programs(1)
    first = jnp.logical_and(pl.program_id(0) == 0, i == 0)
    shift1 = mod_ref[0:1, :]
    scale1 = mod_ref[1:2, :]
    gate1 = mod_ref[2:3, :]
    gain = g1n_ref[...]
    lower = lax.broadcasted_iota(jnp.int32, (1, 2 * HEAD_DIM), 1) < HEAD_DIM

    def project_kv(h_rows):
        k = jnp.dot(h_rows, win_ref[:, OFF_K:OFF_K + KV_W], preferred_element_type=jnp.float32)
        vt = lax.dot_general(wvt_ref[...], h_rows, (((1,), (1,)), ((), ())),
                             preferred_element_type=jnp.float32)
        return k, vt

    def store_kv(k, vt, k_dst, v_dst, r0):
        rows = k.shape[0]
        kn = k * lax.rsqrt(_group_mean_sq(k) + EPS) * kg_ref[...]
        for pair in range(N_KV_HEADS // 2):
            tile = kn[:, pair * 2 * HEAD_DIM:(pair + 1) * 2 * HEAD_DIM]
            swapped = pltpu.roll(tile, HEAD_DIM, 1)
            k_dst[2 * pair, r0:r0 + rows, :] = jnp.where(lower, tile, swapped).astype(jnp.bfloat16)
            k_dst[2 * pair + 1, r0:r0 + rows, :] = jnp.where(lower, swapped, tile).astype(jnp.bfloat16)
        for g in range(N_KV_HEADS):
            v_dst[g, 0:HEAD_DIM, r0:r0 + rows] = (
                vt[g * HEAD_DIM:(g + 1) * HEAD_DIM, :].astype(jnp.bfloat16))

    @pl.when(first)
    def _():
        _stage_weights(win_hbm, win_ref)
        _stage_weights(wo_hbm, wo_ref)
        wvt_ref[...] = win_ref[:, OFF_V:OFF_V + KV_W].astype(jnp.float32).T.astype(jnp.bfloat16)
        kj = lax.broadcasted_iota(jnp.int32, (BLOCK, BLOCK), 0)
        qi = lax.broadcasted_iota(jnp.int32, (BLOCK, BLOCK), 1)
        for blk in range(3):
            dist = jnp.abs(qi + BLOCK - (kj + blk * BLOCK))
            distf = dist.astype(jnp.float32)
            for h in range(N_Q_HEADS):
                bias_ref[3 * h + blk] = jnp.where(dist <= WINDOW,
                                                  -(ALIBI_SLOPES[h] * LOG2_E) * distf, NEG_INF)
        bias_ref[3 * N_Q_HEADS] = jnp.full((BLOCK, BLOCK), NEG_INF, jnp.float32)
        def ones_rows(lanes):
            rows = lax.broadcasted_iota(jnp.int32, (VT_ROWS - HEAD_DIM, lanes), 0)
            return jnp.where(rows == 0, 1.0, 0.0).astype(jnp.bfloat16)

        for g in range(N_KV_HEADS):
            vt_ref[g, HEAD_DIM:, :] = ones_rows(t + 2 * BLOCK)
            vc_ref[g, HEAD_DIM:, :] = ones_rows(2 * BLOCK)
        h0 = _rms_mod(x_ref[0:BLOCK, :], gain, scale1, shift1).astype(jnp.bfloat16)
        store_kv(*project_kv(h0), kc_ref, vc_ref, BLOCK)
        zero_k = jnp.zeros((BLOCK, 2 * HEAD_DIM), jnp.bfloat16)
        zero_v = jnp.zeros((HEAD_DIM, BLOCK), jnp.bfloat16)
        for g in range(N_KV_HEADS):
            kc_ref[g, 0:BLOCK, :] = zero_k
            vc_ref[g, 0:HEAD_DIM, 0:BLOCK] = zero_v

    h_tile = _rms_mod(x_ref[...], gain, scale1, shift1)
    h_ref[0:t, :] = h_tile.astype(jnp.bfloat16)
    ht = h_ref[0:t, :]
    q = jnp.dot(ht, win_ref[:, OFF_Q:OFF_Q + D_MODEL], preferred_element_type=jnp.float32)

    h_after = _rms_mod(xn_ref[...], g1n_ref[...], modn_ref[1:2, :], modn_ref[0:1, :])
    h_ref[t:, :] = h_after.astype(jnp.bfloat16)
    h_before = jnp.where(i > 0, _rms_mod(xp_ref[...], gain, scale1, shift1), 0.0)
    hc_ref[...] = jnp.concatenate(
        [h_before, h_tile, jnp.where(i < n_tiles - 1, h_after[:SUBLANES], 0.0)],
        axis=0).astype(jnp.bfloat16)

    for g in range(N_KV_HEADS):
        k2_ref[g, 0:2 * BLOCK, :] = kc_ref[g]
        vt_ref[g, :, 0:2 * BLOCK] = vc_ref[g]
    k, vt = project_kv(h_ref[BLOCK:, :])

    store_kv(k, vt, k2_ref, vt_ref, 2 * BLOCK)

    qscale = qg_ref[...] * (LOG2_E / math.sqrt(HEAD_DIM))
    for c in range(D_MODEL // KV_W):
        qc = q[:, c * KV_W:(c + 1) * KV_W]
        qn_ref[:, c * KV_W:(c + 1) * KV_W] = (
            qc * lax.rsqrt(_group_mean_sq(qc) + EPS) * qscale[:, c * KV_W:(c + 1) * KV_W]
        ).astype(jnp.bfloat16)

    def gate_a_block(c):
        cs = slice(c * SIDE_COLS, (c + 1) * SIDE_COLS)
        cb = jnp.dot(ht, win_ref[:, OFF_CB + c * SIDE_COLS:OFF_CB + (c + 1) * SIDE_COLS],
                     preferred_element_type=jnp.float32)
        ga = jnp.dot(ht, win_ref[:, OFF_GA + c * SIDE_COLS:OFF_GA + (c + 1) * SIDE_COLS],
                     preferred_element_type=jnp.float32)
        ya_ref[:, cs] = jax.nn.sigmoid(ga) * cb

    def conv_block(c):
        cs = slice(c * SIDE_COLS, (c + 1) * SIDE_COLS)
        hc = hc_ref[...]
        cc = jnp.dot(hc, win_ref[:, OFF_CC + c * SIDE_COLS:OFF_CC + (c + 1) * SIDE_COLS],
                     preferred_element_type=jnp.float32)
        cu = jnp.dot(hc, win_ref[:, OFF_CU + c * SIDE_COLS:OFF_CU + (c + 1) * SIDE_COLS],
                     preferred_element_type=jnp.float32)
        ya_ref[:, cs] = ya_ref[:, cs] * _dwconv3(cc * cu, convw_ref[:, cs])

    def gate_b_block(c):
        cs = slice(c * SIDE_COLS, (c + 1) * SIDE_COLS)
        gb = jnp.dot(ht, win_ref[:, OFF_GB + c * SIDE_COLS:OFF_GB + (c + 1) * SIDE_COLS],
                     preferred_element_type=jnp.float32)
        sgb_ref[:, cs] = jax.nn.sigmoid(gb)

    side_jobs = []
    for c in range(D_MODEL // SIDE_COLS):
        side_jobs.append(lambda c=c: gate_a_block(c))
        side_jobs.append(lambda c=c: conv_block(c))
        side_jobs.append(lambda c=c: gate_b_block(c))

    qlower = lax.broadcasted_iota(jnp.int32, (BLOCK, 2 * HEAD_DIM), 1) < HEAD_DIM
    zero_q = jnp.zeros((BLOCK, 2 * HEAD_DIM), jnp.bfloat16)
    mask_blk = 3 * N_Q_HEADS

    def scores_t(qb, g):
        r0 = qb * BLOCK
        kb = k2_ref[g, r0:r0 + 3 * BLOCK, :]
        qs = []
        for half in range(GROUP // 2):
            c0 = (g * GROUP + 2 * half) * HEAD_DIM
            pair = qn_ref[r0:r0 + BLOCK, c0:c0 + 2 * HEAD_DIM]
            qs.append(jnp.where(qlower, pair, zero_q))
            qs.append(jnp.where(qlower, zero_q, pair))
        q4 = jnp.concatenate(qs, axis=0)
        return lax.dot_general(kb, q4, (((1,), (1,)), ((), ())),
                               preferred_element_type=jnp.float32)

    def attend(qb, g, st):
        r0 = qb * BLOCK
        ps = []
        sink_terms = []
        for j in range(GROUP):
            hh = g * GROUP + j
            idx = [3 * hh, 3 * hh + 1, 3 * hh + 2]
            if qb == 0:
                idx[0] = jnp.where(i == 0, mask_blk, idx[0])
            if qb == t // BLOCK - 1:
                idx[2] = jnp.where(i == n_tiles - 1, mask_blk, idx[2])
            s = jnp.concatenate(
                [st[blk * BLOCK:(blk + 1) * BLOCK, j * BLOCK:(j + 1) * BLOCK] + bias_ref[idx[blk]]
                 for blk in range(3)], axis=0)
            sink = sink_ref[hh] * LOG2_E
            m = jnp.maximum(jnp.max(s, axis=0, keepdims=True), sink)
            ps.append(jnp.exp2(s - m).astype(jnp.bfloat16))
            sink_terms.append(jnp.exp2(sink - m))
        p4 = jnp.concatenate(ps, axis=1)
        ot = jnp.dot(vt_ref[g, :, r0:r0 + 3 * BLOCK], p4,
                     preferred_element_type=jnp.float32)
        den = ot[HEAD_DIM:HEAD_DIM + 1, :] + jnp.concatenate(sink_terms, axis=1)
        y = ot[0:HEAD_DIM, :] / den
        for j in range(GROUP):
            hh = g * GROUP + j
            ybt_ref[hh * HEAD_DIM:(hh + 1) * HEAD_DIM, r0:r0 + BLOCK] = y[:, j * BLOCK:(j + 1) * BLOCK]

    stages = [(qb, g) for qb in range(t // BLOCK) for g in range(N_KV_HEADS)]
    st_next = scores_t(*stages[0])
    for n, (qb, g) in enumerate(stages):
        st = st_next
        if n + 1 < len(stages):
            st_next = scores_t(*stages[n + 1])
        if n % 4 != 3 and side_jobs:
            side_jobs.pop(0)()
        attend(qb, g, st)
    for job in side_jobs:
        job()

    for g in range(N_KV_HEADS):
        kc_ref[g] = k2_ref[g, t:t + 2 * BLOCK, :]
        vc_ref[g] = vt_ref[g, :, t:t + 2 * BLOCK]

    yb = ybt_ref[...].T
    mrg_ref[...] = (ya_ref[...] + sgb_ref[...] * yb).astype(jnp.bfloat16)
    out = jnp.dot(mrg_ref[...], wo_ref[...], preferred_element_type=jnp.float32)
    o_ref[...] = x_ref[...] + gate1 * out


def _token_mixer(x, mod, norm_gain, w_in, conv_w, q_gain, k_gain, sink, w_o):
    b, s, d = x.shape
    t = SEQ_TILE
    assert s % t == 0 and t % BLOCK == 0
    n_tiles = s // t
    bpt = t // BLOCK
    rpt = t // SUBLANES
    n_blocks = s // BLOCK

    def after(bi, i):
        blk = jnp.minimum(bi * n_blocks + (i + 1) * bpt, b * n_blocks - 1)
        return blk // n_blocks, blk % n_blocks

    const = lambda shape: pl.BlockSpec(shape, lambda bi, i, *_: (0,) * len(shape),
                                       pipeline_mode=pl.Buffered(1))
    grid_spec = pltpu.PrefetchScalarGridSpec(
        num_scalar_prefetch=1,
        grid=(b, n_tiles),
        in_specs=[
            pl.BlockSpec((None, t, d), lambda bi, i, *_: (bi, i, 0)),
            pl.BlockSpec((None, SUBLANES, d), lambda bi, i, *_: (bi, jnp.maximum(i * rpt - 1, 0), 0)),
            pl.BlockSpec((None, BLOCK, d), lambda bi, i, *_: (*after(bi, i), 0)),
            pl.BlockSpec((None, 6, d), lambda bi, i, *_: (bi, 0, 0)),
            pl.BlockSpec((None, 6, d), lambda bi, i, *_: (after(bi, i)[0], 0, 0)),
            const((1, d)),
            pl.BlockSpec(memory_space=pl.ANY),
            const((3, D_CONV)),
            const((1, d)),
            const((1, KV_W)),
            pl.BlockSpec(memory_space=pl.ANY),
        ],
        out_specs=pl.BlockSpec((None, t, d), lambda bi, i, *_: (bi, i, 0)),
        scratch_shapes=[
            pltpu.VMEM((d, D_IN), jnp.bfloat16),
            pltpu.VMEM((d, d), jnp.bfloat16),
            pltpu.VMEM((KV_W, d), jnp.bfloat16),
            pltpu.VMEM((t + BLOCK, d), jnp.bfloat16),
            pltpu.VMEM((t + 2 * SUBLANES, d), jnp.bfloat16),
            pltpu.VMEM((N_KV_HEADS, t + 2 * BLOCK, 2 * HEAD_DIM), jnp.bfloat16),
            pltpu.VMEM((N_KV_HEADS, VT_ROWS, t + 2 * BLOCK), jnp.bfloat16),
            pltpu.VMEM((N_KV_HEADS, 2 * BLOCK, 2 * HEAD_DIM), jnp.bfloat16),
            pltpu.VMEM((N_KV_HEADS, VT_ROWS, 2 * BLOCK), jnp.bfloat16),
            pltpu.VMEM((t, d), jnp.bfloat16),
            pltpu.VMEM((t, d), jnp.float32),
            pltpu.VMEM((t, d), jnp.float32),
            pltpu.VMEM((d, t), jnp.float32),
            pltpu.VMEM((t, d), jnp.bfloat16),
            pltpu.VMEM((3 * N_Q_HEADS + 1, BLOCK, BLOCK), jnp.float32),
        ],
    )
    return pl.pallas_call(
        _mixer_kernel,
        grid_spec=grid_spec,
        out_shape=jax.ShapeDtypeStruct((b, s, d), jnp.float32),
        compiler_params=pltpu.CompilerParams(
            dimension_semantics=("arbitrary", "arbitrary"),
            vmem_limit_bytes=VMEM_LIMIT_BYTES),
        name="token_mixer",
    )(sink, x, x, x, mod, mod, norm_gain.reshape(1, d), w_in, conv_w,
      jnp.tile(q_gain, N_Q_HEADS).reshape(1, d), jnp.tile(k_gain, N_KV_HEADS).reshape(1, KV_W), w_o)


def _ffn_kernel(x_ref, xp_ref, xn_ref,
                mod_ref, g2n_ref, wup_hbm, fcw_ref, fcb_ref, wdn_hbm,
                o_ref,
                wup_ref,
                wdn_ref,
                ht_ref,
                he_ref,
                act_ref):
    t = x_ref.shape[0]
    i = pl.program_id(1)
    n_tiles = pl.num_programs(1)

    @pl.when(jnp.logical_and(pl.program_id(0) == 0, i == 0))
    def _():
        _stage_weights(wup_hbm, wup_ref)
        _stage_weights(wdn_hbm, wdn_ref)

    shift2 = mod_ref[3:4, :]
    scale2 = mod_ref[4:5, :]
    gate2 = mod_ref[5:6, :]
    gain = g2n_ref[...]

    h_prev = _rms_mod(xp_ref[...], gain, scale2, shift2)
    h_next = _rms_mod(xn_ref[...], gain, scale2, shift2)
    h_prev = jnp.where(i > 0, h_prev, 0.0)
    h_next = jnp.where(i < n_tiles - 1, h_next, 0.0)
    h_tile = _rms_mod(x_ref[...], gain, scale2, shift2)
    ht_ref[...] = h_tile.astype(jnp.bfloat16)
    he_ref[...] = jnp.concatenate([h_prev, h_tile, h_next], axis=0).astype(jnp.bfloat16)

    for c in range(D_FF // FF_CHUNK):
        c0 = c * FF_CHUNK
        a_ext = jnp.dot(he_ref[...], wup_ref[:, c0:c0 + FF_CHUNK],
                        preferred_element_type=jnp.float32)
        gbr = jnp.dot(ht_ref[...], wup_ref[:, D_FF + c0:D_FF + c0 + FF_CHUNK],
                      preferred_element_type=jnp.float32)
        a = _dwconv3(a_ext, fcw_ref[:, c0:c0 + FF_CHUNK]) + fcb_ref[:, c0:c0 + FF_CHUNK]
        gelu = 0.5 * a * (1.0 + jnp.tanh(math.sqrt(2.0 / math.pi) * (a + 0.044715 * (a * a * a))))
        act_ref[:, c0:c0 + FF_CHUNK] = (gelu * gbr).astype(jnp.bfloat16)

    y = jnp.dot(act_ref[...], wdn_ref[...], preferred_element_type=jnp.float32)
    o_ref[...] = x_ref[...] + gate2 * y


def _conv_ffn(x, mod, norm_gain, w_up, ffn_conv_w, ffn_conv_b, w_down):
    b, s, d = x.shape
    t = SEQ_TILE
    assert s % t == 0 and D_FF % FF_CHUNK == 0
    n_tiles = s // t
    rpt = t // SUBLANES
    n_groups = s // SUBLANES

    const = lambda shape: pl.BlockSpec(shape, lambda bi, i: (0,) * len(shape),
                                       pipeline_mode=pl.Buffered(1))
    return pl.pallas_call(
        _ffn_kernel,
        grid=(b, n_tiles),
        in_specs=[
            pl.BlockSpec((None, t, d), lambda bi, i: (bi, i, 0)),
            pl.BlockSpec((None, SUBLANES, d), lambda bi, i: (bi, jnp.maximum(i * rpt - 1, 0), 0)),
            pl.BlockSpec((None, SUBLANES, d),
                         lambda bi, i: (bi, jnp.minimum((i + 1) * rpt, n_groups - 1), 0)),
            pl.BlockSpec((None, 6, d), lambda bi, i: (bi, 0, 0)),
            const((1, d)),
            pl.BlockSpec(memory_space=pl.ANY),
            const((3, D_FF)),
            const((1, D_FF)),
            pl.BlockSpec(memory_space=pl.ANY),
        ],
        out_specs=pl.BlockSpec((None, t, d), lambda bi, i: (bi, i, 0)),
        out_shape=jax.ShapeDtypeStruct((b, s, d), jnp.float32),
        scratch_shapes=[
            pltpu.VMEM((d, 2 * D_FF), jnp.bfloat16),
            pltpu.VMEM((D_FF, d), jnp.bfloat16),
            pltpu.VMEM((t, d), jnp.bfloat16),
            pltpu.VMEM((t + 2 * SUBLANES, d), jnp.bfloat16),
            pltpu.VMEM((t, D_FF), jnp.bfloat16),
        ],
        compiler_params=pltpu.CompilerParams(
            dimension_semantics=("arbitrary", "arbitrary"),
            vmem_limit_bytes=VMEM_LIMIT_BYTES),
        name="conv_ffn",
    )(x, x, x, mod, norm_gain.reshape(1, d), w_up, ffn_conv_w, ffn_conv_b.reshape(1, D_FF), w_down)


def kernel(x, c, w_ada, b_ada, norm1_gain, w_in, conv_w, q_gain, k_gain, sink, w_o, norm2_gain,
           w_up, ffn_conv_w, ffn_conv_b, w_down):
    depth = w_ada.shape[0]
    b, s, d = x.shape
    for l in range(depth):
        mod = _modulation(c, w_ada[l], b_ada[l]).reshape(b, 6, d)
        x = _token_mixer(x, mod, norm1_gain[l], w_in[l], conv_w[l], q_gain[l], k_gain[l],
                         sink[l], w_o[l])
        x = _conv_ffn(x, mod, norm2_gain[l], w_up[l], ffn_conv_w[l], ffn_conv_b[l], w_down[l])
    return x
```

```python
import math

import jax
import jax.numpy as jnp
from jax import lax
from jax.experimental import pallas as pl
from jax.experimental.pallas import tpu as pltpu

D_MODEL = 1024
HEAD_DIM = 64
N_Q_HEADS = D_MODEL // HEAD_DIM
N_KV_HEADS = N_Q_HEADS // 4
GROUP = N_Q_HEADS // N_KV_HEADS
D_CONV = D_MODEL
WINDOW = 128
BLOCK = 128
D_FF = ((8 * D_MODEL // 3 + 127) // 128) * 128
EPS = 1e-6
NEG_INF = -1e30
LOG2_E = math.log2(math.e)

OFF_CB = 0
OFF_CC = OFF_CB + D_CONV
OFF_CU = OFF_CC + D_CONV
OFF_Q = OFF_CU + D_CONV
OFF_K = OFF_Q + N_Q_HEADS * HEAD_DIM
OFF_V = OFF_K + N_KV_HEADS * HEAD_DIM
OFF_GA = OFF_V + N_KV_HEADS * HEAD_DIM
OFF_GB = OFF_GA + D_MODEL
D_IN = OFF_GB + D_MODEL

KV_W = N_KV_HEADS * HEAD_DIM
SUBLANES = 8
BF16_ROWS = 16
VT_ROWS = HEAD_DIM + BF16_ROWS
SEQ_TILE = 512
FF_CHUNK = 256
SIDE_COLS = 256
WEIGHT_CHUNK_BYTES = 1024 * 1024
WEIGHT_DMA_DEPTH = 8
VMEM_LIMIT_BYTES = 56 * 1024 * 1024

ALIBI_SLOPES = [2.0 ** (-8.0 * (h + 1) / N_Q_HEADS) for h in range(N_Q_HEADS)]


def _rms_mod(xv, gain, scale, shift):
    ms = jnp.mean(xv * xv, axis=-1, keepdims=True)
    return xv * lax.rsqrt(ms + EPS) * (gain * (1.0 + scale)) + shift


def _dwconv3(v_ext, w):
    rows = v_ext.shape[0]
    t = rows - 2 * SUBLANES
    prev = pltpu.roll(v_ext, 1, 0)[SUBLANES:SUBLANES + t]
    nxt = pltpu.roll(v_ext, rows - 1, 0)[SUBLANES:SUBLANES + t]
    return w[0:1, :] * prev + w[1:2, :] * v_ext[SUBLANES:SUBLANES + t] + w[2:3, :] * nxt


def _stage_weights(src_hbm, dst_ref):
    rows, cols = src_hbm.shape
    chunk_rows = max(r for r in range(BF16_ROWS, rows + 1, BF16_ROWS)
                     if rows % r == 0 and r * cols * 4 <= WEIGHT_CHUNK_BYTES
                     and rows // r >= WEIGHT_DMA_DEPTH)
    n_chunks = rows // chunk_rows

    def body(stage_ref, sem_ref):
        def copy(c):
            slot = lax.rem(c, WEIGHT_DMA_DEPTH)
            r0 = pl.multiple_of(c * chunk_rows, chunk_rows)
            return pltpu.make_async_copy(src_hbm.at[pl.ds(r0, chunk_rows), :],
                                         stage_ref.at[slot], sem_ref.at[slot])

        for c in range(WEIGHT_DMA_DEPTH - 1):
            copy(c).start()

        def step(c, carry):
            @pl.when(c + WEIGHT_DMA_DEPTH - 1 < n_chunks)
            def _():
                copy(c + WEIGHT_DMA_DEPTH - 1).start()

            copy(c).wait()
            r0 = pl.multiple_of(c * chunk_rows, chunk_rows)
            dst_ref[pl.ds(r0, chunk_rows), :] = (
                stage_ref[lax.rem(c, WEIGHT_DMA_DEPTH)].astype(jnp.bfloat16))
            return carry

        lax.fori_loop(0, n_chunks, step, 0)

    pl.run_scoped(body, pltpu.VMEM((WEIGHT_DMA_DEPTH, chunk_rows, cols), jnp.float32),
                  pltpu.SemaphoreType.DMA((WEIGHT_DMA_DEPTH,)))


def _ada_kernel(c_ref, w_ref, b_ref, o_ref):
    c = c_ref[...]
    act = c * jax.nn.sigmoid(c)
    o_ref[...] = jnp.dot(act.astype(jnp.bfloat16), w_ref[...].astype(jnp.bfloat16),
                         preferred_element_type=jnp.float32) + b_ref[...]


def _modulation(c, w_ada, b_ada):
    b, d = c.shape
    n = w_ada.shape[1]
    rows = SUBLANES
    c_pad = jnp.zeros((rows, d), c.dtype).at[:b].set(c)
    tn = d
    out = pl.pallas_call(
        _ada_kernel,
        grid=(n // tn,),
        in_specs=[
            pl.BlockSpec((rows, d), lambda j: (0, 0)),
            pl.BlockSpec((d, tn), lambda j: (0, j)),
            pl.BlockSpec((1, tn), lambda j: (0, j)),
        ],
        out_specs=pl.BlockSpec((rows, tn), lambda j: (0, j)),
        out_shape=jax.ShapeDtypeStruct((rows, n), jnp.float32),
        compiler_params=pltpu.CompilerParams(dimension_semantics=("arbitrary",)),
        name="ada_modulation",
    )(c_pad, w_ada, b_ada.reshape(1, n))
    return out[:b]


def _group_mean_sq(v):
    sq = v * v
    lower = lax.broadcasted_iota(jnp.int32, (1, 2 * HEAD_DIM), 1) < HEAD_DIM
    sums = []
    for c in range(v.shape[1] // (2 * HEAD_DIM)):
        pair = sq[:, c * 2 * HEAD_DIM:(c + 1) * 2 * HEAD_DIM]
        both = jnp.sum(pair, axis=-1, keepdims=True)
        first = jnp.sum(jnp.where(lower, pair, 0.0), axis=-1, keepdims=True)
        sums.append(jnp.where(lower, first, both - first))
    return jnp.concatenate(sums, axis=1) * (1.0 / HEAD_DIM)


def _mixer_kernel(sink_ref,
                  x_ref, xp_ref, xn_ref,
                  mod_ref, modn_ref,
                  g1n_ref, win_hbm, convw_ref, qg_ref, kg_ref, wo_hbm,
                  o_ref,
                  win_ref,
                  wo_ref,
                  wvt_ref,
                  h_ref,
                  hc_ref,
                  k2_ref,
                  vt_ref,
                  kc_ref,
                  vc_ref,
                  qn_ref,
                  ya_ref,
                  sgb_ref,
                  ybt_ref,
                  mrg_ref,
                  bias_ref):
    t = x_ref.shape[0]
    i = pl.program_id(1)
    n_tiles = pl.num_programs(1)
    first = jnp.logical_and(pl.program_id(0) == 0, i == 0)
    shift1 = mod_ref[0:1, :]
    scale1 = mod_ref[1:2, :]
    gate1 = mod_ref[2:3, :]
    gain = g1n_ref[...]
    lower = lax.broadcasted_iota(jnp.int32, (1, 2 * HEAD_DIM), 1) < HEAD_DIM

    def project_kv(h_rows):
        k = jnp.dot(h_rows, win_ref[:, OFF_K:OFF_K + KV_W], preferred_element_type=jnp.float32)
        vt = lax.dot_general(wvt_ref[...], h_rows, (((1,), (1,)), ((), ())),
                             preferred_element_type=jnp.float32)
        return k, vt

    def store_kv(k, vt, k_dst, v_dst, r0):
        rows = k.shape[0]
        kn = k * lax.rsqrt(_group_mean_sq(k) + EPS) * kg_ref[...]
        for pair in range(N_KV_HEADS // 2):
            tile = kn[:, pair * 2 * HEAD_DIM:(pair + 1) * 2 * HEAD_DIM]
            swapped = pltpu.roll(tile, HEAD_DIM, 1)
            k_dst[2 * pair, r0:r0 + rows, :] = jnp.where(lower, tile, swapped).astype(jnp.bfloat16)
            k_dst[2 * pair + 1, r0:r0 + rows, :] = jnp.where(lower, swapped, tile).astype(jnp.bfloat16)
        for g in range(N_KV_HEADS):
            v_dst[g, 0:HEAD_DIM, r0:r0 + rows] = (
                vt[g * HEAD_DIM:(g + 1) * HEAD_DIM, :].astype(jnp.bfloat16))

    @pl.when(first)
    def _():
        _stage_weights(win_hbm, win_ref)
        _stage_weights(wo_hbm, wo_ref)
        wvt_ref[...] = win_ref[:, OFF_V:OFF_V + KV_W].astype(jnp.float32).T.astype(jnp.bfloat16)
        kj = lax.broadcasted_iota(jnp.int32, (BLOCK, BLOCK), 0)
        qi = lax.broadcasted_iota(jnp.int32, (BLOCK, BLOCK), 1)
        for blk in range(3):
            dist = jnp.abs(qi + BLOCK - (kj + blk * BLOCK))
            distf = dist.astype(jnp.float32)
            for h in range(N_Q_HEADS):
                bias_ref[3 * h + blk] = jnp.where(dist <= WINDOW,
                                                  -(ALIBI_SLOPES[h] * LOG2_E) * distf, NEG_INF)
        bias_ref[3 * N_Q_HEADS] = jnp.full((BLOCK, BLOCK), NEG_INF, jnp.float32)
        def ones_rows(lanes):
            rows = lax.broadcasted_iota(jnp.int32, (VT_ROWS - HEAD_DIM, lanes), 0)
            return jnp.where(rows == 0, 1.0, 0.0).astype(jnp.bfloat16)

        for g in range(N_KV_HEADS):
            vt_ref[g, HEAD_DIM:, :] = ones_rows(t + 2 * BLOCK)
            vc_ref[g, HEAD_DIM:, :] = ones_rows(2 * BLOCK)
        h0 = _rms_mod(x_ref[0:BLOCK, :], gain, scale1, shift1).astype(jnp.bfloat16)
        store_kv(*project_kv(h0), kc_ref, vc_ref, BLOCK)
        zero_k = jnp.zeros((BLOCK, 2 * HEAD_DIM), jnp.bfloat16)
        zero_v = jnp.zeros((HEAD_DIM, BLOCK), jnp.bfloat16)
        for g in range(N_KV_HEADS):
            kc_ref[g, 0:BLOCK, :] = zero_k
            vc_ref[g, 0:HEAD_DIM, 0:BLOCK] = zero_v

    h_tile = _rms_mod(x_ref[...], gain, scale1, shift1)
    h_ref[0:t, :] = h_tile.astype(jnp.bfloat16)
    ht = h_ref[0:t, :]
    q = jnp.dot(ht, win_ref[:, OFF_Q:OFF_Q + D_MODEL], preferred_element_type=jnp.float32)

    h_after = _rms_mod(xn_ref[...], g1n_ref[...], modn_ref[1:2, :], modn_ref[0:1, :])
    h_ref[t:, :] = h_after.astype(jnp.bfloat16)
    h_before = jnp.where(i > 0, _rms_mod(xp_ref[...], gain, scale1, shift1), 0.0)
    hc_ref[...] = jnp.concatenate(
        [h_before, h_tile, jnp.where(i < n_tiles - 1, h_after[:SUBLANES], 0.0)],
        axis=0).astype(jnp.bfloat16)

    for g in range(N_KV_HEADS):
        k2_ref[g, 0:2 * BLOCK, :] = kc_ref[g]
        vt_ref[g, :, 0:2 * BLOCK] = vc_ref[g]
    k, vt = project_kv(h_ref[BLOCK:, :])

    store_kv(k, vt, k2_ref, vt_ref, 2 * BLOCK)

    qscale = qg_ref[...] * (LOG2_E / math.sqrt(HEAD_DIM))
    for c in range(D_MODEL // KV_W):
        qc = q[:, c * KV_W:(c + 1) * KV_W]
        qn_ref[:, c * KV_W:(c + 1) * KV_W] = (
            qc * lax.rsqrt(_group_mean_sq(qc) + EPS) * qscale[:, c * KV_W:(c + 1) * KV_W]
        ).astype(jnp.bfloat16)

    def gate_a_block(c):
        cs = slice(c * SIDE_COLS, (c + 1) * SIDE_COLS)
        cb = jnp.dot(ht, win_ref[:, OFF_CB + c * SIDE_COLS:OFF_CB + (c + 1) * SIDE_COLS],
                     preferred_element_type=jnp.float32)
        ga = jnp.dot(ht, win_ref[:, OFF_GA + c * SIDE_COLS:OFF_GA + (c + 1) * SIDE_COLS],
                     preferred_element_type=jnp.float32)
        ya_ref[:, cs] = jax.nn.sigmoid(ga) * cb

    def conv_block(c):
        cs = slice(c * SIDE_COLS, (c + 1) * SIDE_COLS)
        hc = hc_ref[...]
        cc = jnp.dot(hc, win_ref[:, OFF_CC + c * SIDE_COLS:OFF_CC + (c + 1) * SIDE_COLS],
                     preferred_element_type=jnp.float32)
        cu = jnp.dot(hc, win_ref[:, OFF_CU + c * SIDE_COLS:OFF_CU + (c + 1) * SIDE_COLS],
                     preferred_element_type=jnp.float32)
        ya_ref[:, cs] = ya_ref[:, cs] * _dwconv3(cc * cu, convw_ref[:, cs])

    def gate_b_block(c):
        cs = slice(c * SIDE_COLS, (c + 1) * SIDE_COLS)
        gb = jnp.dot(ht, win_ref[:, OFF_GB + c * SIDE_COLS:OFF_GB + (c + 1) * SIDE_COLS],
                     preferred_element_type=jnp.float32)
        sgb_ref[:, cs] = jax.nn.sigmoid(gb)

    side_jobs = []
    for c in range(D_MODEL // SIDE_COLS):
        side_jobs.append(lambda c=c: gate_a_block(c))
        side_jobs.append(lambda c=c: conv_block(c))
        side_jobs.append(lambda c=c: gate_b_block(c))

    qlower = lax.broadcasted_iota(jnp.int32, (BLOCK, 2 * HEAD_DIM), 1) < HEAD_DIM
    zero_q = jnp.zeros((BLOCK, 2 * HEAD_DIM), jnp.bfloat16)
    mask_blk = 3 * N_Q_HEADS

    def scores_t(qb, g):
        r0 = qb * BLOCK
        kb = k2_ref[g, r0:r0 + 3 * BLOCK, :]
        qs = []
        for half in range(GROUP // 2):
            c0 = (g * GROUP + 2 * half) * HEAD_DIM
            pair = qn_ref[r0:r0 + BLOCK, c0:c0 + 2 * HEAD_DIM]
            qs.append(jnp.where(qlower, pair, zero_q))
            qs.append(jnp.where(qlower, zero_q, pair))
        q4 = jnp.concatenate(qs, axis=0)
        return lax.dot_general(kb, q4, (((1,), (1,)), ((), ())),
                               preferred_element_type=jnp.float32)

    def attend(qb, g, st):
        r0 = qb * BLOCK
        ps = []
        sink_terms = []
        for j in range(GROUP):
            hh = g * GROUP + j
            idx = [3 * hh, 3 * hh + 1, 3 * hh + 2]
            if qb == 0:
                idx[0] = jnp.where(i == 0, mask_blk, idx[0])
            if qb == t // BLOCK - 1:
                idx[2] = jnp.where(i == n_tiles - 1, mask_blk, idx[2])
            s = jnp.concatenate(
                [st[blk * BLOCK:(blk + 1) * BLOCK, j * BLOCK:(j + 1) * BLOCK] + bias_ref[idx[blk]]
                 for blk in range(3)], axis=0)
            sink = sink_ref[hh] * LOG2_E
            m = jnp.maximum(jnp.max(s, axis=0, keepdims=True), sink)
            ps.append(jnp.exp2(s - m).astype(jnp.bfloat16))
            sink_terms.append(jnp.exp2(sink - m))
        p4 = jnp.concatenate(ps, axis=1)
        ot = jnp.dot(vt_ref[g, :, r0:r0 + 3 * BLOCK], p4,
                     preferred_element_type=jnp.float32)
        den = ot[HEAD_DIM:HEAD_DIM + 1, :] + jnp.concatenate(sink_terms, axis=1)
        y = ot[0:HEAD_DIM, :] / den
        for j in range(GROUP):
            hh = g * GROUP + j
            ybt_ref[hh * HEAD_DIM:(hh + 1) * HEAD_DIM, r0:r0 + BLOCK] = y[:, j * BLOCK:(j + 1) * BLOCK]

    stages = [(qb, g) for qb in range(t // BLOCK) for g in range(N_KV_HEADS)]
    st_next = scores_t(*stages[0])
    for n, (qb, g) in enumerate(stages):
        st = st_next
        if n + 1 < len(stages):
            st_next = scores_t(*stages[n + 1])
        if n % 4 != 3 and side_jobs:
            side_jobs.pop(0)()
        attend(qb, g, st)
    for job in side_jobs:
        job()

    for g in range(N_KV_HEADS):
        kc_ref[g] = k2_ref[g, t:t + 2 * BLOCK, :]
        vc_ref[g] = vt_ref[g, :, t:t + 2 * BLOCK]

    yb = ybt_ref[...].T
    mrg_ref[...] = (ya_ref[...] + sgb_ref[...] * yb).astype(jnp.bfloat16)
    out = jnp.dot(mrg_ref[...], wo_ref[...], preferred_element_type=jnp.float32)
    o_ref[...] = x_ref[...] + gate1 * out


def _token_mixer(x, mod, norm_gain, w_in, conv_w, q_gain, k_gain, sink, w_o):
    b, s, d = x.shape
    t = SEQ_TILE
    assert s % t == 0 and t % BLOCK == 0
    n_tiles = s // t
    bpt = t // BLOCK
    rpt = t // SUBLANES
    n_blocks = s // BLOCK

    def after(bi, i):
        blk = jnp.minimum(bi * n_blocks + (i + 1) * bpt, b * n_blocks - 1)
        return blk // n_blocks, blk % n_blocks

    const = lambda shape: pl.BlockSpec(shape, lambda bi, i, *_: (0,) * len(shape),
                                       pipeline_mode=pl.Buffered(1))
    grid_spec = pltpu.PrefetchScalarGridSpec(
        num_scalar_prefetch=1,
        grid=(b, n_tiles),
        in_specs=[
            pl.BlockSpec((None, t, d), lambda bi, i, *_: (bi, i, 0)),
            pl.BlockSpec((None, SUBLANES, d), lambda bi, i, *_: (bi, jnp.maximum(i * rpt - 1, 0), 0)),
            pl.BlockSpec((None, BLOCK, d), lambda bi, i, *_: (*after(bi, i), 0)),
            pl.BlockSpec((None, 6, d), lambda bi, i, *_: (bi, 0, 0)),
            pl.BlockSpec((None, 6, d), lambda bi, i, *_: (after(bi, i)[0], 0, 0)),
            const((1, d)),
            pl.BlockSpec(memory_space=pl.ANY),
            const((3, D_CONV)),
            const((1, d)),
            const((1, KV_W)),
            pl.BlockSpec(memory_space=pl.ANY),
        ],
        out_specs=pl.BlockSpec((None, t, d), lambda bi, i, *_: (bi, i, 0)),
        scratch_shapes=[
            pltpu.VMEM((d, D_IN), jnp.bfloat16),
            pltpu.VMEM((d, d), jnp.bfloat16),
            pltpu.VMEM((KV_W, d), jnp.bfloat16),
            pltpu.VMEM((t + BLOCK, d), jnp.bfloat16),
            pltpu.VMEM((t + 2 * SUBLANES, d), jnp.bfloat16),
            pltpu.VMEM((N_KV_HEADS, t + 2 * BLOCK, 2 * HEAD_DIM), jnp.bfloat16),
            pltpu.VMEM((N_KV_HEADS, VT_ROWS, t + 2 * BLOCK), jnp.bfloat16),
            pltpu.VMEM((N_KV_HEADS, 2 * BLOCK, 2 * HEAD_DIM), jnp.bfloat16),
            pltpu.VMEM((N_KV_HEADS, VT_ROWS, 2 * BLOCK), jnp.bfloat16),
            pltpu.VMEM((t, d), jnp.bfloat16),
            pltpu.VMEM((t, d), jnp.float32),
            pltpu.VMEM((t, d), jnp.float32),
            pltpu.VMEM((d, t), jnp.float32),
            pltpu.VMEM((t, d), jnp.bfloat16),
            pltpu.VMEM((3 * N_Q_HEADS + 1, BLOCK, BLOCK), jnp.float32),
        ],
    )
    return pl.pallas_call(
        _mixer_kernel,
        grid_spec=grid_spec,
        out_shape=jax.ShapeDtypeStruct((b, s, d), jnp.float32),
        compiler_params=pltpu.CompilerParams(
            dimension_semantics=("arbitrary", "arbitrary"),
            vmem_limit_bytes=VMEM_LIMIT_BYTES),
        name="token_mixer",
    )(sink, x, x, x, mod, mod, norm_gain.reshape(1, d), w_in, conv_w,
      jnp.tile(q_gain, N_Q_HEADS).reshape(1, d), jnp.tile(k_gain, N_KV_HEADS).reshape(1, KV_W), w_o)


def _ffn_kernel(x_ref, xp_ref, xn_ref,
                mod_ref, g2n_ref, wup_hbm, fcw_ref, fcb_ref, wdn_hbm,
                o_ref,
                wup_ref,
                wdn_ref,
                ht_ref,
                he_ref,
                act_ref):
    t = x_ref.shape[0]
    i = pl.program_id(1)
    n_tiles = pl.num_programs(1)

    @pl.when(jnp.logical_and(pl.program_id(0) == 0, i == 0))
    def _():
        _stage_weights(wup_hbm, wup_ref)
        _stage_weights(wdn_hbm, wdn_ref)

    shift2 = mod_ref[3:4, :]
    scale2 = mod_ref[4:5, :]
    gate2 = mod_ref[5:6, :]
    gain = g2n_ref[...]

    h_prev = _rms_mod(xp_ref[...], gain, scale2, shift2)
    h_next = _rms_mod(xn_ref[...], gain, scale2, shift2)
    h_prev = jnp.where(i > 0, h_prev, 0.0)
    h_next = jnp.where(i < n_tiles - 1, h_next, 0.0)
    h_tile = _rms_mod(x_ref[...], gain, scale2, shift2)
    ht_ref[...] = h_tile.astype(jnp.bfloat16)
    he_ref[...] = jnp.concatenate([h_prev, h_tile, h_next], axis=0).astype(jnp.bfloat16)

    for c in range(D_FF // FF_CHUNK):
        c0 = c * FF_CHUNK
        a_ext = jnp.dot(he_ref[...], wup_ref[:, c0:c0 + FF_CHUNK],
                        preferred_element_type=jnp.float32)
        gbr = jnp.dot(ht_ref[...], wup_ref[:, D_FF + c0:D_FF + c0 + FF_CHUNK],
                      preferred_element_type=jnp.float32)
        a = _dwconv3(a_ext, fcw_ref[:, c0:c0 + FF_CHUNK]) + fcb_ref[:, c0:c0 + FF_CHUNK]
        gelu = 0.5 * a * (1.0 + jnp.tanh(math.sqrt(2.0 / math.pi) * (a + 0.044715 * (a * a * a))))
        act_ref[:, c0:c0 + FF_CHUNK] = (gelu * gbr).astype(jnp.bfloat16)

    y = jnp.dot(act_ref[...], wdn_ref[...], preferred_element_type=jnp.float32)
    o_ref[...] = x_ref[...] + gate2 * y


def _conv_ffn(x, mod, norm_gain, w_up, ffn_conv_w, ffn_conv_b, w_down):
    b, s, d = x.shape
    t = SEQ_TILE
    assert s % t == 0 and D_FF % FF_CHUNK == 0
    n_tiles = s // t
    rpt = t // SUBLANES
    n_groups = s // SUBLANES

    const = lambda shape: pl.BlockSpec(shape, lambda bi, i: (0,) * len(shape),
                                       pipeline_mode=pl.Buffered(1))
    return pl.pallas_call(
        _ffn_kernel,
        grid=(b, n_tiles),
        in_specs=[
            pl.BlockSpec((None, t, d), lambda bi, i: (bi, i, 0)),
            pl.BlockSpec((None, SUBLANES, d), lambda bi, i: (bi, jnp.maximum(i * rpt - 1, 0), 0)),
            pl.BlockSpec((None, SUBLANES, d),
                         lambda bi, i: (bi, jnp.minimum((i + 1) * rpt, n_groups - 1), 0)),
            pl.BlockSpec((None, 6, d), lambda bi, i: (bi, 0, 0)),
            const((1, d)),
            pl.BlockSpec(memory_space=pl.ANY),
            const((3, D_FF)),
            const((1, D_FF)),
            pl.BlockSpec(memory_space=pl.ANY),
        ],
        out_specs=pl.BlockSpec((None, t, d), lambda bi, i: (bi, i, 0)),
        out_shape=jax.ShapeDtypeStruct((b, s, d), jnp.float32),
        scratch_shapes=[
            pltpu.VMEM((d, 2 * D_FF), jnp.bfloat16),
            pltpu.VMEM((D_FF, d), jnp.bfloat16),
            pltpu.VMEM((t, d), jnp.bfloat16),
            pltpu.VMEM((t + 2 * SUBLANES, d), jnp.bfloat16),
            pltpu.VMEM((t, D_FF), jnp.bfloat16),
        ],
        compiler_params=pltpu.CompilerParams(
            dimension_semantics=("arbitrary", "arbitrary"),
            vmem_limit_bytes=VMEM_LIMIT_BYTES),
        name="conv_ffn",
    )(x, x, x, mod, norm_gain.reshape(1, d), w_up, ffn_conv_w, ffn_conv_b.reshape(1, D_FF), w_down)


def kernel(x, c, w_ada, b_ada, norm1_gain, w_in, conv_w, q_gain, k_gain, sink, w_o, norm2_gain,
           w_up, ffn_conv_w, ffn_conv_b, w_down):
    depth = w_ada.shape[0]
    b, s, d = x.shape
    for l in range(depth):
        mod = _modulation(c, w_ada[l], b_ada[l]).reshape(b, 6, d)
        x = _token_mixer(x, mod, norm1_gain[l], w_in[l], conv_w[l], q_gain[l], k_gain[l],
                         sink[l], w_o[l])
        x = _conv_ffn(x, mod, norm2_gain[l], w_up[l], ffn_conv_w[l], ffn_conv_b[l], w_down[l])
    return x
```

```python
import math

import jax
import jax.numpy as jnp
from jax import lax
from jax.experimental import pallas as pl
from jax.experimental.pallas import tpu as pltpu

D_MODEL = 1024
HEAD_DIM = 64
N_Q_HEADS = D_MODEL // HEAD_DIM
N_KV_HEADS = N_Q_HEADS // 4
GROUP = N_Q_HEADS // N_KV_HEADS
D_CONV = D_MODEL
WINDOW = 128
BLOCK = 128
D_FF = ((8 * D_MODEL // 3 + 127) // 128) * 128
EPS = 1e-6
NEG_INF = -1e30
LOG2_E = math.log2(math.e)

OFF_CB = 0
OFF_CC = OFF_CB + D_CONV
OFF_CU = OFF_CC + D_CONV
OFF_Q = OFF_CU + D_CONV
OFF_K = OFF_Q + N_Q_HEADS * HEAD_DIM
OFF_V = OFF_K + N_KV_HEADS * HEAD_DIM
OFF_GA = OFF_V + N_KV_HEADS * HEAD_DIM
OFF_GB = OFF_GA + D_MODEL
D_IN = OFF_GB + D_MODEL

KV_W = N_KV_HEADS * HEAD_DIM
SUBLANES = 8
BF16_ROWS = 16
VT_ROWS = HEAD_DIM + BF16_ROWS
SEQ_TILE = 512
FF_CHUNK = 256
SIDE_COLS = 256
ADA_SPLIT = 4
WEIGHT_CHUNK_BYTES = 1024 * 1024
WEIGHT_DMA_DEPTH = 8
VMEM_LIMIT_BYTES = 56 * 1024 * 1024

ALIBI_SLOPES = [2.0 ** (-8.0 * (h + 1) / N_Q_HEADS) for h in range(N_Q_HEADS)]


def _rms_mod(xv, gain, scale, shift):
    ms = jnp.mean(xv * xv, axis=-1, keepdims=True)
    return xv * lax.rsqrt(ms + EPS) * (gain * (1.0 + scale)) + shift


def _dwconv3(v_ext, w):
    rows = v_ext.shape[0]
    t = rows - 2 * SUBLANES
    prev = pltpu.roll(v_ext, 1, 0)[SUBLANES:SUBLANES + t]
    nxt = pltpu.roll(v_ext, rows - 1, 0)[SUBLANES:SUBLANES + t]
    return w[0:1, :] * prev + w[1:2, :] * v_ext[SUBLANES:SUBLANES + t] + w[2:3, :] * nxt


def _stage_weights(src_hbm, dst_ref):
    rows, cols = src_hbm.shape
    chunk_rows = max(r for r in range(BF16_ROWS, rows + 1, BF16_ROWS)
                     if rows % r == 0 and r * cols * 4 <= WEIGHT_CHUNK_BYTES
                     and rows // r >= WEIGHT_DMA_DEPTH)
    n_chunks = rows // chunk_rows

    def body(stage_ref, sem_ref):
        def copy(c):
            slot = lax.rem(c, WEIGHT_DMA_DEPTH)
            r0 = pl.multiple_of(c * chunk_rows, chunk_rows)
            return pltpu.make_async_copy(src_hbm.at[pl.ds(r0, chunk_rows), :],
                                         stage_ref.at[slot], sem_ref.at[slot])

        for c in range(WEIGHT_DMA_DEPTH - 1):
            copy(c).start()

        def step(c, carry):
            @pl.when(c + WEIGHT_DMA_DEPTH - 1 < n_chunks)
            def _():
                copy(c + WEIGHT_DMA_DEPTH - 1).start()

            copy(c).wait()
            r0 = pl.multiple_of(c * chunk_rows, chunk_rows)
            dst_ref[pl.ds(r0, chunk_rows), :] = (
                stage_ref[lax.rem(c, WEIGHT_DMA_DEPTH)].astype(jnp.bfloat16))
            return carry

        lax.fori_loop(0, n_chunks, step, 0)

    pl.run_scoped(body, pltpu.VMEM((WEIGHT_DMA_DEPTH, chunk_rows, cols), jnp.float32),
                  pltpu.SemaphoreType.DMA((WEIGHT_DMA_DEPTH,)))


def _ada_kernel(c_ref, *refs):
    w_refs, b_ref, o_ref = refs[:ADA_SPLIT], refs[ADA_SPLIT], refs[ADA_SPLIT + 1]
    c = c_ref[...]
    act = (c * jax.nn.sigmoid(c)).astype(jnp.bfloat16)
    slab = c.shape[1] // ADA_SPLIT
    acc = b_ref[...]
    for r, w_ref in enumerate(w_refs):
        acc = acc + jnp.dot(act[:, r * slab:(r + 1) * slab], w_ref[...].astype(jnp.bfloat16),
                            preferred_element_type=jnp.float32)
    o_ref[...] = acc


def _modulation(c, w_ada, b_ada):
    b, d = c.shape
    n = w_ada.shape[1]
    rows = SUBLANES
    c_pad = jnp.zeros((rows, d), c.dtype).at[:b].set(c)
    tn = d
    out = pl.pallas_call(
        _ada_kernel,
        grid=(n // tn,),
        in_specs=[pl.BlockSpec((rows, d), lambda j: (0, 0))]
        + [pl.BlockSpec((d // ADA_SPLIT, tn), lambda j, r=r: (r, j)) for r in range(ADA_SPLIT)]
        + [pl.BlockSpec((1, tn), lambda j: (0, j))],
        out_specs=pl.BlockSpec((rows, tn), lambda j: (0, j)),
        out_shape=jax.ShapeDtypeStruct((rows, n), jnp.float32),
        compiler_params=pltpu.CompilerParams(dimension_semantics=("arbitrary",)),
        name="ada_modulation",
    )(c_pad, *([w_ada] * ADA_SPLIT), b_ada.reshape(1, n))
    return out[:b]


def _group_mean_sq(v):
    sq = v * v
    lower = lax.broadcasted_iota(jnp.int32, (1, 2 * HEAD_DIM), 1) < HEAD_DIM
    sums = []
    for c in range(v.shape[1] // (2 * HEAD_DIM)):
        pair = sq[:, c * 2 * HEAD_DIM:(c + 1) * 2 * HEAD_DIM]
        both = jnp.sum(pair, axis=-1, keepdims=True)
        first = jnp.sum(jnp.where(lower, pair, 0.0), axis=-1, keepdims=True)
        sums.append(jnp.where(lower, first, both - first))
    return jnp.concatenate(sums, axis=1) * (1.0 / HEAD_DIM)


def _mixer_kernel(sink_ref,
                  x_ref, xp_ref, xn_ref,
                  mod_ref, modn_ref,
                  g1n_ref, win_hbm, convw_ref, qg_ref, kg_ref, wo_hbm,
                  o_ref,
                  win_ref,
                  wo_ref,
                  wvt_ref,
                  h_ref,
                  hc_ref,
                  k2_ref,
                  vt_ref,
                  kc_ref,
                  vc_ref,
                  qn_ref,
                  ya_ref,
                  sgb_ref,
                  ybt_ref,
                  mrg_ref,
                  bias_ref):
    t = x_ref.shape[0]
    i = pl.program_id(1)
    n_tiles = pl.num_programs(1)
    first = jnp.logical_and(pl.program_id(0) == 0, i == 0)
    shift1 = mod_ref[0:1, :]
    scale1 = mod_ref[1:2, :]
    gate1 = mod_ref[2:3, :]
    gain = g1n_ref[...]
    lower = lax.broadcasted_iota(jnp.int32, (1, 2 * HEAD_DIM), 1) < HEAD_DIM

    def project_kv(h_rows):
        k = jnp.dot(h_rows, win_ref[:, OFF_K:OFF_K + KV_W], preferred_element_type=jnp.float32)
        vt = lax.dot_general(wvt_ref[...], h_rows, (((1,), (1,)), ((), ())),
                             preferred_element_type=jnp.float32)
        return k, vt

    def store_kv(k, vt, k_dst, v_dst, r0):
        rows = k.shape[0]
        kn = k * lax.rsqrt(_group_mean_sq(k) + EPS) * kg_ref[...]
        for pair in range(N_KV_HEADS // 2):
            tile = kn[:, pair * 2 * HEAD_DIM:(pair + 1) * 2 * HEAD_DIM]
            swapped = pltpu.roll(tile, HEAD_DIM, 1)
            k_dst[2 * pair, r0:r0 + rows, :] = jnp.where(lower, tile, swapped).astype(jnp.bfloat16)
            k_dst[2 * pair + 1, r0:r0 + rows, :] = jnp.where(lower, swapped, tile).astype(jnp.bfloat16)
        for g in range(N_KV_HEADS):
            v_dst[g, 0:HEAD_DIM, r0:r0 + rows] = (
                vt[g * HEAD_DIM:(g + 1) * HEAD_DIM, :].astype(jnp.bfloat16))

    @pl.when(first)
    def _():
        _stage_weights(win_hbm, win_ref)
        _stage_weights(wo_hbm, wo_ref)
        wvt_ref[...] = win_ref[:, OFF_V:OFF_V + KV_W].astype(jnp.float32).T.astype(jnp.bfloat16)
        kj = lax.broadcasted_iota(jnp.int32, (BLOCK, BLOCK), 0)
        qi = lax.broadcasted_iota(jnp.int32, (BLOCK, BLOCK), 1)
        for blk in range(3):
            dist = jnp.abs(qi + BLOCK - (kj + blk * BLOCK))
            distf = dist.astype(jnp.float32)
            for h in range(N_Q_HEADS):
                bias_ref[3 * h + blk] = jnp.where(dist <= WINDOW,
                                                  -(ALIBI_SLOPES[h] * LOG2_E) * distf, NEG_INF)
        bias_ref[3 * N_Q_HEADS] = jnp.full((BLOCK, BLOCK), NEG_INF, jnp.float32)
        def ones_rows(lanes):
            rows = lax.broadcasted_iota(jnp.int32, (VT_ROWS - HEAD_DIM, lanes), 0)
            return jnp.where(rows == 0, 1.0, 0.0).astype(jnp.bfloat16)

        for g in range(N_KV_HEADS):
            vt_ref[g, HEAD_DIM:, :] = ones_rows(t + 2 * BLOCK)
            vc_ref[g, HEAD_DIM:, :] = ones_rows(2 * BLOCK)
        h0 = _rms_mod(x_ref[0:BLOCK, :], gain, scale1, shift1).astype(jnp.bfloat16)
        store_kv(*project_kv(h0), kc_ref, vc_ref, BLOCK)
        zero_k = jnp.zeros((BLOCK, 2 * HEAD_DIM), jnp.bfloat16)
        zero_v = jnp.zeros((HEAD_DIM, BLOCK), jnp.bfloat16)
        for g in range(N_KV_HEADS):
            kc_ref[g, 0:BLOCK, :] = zero_k
            vc_ref[g, 0:HEAD_DIM, 0:BLOCK] = zero_v

    h_tile = _rms_mod(x_ref[...], gain, scale1, shift1)
    h_ref[0:t, :] = h_tile.astype(jnp.bfloat16)
    ht = h_ref[0:t, :]
    q = jnp.dot(ht, win_ref[:, OFF_Q:OFF_Q + D_MODEL], preferred_element_type=jnp.float32)

    h_after = _rms_mod(xn_ref[...], g1n_ref[...], modn_ref[1:2, :], modn_ref[0:1, :])
    h_ref[t:, :] = h_after.astype(jnp.bfloat16)
    h_before = jnp.where(i > 0, _rms_mod(xp_ref[...], gain, scale1, shift1), 0.0)
    hc_ref[...] = jnp.concatenate(
        [h_before, h_tile, jnp.where(i < n_tiles - 1, h_after[:SUBLANES], 0.0)],
        axis=0).astype(jnp.bfloat16)

    for g in range(N_KV_HEADS):
        k2_ref[g, 0:2 * BLOCK, :] = kc_ref[g]
        vt_ref[g, :, 0:2 * BLOCK] = vc_ref[g]
    k, vt = project_kv(h_ref[BLOCK:, :])

    store_kv(k, vt, k2_ref, vt_ref, 2 * BLOCK)

    qscale = qg_ref[...] * (LOG2_E / math.sqrt(HEAD_DIM))
    for c in range(D_MODEL // KV_W):
        qc = q[:, c * KV_W:(c + 1) * KV_W]
        qn_ref[:, c * KV_W:(c + 1) * KV_W] = (
            qc * lax.rsqrt(_group_mean_sq(qc) + EPS) * qscale[:, c * KV_W:(c + 1) * KV_W]
        ).astype(jnp.bfloat16)

    def gate_a_block(c):
        cs = slice(c * SIDE_COLS, (c + 1) * SIDE_COLS)
        cb = jnp.dot(ht, win_ref[:, OFF_CB + c * SIDE_COLS:OFF_CB + (c + 1) * SIDE_COLS],
                     preferred_element_type=jnp.float32)
        ga = jnp.dot(ht, win_ref[:, OFF_GA + c * SIDE_COLS:OFF_GA + (c + 1) * SIDE_COLS],
                     preferred_element_type=jnp.float32)
        ya_ref[:, cs] = jax.nn.sigmoid(ga) * cb

    def conv_block(c):
        cs = slice(c * SIDE_COLS, (c + 1) * SIDE_COLS)
        hc = hc_ref[...]
        cc = jnp.dot(hc, win_ref[:, OFF_CC + c * SIDE_COLS:OFF_CC + (c + 1) * SIDE_COLS],
                     preferred_element_type=jnp.float32)
        cu = jnp.dot(hc, win_ref[:, OFF_CU + c * SIDE_COLS:OFF_CU + (c + 1) * SIDE_COLS],
                     preferred_element_type=jnp.float32)
        ya_ref[:, cs] = ya_ref[:, cs] * _dwconv3(cc * cu, convw_ref[:, cs])

    def gate_b_block(c):
        cs = slice(c * SIDE_COLS, (c + 1) * SIDE_COLS)
        gb = jnp.dot(ht, win_ref[:, OFF_GB + c * SIDE_COLS:OFF_GB + (c + 1) * SIDE_COLS],
                     preferred_element_type=jnp.float32)
        sgb_ref[:, cs] = jax.nn.sigmoid(gb)

    side_jobs = []
    for c in range(D_MODEL // SIDE_COLS):
        side_jobs.append(lambda c=c: gate_a_block(c))
        side_jobs.append(lambda c=c: conv_block(c))
        side_jobs.append(lambda c=c: gate_b_block(c))

    qlower = lax.broadcasted_iota(jnp.int32, (BLOCK, 2 * HEAD_DIM), 1) < HEAD_DIM
    zero_q = jnp.zeros((BLOCK, 2 * HEAD_DIM), jnp.bfloat16)
    mask_blk = 3 * N_Q_HEADS

    def scores_t(qb, g):
        r0 = qb * BLOCK
        kb = k2_ref[g, r0:r0 + 3 * BLOCK, :]
        qs = []
        for half in range(GROUP // 2):
            c0 = (g * GROUP + 2 * half) * HEAD_DIM
            pair = qn_ref[r0:r0 + BLOCK, c0:c0 + 2 * HEAD_DIM]
            qs.append(jnp.where(qlower, pair, zero_q))
            qs.append(jnp.where(qlower, zero_q, pair))
        q4 = jnp.concatenate(qs, axis=0)
        return lax.dot_general(kb, q4, (((1,), (1,)), ((), ())),
                               preferred_element_type=jnp.float32)

    def attend(qb, g, st):
        r0 = qb * BLOCK
        ps = []
        sink_terms = []
        for j in range(GROUP):
            hh = g * GROUP + j
            idx = [3 * hh, 3 * hh + 1, 3 * hh + 2]
            if qb == 0:
                idx[0] = jnp.where(i == 0, mask_blk, idx[0])
            if qb == t // BLOCK - 1:
                idx[2] = jnp.where(i == n_tiles - 1, mask_blk, idx[2])
            s = jnp.concatenate(
                [st[blk * BLOCK:(blk + 1) * BLOCK, j * BLOCK:(j + 1) * BLOCK] + bias_ref[idx[blk]]
                 for blk in range(3)], axis=0)
            sink = sink_ref[hh] * LOG2_E
            m = jnp.maximum(jnp.max(s, axis=0, keepdims=True), sink)
            ps.append(jnp.exp2(s - m).astype(jnp.bfloat16))
            sink_terms.append(jnp.exp2(sink - m))
        p4 = jnp.concatenate(ps, axis=1)
        ot = jnp.dot(vt_ref[g, :, r0:r0 + 3 * BLOCK], p4,
                     preferred_element_type=jnp.float32)
        den = ot[HEAD_DIM:HEAD_DIM + 1, :] + jnp.concatenate(sink_terms, axis=1)
        y = ot[0:HEAD_DIM, :] / den
        for j in range(GROUP):
            hh = g * GROUP + j
            ybt_ref[hh * HEAD_DIM:(hh + 1) * HEAD_DIM, r0:r0 + BLOCK] = y[:, j * BLOCK:(j + 1) * BLOCK]

    stages = [(qb, g) for qb in range(t // BLOCK) for g in range(N_KV_HEADS)]
    st_next = scores_t(*stages[0])
    for n, (qb, g) in enumerate(stages):
        st = st_next
        if n + 1 < len(stages):
            st_next = scores_t(*stages[n + 1])
        if n % 4 != 3 and side_jobs:
            side_jobs.pop(0)()
        attend(qb, g, st)
    for job in side_jobs:
        job()

    for g in range(N_KV_HEADS):
        kc_ref[g] = k2_ref[g, t:t + 2 * BLOCK, :]
        vc_ref[g] = vt_ref[g, :, t:t + 2 * BLOCK]

    yb = ybt_ref[...].T
    mrg_ref[...] = (ya_ref[...] + sgb_ref[...] * yb).astype(jnp.bfloat16)
    out = jnp.dot(mrg_ref[...], wo_ref[...], preferred_element_type=jnp.float32)
    o_ref[...] = x_ref[...] + gate1 * out


def _token_mixer(x, mod, norm_gain, w_in, conv_w, q_gain, k_gain, sink, w_o):
    b, s, d = x.shape
    t = SEQ_TILE
    assert s % t == 0 and t % BLOCK == 0
    n_tiles = s // t
    bpt = t // BLOCK
    rpt = t // SUBLANES
    n_blocks = s // BLOCK

    def after(bi, i):
        blk = jnp.minimum(bi * n_blocks + (i + 1) * bpt, b * n_blocks - 1)
        return blk // n_blocks, blk % n_blocks

    const = lambda shape: pl.BlockSpec(shape, lambda bi, i, *_: (0,) * len(shape),
                                       pipeline_mode=pl.Buffered(1))
    grid_spec = pltpu.PrefetchScalarGridSpec(
        num_scalar_prefetch=1,
        grid=(b, n_tiles),
        in_specs=[
            pl.BlockSpec((None, t, d), lambda bi, i, *_: (bi, i, 0)),
            pl.BlockSpec((None, SUBLANES, d), lambda bi, i, *_: (bi, jnp.maximum(i * rpt - 1, 0), 0)),
            pl.BlockSpec((None, BLOCK, d), lambda bi, i, *_: (*after(bi, i), 0)),
            pl.BlockSpec((None, 6, d), lambda bi, i, *_: (bi, 0, 0)),
            pl.BlockSpec((None, 6, d), lambda bi, i, *_: (after(bi, i)[0], 0, 0)),
            const((1, d)),
            pl.BlockSpec(memory_space=pl.ANY),
            const((3, D_CONV)),
            const((1, d)),
            const((1, KV_W)),
            pl.BlockSpec(memory_space=pl.ANY),
        ],
        out_specs=pl.BlockSpec((None, t, d), lambda bi, i, *_: (bi, i, 0)),
        scratch_shapes=[
            pltpu.VMEM((d, D_IN), jnp.bfloat16),
            pltpu.VMEM((d, d), jnp.bfloat16),
            pltpu.VMEM((KV_W, d), jnp.bfloat16),
            pltpu.VMEM((t + BLOCK, d), jnp.bfloat16),
            pltpu.VMEM((t + 2 * SUBLANES, d), jnp.bfloat16),
            pltpu.VMEM((N_KV_HEADS, t + 2 * BLOCK, 2 * HEAD_DIM), jnp.bfloat16),
            pltpu.VMEM((N_KV_HEADS, VT_ROWS, t + 2 * BLOCK), jnp.bfloat16),
            pltpu.VMEM((N_KV_HEADS, 2 * BLOCK, 2 * HEAD_DIM), jnp.bfloat16),
            pltpu.VMEM((N_KV_HEADS, VT_ROWS, 2 * BLOCK), jnp.bfloat16),
            pltpu.VMEM((t, d), jnp.bfloat16),
            pltpu.VMEM((t, d), jnp.float32),
            pltpu.VMEM((t, d), jnp.float32),
            pltpu.VMEM((d, t), jnp.float32),
            pltpu.VMEM((t, d), jnp.bfloat16),
            pltpu.VMEM((3 * N_Q_HEADS + 1, BLOCK, BLOCK), jnp.float32),
        ],
    )
    return pl.pallas_call(
        _mixer_kernel,
        grid_spec=grid_spec,
        out_shape=jax.ShapeDtypeStruct((b, s, d), jnp.float32),
        compiler_params=pltpu.CompilerParams(
            dimension_semantics=("arbitrary", "arbitrary"),
            vmem_limit_bytes=VMEM_LIMIT_BYTES),
        name="token_mixer",
    )(sink, x, x, x, mod, mod, norm_gain.reshape(1, d), w_in, conv_w,
      jnp.tile(q_gain, N_Q_HEADS).reshape(1, d), jnp.tile(k_gain, N_KV_HEADS).reshape(1, KV_W), w_o)


def _ffn_kernel(x_ref, xp_ref, xn_ref,
                mod_ref, g2n_ref, wup_hbm, fcw_ref, fcb_ref, wdn_hbm,
                o_ref,
                wup_ref,
                wdn_ref,
                ht_ref,
                he_ref,
                act_ref):
    t = x_ref.shape[0]
    i = pl.program_id(1)
    n_tiles = pl.num_programs(1)

    @pl.when(jnp.logical_and(pl.program_id(0) == 0, i == 0))
    def _():
        _stage_weights(wup_hbm, wup_ref)
        _stage_weights(wdn_hbm, wdn_ref)

    shift2 = mod_ref[3:4, :]
    scale2 = mod_ref[4:5, :]
    gate2 = mod_ref[5:6, :]
    gain = g2n_ref[...]

    h_prev = _rms_mod(xp_ref[...], gain, scale2, shift2)
    h_next = _rms_mod(xn_ref[...], gain, scale2, shift2)
    h_prev = jnp.where(i > 0, h_prev, 0.0)
    h_next = jnp.where(i < n_tiles - 1, h_next, 0.0)
    h_tile = _rms_mod(x_ref[...], gain, scale2, shift2)
    ht_ref[...] = h_tile.astype(jnp.bfloat16)
    he_ref[...] = jnp.concatenate([h_prev, h_tile, h_next], axis=0).astype(jnp.bfloat16)

    for c in range(D_FF // FF_CHUNK):
        c0 = c * FF_CHUNK
        a_ext = jnp.dot(he_ref[...], wup_ref[:, c0:c0 + FF_CHUNK],
                        preferred_element_type=jnp.float32)
        gbr = jnp.dot(ht_ref[...], wup_ref[:, D_FF + c0:D_FF + c0 + FF_CHUNK],
                      preferred_element_type=jnp.float32)
        a = _dwconv3(a_ext, fcw_ref[:, c0:c0 + FF_CHUNK]) + fcb_ref[:, c0:c0 + FF_CHUNK]
        gelu = 0.5 * a * (1.0 + jnp.tanh(math.sqrt(2.0 / math.pi) * (a + 0.044715 * (a * a * a))))
        act_ref[:, c0:c0 + FF_CHUNK] = (gelu * gbr).astype(jnp.bfloat16)

    y = jnp.dot(act_ref[...], wdn_ref[...], preferred_element_type=jnp.float32)
    o_ref[...] = x_ref[...] + gate2 * y


def _conv_ffn(x, mod, norm_gain, w_up, ffn_conv_w, ffn_conv_b, w_down):
    b, s, d = x.shape
    t = SEQ_TILE
    assert s % t == 0 and D_FF % FF_CHUNK == 0
    n_tiles = s // t
    rpt = t // SUBLANES
    n_groups = s // SUBLANES

    const = lambda shape: pl.BlockSpec(shape, lambda bi, i: (0,) * len(shape),
                                       pipeline_mode=pl.Buffered(1))
    return pl.pallas_call(
        _ffn_kernel,
        grid=(b, n_tiles),
        in_specs=[
            pl.BlockSpec((None, t, d), lambda bi, i: (bi, i, 0)),
            pl.BlockSpec((None, SUBLANES, d), lambda bi, i: (bi, jnp.maximum(i * rpt - 1, 0), 0)),
            pl.BlockSpec((None, SUBLANES, d),
                         lambda bi, i: (bi, jnp.minimum((i + 1) * rpt, n_groups - 1), 0)),
            pl.BlockSpec((None, 6, d), lambda bi, i: (bi, 0, 0)),
            const((1, d)),
            pl.BlockSpec(memory_space=pl.ANY),
            const((3, D_FF)),
            const((1, D_FF)),
            pl.BlockSpec(memory_space=pl.ANY),
        ],
        out_specs=pl.BlockSpec((None, t, d), lambda bi, i: (bi, i, 0)),
        out_shape=jax.ShapeDtypeStruct((b, s, d), jnp.float32),
        scratch_shapes=[
            pltpu.VMEM((d, 2 * D_FF), jnp.bfloat16),
            pltpu.VMEM((D_FF, d), jnp.bfloat16),
            pltpu.VMEM((t, d), jnp.bfloat16),
            pltpu.VMEM((t + 2 * SUBLANES, d), jnp.bfloat16),
            pltpu.VMEM((t, D_FF), jnp.bfloat16),
        ],
        compiler_params=pltpu.CompilerParams(
            dimension_semantics=("arbitrary", "arbitrary"),
            vmem_limit_bytes=VMEM_LIMIT_BYTES),
        name="conv_ffn",
    )(x, x, x, mod, norm_gain.reshape(1, d), w_up, ffn_conv_w, ffn_conv_b.reshape(1, D_FF), w_down)


def kernel(x, c, w_ada, b_ada, norm1_gain, w_in, conv_w, q_gain, k_gain, sink, w_o, norm2_gain,
           w_up, ffn_conv_w, ffn_conv_b, w_down):
    depth = w_ada.shape[0]
    b, s, d = x.shape
    for l in range(depth):
        mod = _modulation(c, w_ada[l], b_ada[l]).reshape(b, 6, d)
        x = _token_mixer(x, mod, norm1_gain[l], w_in[l], conv_w[l], q_gain[l], k_gain[l],
                         sink[l], w_o[l])
        x = _conv_ffn(x, mod, norm2_gain[l], w_up[l], ffn_conv_w[l], ffn_conv_b[l], w_down[l])
    return x
```

```python
import math

import jax
import jax.numpy as jnp
from jax import lax
from jax.experimental import pallas as pl
from jax.experimental.pallas import tpu as pltpu

D_MODEL = 1024
HEAD_DIM = 64
N_Q_HEADS = D_MODEL // HEAD_DIM
N_KV_HEADS = N_Q_HEADS // 4
GROUP = N_Q_HEADS // N_KV_HEADS
D_CONV = D_MODEL
WINDOW = 128
BLOCK = 128
D_FF = ((8 * D_MODEL // 3 + 127) // 128) * 128
EPS = 1e-6
NEG_INF = -1e30
LOG2_E = math.log2(math.e)

OFF_CB = 0
OFF_CC = OFF_CB + D_CONV
OFF_CU = OFF_CC + D_CONV
OFF_Q = OFF_CU + D_CONV
OFF_K = OFF_Q + N_Q_HEADS * HEAD_DIM
OFF_V = OFF_K + N_KV_HEADS * HEAD_DIM
OFF_GA = OFF_V + N_KV_HEADS * HEAD_DIM
OFF_GB = OFF_GA + D_MODEL
D_IN = OFF_GB + D_MODEL

KV_W = N_KV_HEADS * HEAD_DIM
SUBLANES = 8
BF16_ROWS = 16
VT_ROWS = HEAD_DIM + BF16_ROWS
SEQ_TILE = 512
FF_CHUNK = 256
SIDE_COLS = 256
ADA_K_BLOCK = 128
WEIGHT_CHUNK_BYTES = 1024 * 1024
WEIGHT_DMA_DEPTH = 8
VMEM_LIMIT_BYTES = 56 * 1024 * 1024

ALIBI_SLOPES = [2.0 ** (-8.0 * (h + 1) / N_Q_HEADS) for h in range(N_Q_HEADS)]


def _rms_mod(xv, gain, scale, shift):
    ms = jnp.mean(xv * xv, axis=-1, keepdims=True)
    return xv * lax.rsqrt(ms + EPS) * (gain * (1.0 + scale)) + shift


def _dwconv3(v_ext, w):
    rows = v_ext.shape[0]
    t = rows - 2 * SUBLANES
    prev = pltpu.roll(v_ext, 1, 0)[SUBLANES:SUBLANES + t]
    nxt = pltpu.roll(v_ext, rows - 1, 0)[SUBLANES:SUBLANES + t]
    return w[0:1, :] * prev + w[1:2, :] * v_ext[SUBLANES:SUBLANES + t] + w[2:3, :] * nxt


def _stage_weights(src_hbm, dst_ref):
    rows, cols = src_hbm.shape
    chunk_rows = max(r for r in range(BF16_ROWS, rows + 1, BF16_ROWS)
                     if rows % r == 0 and r * cols * 4 <= WEIGHT_CHUNK_BYTES
                     and rows // r >= WEIGHT_DMA_DEPTH)
    n_chunks = rows // chunk_rows

    def body(stage_ref, sem_ref):
        def copy(c):
            slot = lax.rem(c, WEIGHT_DMA_DEPTH)
            r0 = pl.multiple_of(c * chunk_rows, chunk_rows)
            return pltpu.make_async_copy(src_hbm.at[pl.ds(r0, chunk_rows), :],
                                         stage_ref.at[slot], sem_ref.at[slot])

        for c in range(WEIGHT_DMA_DEPTH - 1):
            copy(c).start()

        def step(c, carry):
            @pl.when(c + WEIGHT_DMA_DEPTH - 1 < n_chunks)
            def _():
                copy(c + WEIGHT_DMA_DEPTH - 1).start()

            copy(c).wait()
            r0 = pl.multiple_of(c * chunk_rows, chunk_rows)
            dst_ref[pl.ds(r0, chunk_rows), :] = (
                stage_ref[lax.rem(c, WEIGHT_DMA_DEPTH)].astype(jnp.bfloat16))
            return carry

        lax.fori_loop(0, n_chunks, step, 0)

    pl.run_scoped(body, pltpu.VMEM((WEIGHT_DMA_DEPTH, chunk_rows, cols), jnp.float32),
                  pltpu.SemaphoreType.DMA((WEIGHT_DMA_DEPTH,)))


def _ada_kernel(c_ref, w_ref, b_ref, o_ref):
    @pl.when(pl.program_id(0) == 0)
    def _():
        o_ref[...] = jnp.broadcast_to(b_ref[...], o_ref.shape)

    c = c_ref[...]
    act = c * jax.nn.sigmoid(c)
    o_ref[...] += jnp.dot(act.astype(jnp.bfloat16), w_ref[...].astype(jnp.bfloat16),
                          preferred_element_type=jnp.float32)


def _modulation(c, w_ada, b_ada):
    b, d = c.shape
    n = w_ada.shape[1]
    rows = SUBLANES
    c_pad = jnp.zeros((rows, d), c.dtype).at[:b].set(c)
    tk = ADA_K_BLOCK
    out = pl.pallas_call(
        _ada_kernel,
        grid=(d // tk,),
        in_specs=[
            pl.BlockSpec((rows, tk), lambda j: (0, j)),
            pl.BlockSpec((tk, n), lambda j: (j, 0)),
            pl.BlockSpec((1, n), lambda j: (0, 0)),
        ],
        out_specs=pl.BlockSpec((rows, n), lambda j: (0, 0)),
        out_shape=jax.ShapeDtypeStruct((rows, n), jnp.float32),
        compiler_params=pltpu.CompilerParams(dimension_semantics=("arbitrary",)),
        name="ada_modulation",
    )(c_pad, w_ada, b_ada.reshape(1, n))
    return out[:b]


def _group_mean_sq(v):
    sq = v * v
    lower = lax.broadcasted_iota(jnp.int32, (1, 2 * HEAD_DIM), 1) < HEAD_DIM
    sums = []
    for c in range(v.shape[1] // (2 * HEAD_DIM)):
        pair = sq[:, c * 2 * HEAD_DIM:(c + 1) * 2 * HEAD_DIM]
        both = jnp.sum(pair, axis=-1, keepdims=True)
        first = jnp.sum(jnp.where(lower, pair, 0.0), axis=-1, keepdims=True)
        sums.append(jnp.where(lower, first, both - first))
    return jnp.concatenate(sums, axis=1) * (1.0 / HEAD_DIM)


def _mixer_kernel(sink_ref,
                  x_ref, xp_ref, xn_ref,
                  mod_ref, modn_ref,
                  g1n_ref, win_hbm, convw_ref, qg_ref, kg_ref, wo_hbm,
                  o_ref,
                  win_ref,
                  wo_ref,
                  wvt_ref,
                  h_ref,
                  hc_ref,
                  k2_ref,
                  vt_ref,
                  kc_ref,
                  vc_ref,
                  qn_ref,
                  ya_ref,
                  sgb_ref,
                  ybt_ref,
                  mrg_ref,
                  bias_ref):
    t = x_ref.shape[0]
    i = pl.program_id(1)
    n_tiles = pl.num_programs(1)
    first = jnp.logical_and(pl.program_id(0) == 0, i == 0)
    shift1 = mod_ref[0:1, :]
    scale1 = mod_ref[1:2, :]
    gate1 = mod_ref[2:3, :]
    gain = g1n_ref[...]
    lower = lax.broadcasted_iota(jnp.int32, (1, 2 * HEAD_DIM), 1) < HEAD_DIM

    def project_kv(h_rows):
        k = jnp.dot(h_rows, win_ref[:, OFF_K:OFF_K + KV_W], preferred_element_type=jnp.float32)
        vt = lax.dot_general(wvt_ref[...], h_rows, (((1,), (1,)), ((), ())),
                             preferred_element_type=jnp.float32)
        return k, vt

    def store_kv(k, vt, k_dst, v_dst, r0):
        rows = k.shape[0]
        kn = k * lax.rsqrt(_group_mean_sq(k) + EPS) * kg_ref[...]
        for pair in range(N_KV_HEADS // 2):
            tile = kn[:, pair * 2 * HEAD_DIM:(pair + 1) * 2 * HEAD_DIM]
            swapped = pltpu.roll(tile, HEAD_DIM, 1)
            k_dst[2 * pair, r0:r0 + rows, :] = jnp.where(lower, tile, swapped).astype(jnp.bfloat16)
            k_dst[2 * pair + 1, r0:r0 + rows, :] = jnp.where(lower, swapped, tile).astype(jnp.bfloat16)
        for g in range(N_KV_HEADS):
            v_dst[g, 0:HEAD_DIM, r0:r0 + rows] = (
                vt[g * HEAD_DIM:(g + 1) * HEAD_DIM, :].astype(jnp.bfloat16))

    @pl.when(first)
    def _():
        _stage_weights(win_hbm, win_ref)
        _stage_weights(wo_hbm, wo_ref)
        wvt_ref[...] = win_ref[:, OFF_V:OFF_V + KV_W].astype(jnp.float32).T.astype(jnp.bfloat16)
        kj = lax.broadcasted_iota(jnp.int32, (BLOCK, BLOCK), 0)
        qi = lax.broadcasted_iota(jnp.int32, (BLOCK, BLOCK), 1)
        for blk in range(3):
            dist = jnp.abs(qi + BLOCK - (kj + blk * BLOCK))
            distf = dist.astype(jnp.float32)
            for h in range(N_Q_HEADS):
                bias_ref[3 * h + blk] = jnp.where(dist <= WINDOW,
                                                  -(ALIBI_SLOPES[h] * LOG2_E) * distf, NEG_INF)
        bias_ref[3 * N_Q_HEADS] = jnp.full((BLOCK, BLOCK), NEG_INF, jnp.float32)
        def ones_rows(lanes):
            rows = lax.broadcasted_iota(jnp.int32, (VT_ROWS - HEAD_DIM, lanes), 0)
            return jnp.where(rows == 0, 1.0, 0.0).astype(jnp.bfloat16)

        for g in range(N_KV_HEADS):
            vt_ref[g, HEAD_DIM:, :] = ones_rows(t + 2 * BLOCK)
            vc_ref[g, HEAD_DIM:, :] = ones_rows(2 * BLOCK)
        h0 = _rms_mod(x_ref[0:BLOCK, :], gain, scale1, shift1).astype(jnp.bfloat16)
        store_kv(*project_kv(h0), kc_ref, vc_ref, BLOCK)
        zero_k = jnp.zeros((BLOCK, 2 * HEAD_DIM), jnp.bfloat16)
        zero_v = jnp.zeros((HEAD_DIM, BLOCK), jnp.bfloat16)
        for g in range(N_KV_HEADS):
            kc_ref[g, 0:BLOCK, :] = zero_k
            vc_ref[g, 0:HEAD_DIM, 0:BLOCK] = zero_v

    h_tile = _rms_mod(x_ref[...], gain, scale1, shift1)
    h_ref[0:t, :] = h_tile.astype(jnp.bfloat16)
    ht = h_ref[0:t, :]
    q = jnp.dot(ht, win_ref[:, OFF_Q:OFF_Q + D_MODEL], preferred_element_type=jnp.float32)

    h_after = _rms_mod(xn_ref[...], g1n_ref[...], modn_ref[1:2, :], modn_ref[0:1, :])
    h_ref[t:, :] = h_after.astype(jnp.bfloat16)
    h_before = jnp.where(i > 0, _rms_mod(xp_ref[...], gain, scale1, shift1), 0.0)
    hc_ref[...] = jnp.concatenate(
        [h_before, h_tile, jnp.where(i < n_tiles - 1, h_after[:SUBLANES], 0.0)],
        axis=0).astype(jnp.bfloat16)

    for g in range(N_KV_HEADS):
        k2_ref[g, 0:2 * BLOCK, :] = kc_ref[g]
        vt_ref[g, :, 0:2 * BLOCK] = vc_ref[g]
    k, vt = project_kv(h_ref[BLOCK:, :])

    store_kv(k, vt, k2_ref, vt_ref, 2 * BLOCK)

    qscale = qg_ref[...] * (LOG2_E / math.sqrt(HEAD_DIM))
    for c in range(D_MODEL // KV_W):
        qc = q[:, c * KV_W:(c + 1) * KV_W]
        qn_ref[:, c * KV_W:(c + 1) * KV_W] = (
            qc * lax.rsqrt(_group_mean_sq(qc) + EPS) * qscale[:, c * KV_W:(c + 1) * KV_W]
        ).astype(jnp.bfloat16)

    def gate_a_block(c):
        cs = slice(c * SIDE_COLS, (c + 1) * SIDE_COLS)
        cb = jnp.dot(ht, win_ref[:, OFF_CB + c * SIDE_COLS:OFF_CB + (c + 1) * SIDE_COLS],
                     preferred_element_type=jnp.float32)
        ga = jnp.dot(ht, win_ref[:, OFF_GA + c * SIDE_COLS:OFF_GA + (c + 1) * SIDE_COLS],
                     preferred_element_type=jnp.float32)
        ya_ref[:, cs] = jax.nn.sigmoid(ga) * cb

    def conv_block(c):
        cs = slice(c * SIDE_COLS, (c + 1) * SIDE_COLS)
        hc = hc_ref[...]
        cc = jnp.dot(hc, win_ref[:, OFF_CC + c * SIDE_COLS:OFF_CC + (c + 1) * SIDE_COLS],
                     preferred_element_type=jnp.float32)
        cu = jnp.dot(hc, win_ref[:, OFF_CU + c * SIDE_COLS:OFF_CU + (c + 1) * SIDE_COLS],
                     preferred_element_type=jnp.float32)
        ya_ref[:, cs] = ya_ref[:, cs] * _dwconv3(cc * cu, convw_ref[:, cs])

    def gate_b_block(c):
        cs = slice(c * SIDE_COLS, (c + 1) * SIDE_COLS)
        gb = jnp.dot(ht, win_ref[:, OFF_GB + c * SIDE_COLS:OFF_GB + (c + 1) * SIDE_COLS],
                     preferred_element_type=jnp.float32)
        sgb_ref[:, cs] = jax.nn.sigmoid(gb)

    side_jobs = []
    for c in range(D_MODEL // SIDE_COLS):
        side_jobs.append(lambda c=c: gate_a_block(c))
        side_jobs.append(lambda c=c: conv_block(c))
        side_jobs.append(lambda c=c: gate_b_block(c))

    qlower = lax.broadcasted_iota(jnp.int32, (BLOCK, 2 * HEAD_DIM), 1) < HEAD_DIM
    zero_q = jnp.zeros((BLOCK, 2 * HEAD_DIM), jnp.bfloat16)
    mask_blk = 3 * N_Q_HEADS

    def scores_t(qb, g):
        r0 = qb * BLOCK
        kb = k2_ref[g, r0:r0 + 3 * BLOCK, :]
        qs = []
        for half in range(GROUP // 2):
            c0 = (g * GROUP + 2 * half) * HEAD_DIM
            pair = qn_ref[r0:r0 + BLOCK, c0:c0 + 2 * HEAD_DIM]
            qs.append(jnp.where(qlower, pair, zero_q))
            qs.append(jnp.where(qlower, zero_q, pair))
        q4 = jnp.concatenate(qs, axis=0)
        return lax.dot_general(kb, q4, (((1,), (1,)), ((), ())),
                               preferred_element_type=jnp.float32)

    def attend(qb, g, st):
        r0 = qb * BLOCK
        ps = []
        sink_terms = []
        for j in range(GROUP):
            hh = g * GROUP + j
            idx = [3 * hh, 3 * hh + 1, 3 * hh + 2]
            if qb == 0:
                idx[0] = jnp.where(i == 0, mask_blk, idx[0])
            if qb == t // BLOCK - 1:
                idx[2] = jnp.where(i == n_tiles - 1, mask_blk, idx[2])
            s = jnp.concatenate(
                [st[blk * BLOCK:(blk + 1) * BLOCK, j * BLOCK:(j + 1) * BLOCK] + bias_ref[idx[blk]]
                 for blk in range(3)], axis=0)
            sink = sink_ref[hh] * LOG2_E
            m = jnp.maximum(jnp.max(s, axis=0, keepdims=True), sink)
            ps.append(jnp.exp2(s - m).astype(jnp.bfloat16))
            sink_terms.append(jnp.exp2(sink - m))
        p4 = jnp.concatenate(ps, axis=1)
        ot = jnp.dot(vt_ref[g, :, r0:r0 + 3 * BLOCK], p4,
                     preferred_element_type=jnp.float32)
        den = ot[HEAD_DIM:HEAD_DIM + 1, :] + jnp.concatenate(sink_terms, axis=1)
        y = ot[0:HEAD_DIM, :] / den
        for j in range(GROUP):
            hh = g * GROUP + j
            ybt_ref[hh * HEAD_DIM:(hh + 1) * HEAD_DIM, r0:r0 + BLOCK] = y[:, j * BLOCK:(j + 1) * BLOCK]

    stages = [(qb, g) for qb in range(t // BLOCK) for g in range(N_KV_HEADS)]
    st_next = scores_t(*stages[0])
    for n, (qb, g) in enumerate(stages):
        st = st_next
        if n + 1 < len(stages):
            st_next = scores_t(*stages[n + 1])
        if n % 4 != 3 and side_jobs:
            side_jobs.pop(0)()
        attend(qb, g, st)
    for job in side_jobs:
        job()

    for g in range(N_KV_HEADS):
        kc_ref[g] = k2_ref[g, t:t + 2 * BLOCK, :]
        vc_ref[g] = vt_ref[g, :, t:t + 2 * BLOCK]

    yb = ybt_ref[...].T
    mrg_ref[...] = (ya_ref[...] + sgb_ref[...] * yb).astype(jnp.bfloat16)
    out = jnp.dot(mrg_ref[...], wo_ref[...], preferred_element_type=jnp.float32)
    o_ref[...] = x_ref[...] + gate1 * out


def _token_mixer(x, mod, norm_gain, w_in, conv_w, q_gain, k_gain, sink, w_o):
    b, s, d = x.shape
    t = SEQ_TILE
    assert s % t == 0 and t % BLOCK == 0
    n_tiles = s // t
    bpt = t // BLOCK
    rpt = t // SUBLANES
    n_blocks = s // BLOCK

    def after(bi, i):
        blk = jnp.minimum(bi * n_blocks + (i + 1) * bpt, b * n_blocks - 1)
        return blk // n_blocks, blk % n_blocks

    const = lambda shape: pl.BlockSpec(shape, lambda bi, i, *_: (0,) * len(shape),
                                       pipeline_mode=pl.Buffered(1))
    grid_spec = pltpu.PrefetchScalarGridSpec(
        num_scalar_prefetch=1,
        grid=(b, n_tiles),
        in_specs=[
            pl.BlockSpec((None, t, d), lambda bi, i, *_: (bi, i, 0)),
            pl.BlockSpec((None, SUBLANES, d), lambda bi, i, *_: (bi, jnp.maximum(i * rpt - 1, 0), 0)),
            pl.BlockSpec((None, BLOCK, d), lambda bi, i, *_: (*after(bi, i), 0)),
            pl.BlockSpec((None, 6, d), lambda bi, i, *_: (bi, 0, 0)),
            pl.BlockSpec((None, 6, d), lambda bi, i, *_: (after(bi, i)[0], 0, 0)),
            const((1, d)),
            pl.BlockSpec(memory_space=pl.ANY),
            const((3, D_CONV)),
            const((1, d)),
            const((1, KV_W)),
            pl.BlockSpec(memory_space=pl.ANY),
        ],
        out_specs=pl.BlockSpec((None, t, d), lambda bi, i, *_: (bi, i, 0)),
        scratch_shapes=[
            pltpu.VMEM((d, D_IN), jnp.bfloat16),
            pltpu.VMEM((d, d), jnp.bfloat16),
            pltpu.VMEM((KV_W, d), jnp.bfloat16),
            pltpu.VMEM((t + BLOCK, d), jnp.bfloat16),
            pltpu.VMEM((t + 2 * SUBLANES, d), jnp.bfloat16),
            pltpu.VMEM((N_KV_HEADS, t + 2 * BLOCK, 2 * HEAD_DIM), jnp.bfloat16),
            pltpu.VMEM((N_KV_HEADS, VT_ROWS, t + 2 * BLOCK), jnp.bfloat16),
            pltpu.VMEM((N_KV_HEADS, 2 * BLOCK, 2 * HEAD_DIM), jnp.bfloat16),
            pltpu.VMEM((N_KV_HEADS, VT_ROWS, 2 * BLOCK), jnp.bfloat16),
            pltpu.VMEM((t, d), jnp.bfloat16),
            pltpu.VMEM((t, d), jnp.float32),
            pltpu.VMEM((t, d), jnp.float32),
            pltpu.VMEM((d, t), jnp.float32),
            pltpu.VMEM((t, d), jnp.bfloat16),
            pltpu.VMEM((3 * N_Q_HEADS + 1, BLOCK, BLOCK), jnp.float32),
        ],
    )
    return pl.pallas_call(
        _mixer_kernel,
        grid_spec=grid_spec,
        out_shape=jax.ShapeDtypeStruct((b, s, d), jnp.float32),
        compiler_params=pltpu.CompilerParams(
            dimension_semantics=("arbitrary", "arbitrary"),
            vmem_limit_bytes=VMEM_LIMIT_BYTES),
        name="token_mixer",
    )(sink, x, x, x, mod, mod, norm_gain.reshape(1, d), w_in, conv_w,
      jnp.tile(q_gain, N_Q_HEADS).reshape(1, d), jnp.tile(k_gain, N_KV_HEADS).reshape(1, KV_W), w_o)


def _ffn_kernel(x_ref, xp_ref, xn_ref,
                mod_ref, g2n_ref, wup_hbm, fcw_ref, fcb_ref, wdn_hbm,
                o_ref,
                wup_ref,
                wdn_ref,
                ht_ref,
                he_ref,
                act_ref):
    t = x_ref.shape[0]
    i = pl.program_id(1)
    n_tiles = pl.num_programs(1)

    @pl.when(jnp.logical_and(pl.program_id(0) == 0, i == 0))
    def _():
        _stage_weights(wup_hbm, wup_ref)
        _stage_weights(wdn_hbm, wdn_ref)

    shift2 = mod_ref[3:4, :]
    scale2 = mod_ref[4:5, :]
    gate2 = mod_ref[5:6, :]
    gain = g2n_ref[...]

    h_prev = _rms_mod(xp_ref[...], gain, scale2, shift2)
    h_next = _rms_mod(xn_ref[...], gain, scale2, shift2)
    h_prev = jnp.where(i > 0, h_prev, 0.0)
    h_next = jnp.where(i < n_tiles - 1, h_next, 0.0)
    h_tile = _rms_mod(x_ref[...], gain, scale2, shift2)
    ht_ref[...] = h_tile.astype(jnp.bfloat16)
    he_ref[...] = jnp.concatenate([h_prev, h_tile, h_next], axis=0).astype(jnp.bfloat16)

    for c in range(D_FF // FF_CHUNK):
        c0 = c * FF_CHUNK
        a_ext = jnp.dot(he_ref[...], wup_ref[:, c0:c0 + FF_CHUNK],
                        preferred_element_type=jnp.float32)
        gbr = jnp.dot(ht_ref[...], wup_ref[:, D_FF + c0:D_FF + c0 + FF_CHUNK],
                      preferred_element_type=jnp.float32)
        a = _dwconv3(a_ext, fcw_ref[:, c0:c0 + FF_CHUNK]) + fcb_ref[:, c0:c0 + FF_CHUNK]
        gelu = 0.5 * a * (1.0 + jnp.tanh(math.sqrt(2.0 / math.pi) * (a + 0.044715 * (a * a * a))))
        act_ref[:, c0:c0 + FF_CHUNK] = (gelu * gbr).astype(jnp.bfloat16)

    y = jnp.dot(act_ref[...], wdn_ref[...], preferred_element_type=jnp.float32)
    o_ref[...] = x_ref[...] + gate2 * y


def _conv_ffn(x, mod, norm_gain, w_up, ffn_conv_w, ffn_conv_b, w_down):
    b, s, d = x.shape
    t = SEQ_TILE
    assert s % t == 0 and D_FF % FF_CHUNK == 0
    n_tiles = s // t
    rpt = t // SUBLANES
    n_groups = s // SUBLANES

    const = lambda shape: pl.BlockSpec(shape, lambda bi, i: (0,) * len(shape),
                                       pipeline_mode=pl.Buffered(1))
    return pl.pallas_call(
        _ffn_kernel,
        grid=(b, n_tiles),
        in_specs=[
            pl.BlockSpec((None, t, d), lambda bi, i: (bi, i, 0)),
            pl.BlockSpec((None, SUBLANES, d), lambda bi, i: (bi, jnp.maximum(i * rpt - 1, 0), 0)),
            pl.BlockSpec((None, SUBLANES, d),
                         lambda bi, i: (bi, jnp.minimum((i + 1) * rpt, n_groups - 1), 0)),
            pl.BlockSpec((None, 6, d), lambda bi, i: (bi, 0, 0)),
            const((1, d)),
            pl.BlockSpec(memory_space=pl.ANY),
            const((3, D_FF)),
            const((1, D_FF)),
            pl.BlockSpec(memory_space=pl.ANY),
        ],
        out_specs=pl.BlockSpec((None, t, d), lambda bi, i: (bi, i, 0)),
        out_shape=jax.ShapeDtypeStruct((b, s, d), jnp.float32),
        scratch_shapes=[
            pltpu.VMEM((d, 2 * D_FF), jnp.bfloat16),
            pltpu.VMEM((D_FF, d), jnp.bfloat16),
            pltpu.VMEM((t, d), jnp.bfloat16),
            pltpu.VMEM((t + 2 * SUBLANES, d), jnp.bfloat16),
            pltpu.VMEM((t, D_FF), jnp.bfloat16),
        ],
        compiler_params=pltpu.CompilerParams(
            dimension_semantics=("arbitrary", "arbitrary"),
            vmem_limit_bytes=VMEM_LIMIT_BYTES),
        name="conv_ffn",
    )(x, x, x, mod, norm_gain.reshape(1, d), w_up, ffn_conv_w, ffn_conv_b.reshape(1, D_FF), w_down)


def kernel(x, c, w_ada, b_ada, norm1_gain, w_in, conv_w, q_gain, k_gain, sink, w_o, norm2_gain,
           w_up, ffn_conv_w, ffn_conv_b, w_down):
    depth = w_ada.shape[0]
    b, s, d = x.shape
    for l in range(depth):
        mod = _modulation(c, w_ada[l], b_ada[l]).reshape(b, 6, d)
        x = _token_mixer(x, mod, norm1_gain[l], w_in[l], conv_w[l], q_gain[l], k_gain[l],
                         sink[l], w_o[l])
        x = _conv_ffn(x, mod, norm2_gain[l], w_up[l], ffn_conv_w[l], ffn_conv_b[l], w_down[l])
    return x
```

```python
import math

import jax
import jax.numpy as jnp
from jax import lax
from jax.experimental import pallas as pl
from jax.experimental.pallas import tpu as pltpu

D_MODEL = 1024
HEAD_DIM = 64
N_Q_HEADS = D_MODEL // HEAD_DIM
N_KV_HEADS = N_Q_HEADS // 4
GROUP = N_Q_HEADS // N_KV_HEADS
D_CONV = D_MODEL
WINDOW = 128
BLOCK = 128
D_FF = ((8 * D_MODEL // 3 + 127) // 128) * 128
EPS = 1e-6
NEG_INF = -1e30
LOG2_E = math.log2(math.e)

OFF_CB = 0
OFF_CC = OFF_CB + D_CONV
OFF_CU = OFF_CC + D_CONV
OFF_Q = OFF_CU + D_CONV
OFF_K = OFF_Q + N_Q_HEADS * HEAD_DIM
OFF_V = OFF_K + N_KV_HEADS * HEAD_DIM
OFF_GA = OFF_V + N_KV_HEADS * HEAD_DIM
OFF_GB = OFF_GA + D_MODEL
D_IN = OFF_GB + D_MODEL

KV_W = N_KV_HEADS * HEAD_DIM
SUBLANES = 8
BF16_ROWS = 16
VT_ROWS = HEAD_DIM + BF16_ROWS
SEQ_TILE = 512
FF_CHUNK = 256
SIDE_COLS = 256
WEIGHT_CHUNK_BYTES = 1024 * 1024
WEIGHT_DMA_DEPTH = 8
VMEM_LIMIT_BYTES = 56 * 1024 * 1024

ALIBI_SLOPES = [2.0 ** (-8.0 * (h + 1) / N_Q_HEADS) for h in range(N_Q_HEADS)]


def _rms_mod(xv, gain, scale, shift):
    ms = jnp.mean(xv * xv, axis=-1, keepdims=True)
    return xv * lax.rsqrt(ms + EPS) * (gain * (1.0 + scale)) + shift


def _dwconv3(v_ext, w):
    rows = v_ext.shape[0]
    t = rows - 2 * SUBLANES
    prev = pltpu.roll(v_ext, 1, 0)[SUBLANES:SUBLANES + t]
    nxt = pltpu.roll(v_ext, rows - 1, 0)[SUBLANES:SUBLANES + t]
    return w[0:1, :] * prev + w[1:2, :] * v_ext[SUBLANES:SUBLANES + t] + w[2:3, :] * nxt


def _stage_weights(src_hbm, dst_ref):
    rows, cols = src_hbm.shape
    chunk_rows = max(r for r in range(BF16_ROWS, rows + 1, BF16_ROWS)
                     if rows % r == 0 and r * cols * 4 <= WEIGHT_CHUNK_BYTES
                     and rows // r >= WEIGHT_DMA_DEPTH)
    n_chunks = rows // chunk_rows

    def body(stage_ref, sem_ref):
        def copy(c):
            slot = lax.rem(c, WEIGHT_DMA_DEPTH)
            r0 = pl.multiple_of(c * chunk_rows, chunk_rows)
            return pltpu.make_async_copy(src_hbm.at[pl.ds(r0, chunk_rows), :],
                                         stage_ref.at[slot], sem_ref.at[slot])

        for c in range(WEIGHT_DMA_DEPTH - 1):
            copy(c).start()

        def step(c, carry):
            @pl.when(c + WEIGHT_DMA_DEPTH - 1 < n_chunks)
            def _():
                copy(c + WEIGHT_DMA_DEPTH - 1).start()

            copy(c).wait()
            r0 = pl.multiple_of(c * chunk_rows, chunk_rows)
            dst_ref[pl.ds(r0, chunk_rows), :] = (
                stage_ref[lax.rem(c, WEIGHT_DMA_DEPTH)].astype(jnp.bfloat16))
            return carry

        lax.fori_loop(0, n_chunks, step, 0)

    pl.run_scoped(body, pltpu.VMEM((WEIGHT_DMA_DEPTH, chunk_rows, cols), jnp.float32),
                  pltpu.SemaphoreType.DMA((WEIGHT_DMA_DEPTH,)))


def _ada_kernel(c_ref, w_ref, b_ref, o_ref):
    c = c_ref[...]
    act = c * jax.nn.sigmoid(c)
    o_ref[...] = jnp.dot(act.astype(jnp.bfloat16), w_ref[...].astype(jnp.bfloat16),
                         preferred_element_type=jnp.float32) + b_ref[...]


def _modulation(c, w_ada, b_ada):
    b, d = c.shape
    n = w_ada.shape[1]
    rows = SUBLANES
    c_pad = jnp.zeros((rows, d), c.dtype).at[:b].set(c)
    tn = d
    out = pl.pallas_call(
        _ada_kernel,
        grid=(n // tn,),
        in_specs=[
            pl.BlockSpec((rows, d), lambda j: (0, 0)),
            pl.BlockSpec((d, tn), lambda j: (0, j)),
            pl.BlockSpec((1, tn), lambda j: (0, j)),
        ],
        out_specs=pl.BlockSpec((rows, tn), lambda j: (0, j)),
        out_shape=jax.ShapeDtypeStruct((rows, n), jnp.float32),
        compiler_params=pltpu.CompilerParams(dimension_semantics=("arbitrary",)),
        name="ada_modulation",
    )(c_pad, w_ada, b_ada.reshape(1, n))
    return out[:b]


def _group_mean_sq(v):
    sq = v * v
    lower = lax.broadcasted_iota(jnp.int32, (1, 2 * HEAD_DIM), 1) < HEAD_DIM
    sums = []
    for c in range(v.shape[1] // (2 * HEAD_DIM)):
        pair = sq[:, c * 2 * HEAD_DIM:(c + 1) * 2 * HEAD_DIM]
        both = jnp.sum(pair, axis=-1, keepdims=True)
        first = jnp.sum(jnp.where(lower, pair, 0.0), axis=-1, keepdims=True)
        sums.append(jnp.where(lower, first, both - first))
    return jnp.concatenate(sums, axis=1) * (1.0 / HEAD_DIM)


def _mixer_kernel(sink_ref,
                  x_ref, xp_ref, xn_ref,
                  mod_ref, modn_ref,
                  g1n_ref, win_hbm, convw_ref, qg_ref, kg_ref, wo_hbm,
                  o_ref,
                  win_ref,
                  wo_ref,
                  wvt_ref,
                  h_ref,
                  hc_ref,
                  k2_ref,
                  vt_ref,
                  kc_ref,
                  vc_ref,
                  qn_ref,
                  ya_ref,
                  sgb_ref,
                  ybt_ref,
                  mrg_ref,
                  bias_ref):
    t = x_ref.shape[0]
    i = pl.program_id(1)
    n_tiles = pl.num_programs(1)
    first = jnp.logical_and(pl.program_id(0) == 0, i == 0)
    shift1 = mod_ref[0:1, :]
    scale1 = mod_ref[1:2, :]
    gate1 = mod_ref[2:3, :]
    gain = g1n_ref[...]
    lower = lax.broadcasted_iota(jnp.int32, (1, 2 * HEAD_DIM), 1) < HEAD_DIM

    def project_kv(h_rows):
        k = jnp.dot(h_rows, win_ref[:, OFF_K:OFF_K + KV_W], preferred_element_type=jnp.float32)
        vt = lax.dot_general(wvt_ref[...], h_rows, (((1,), (1,)), ((), ())),
                             preferred_element_type=jnp.float32)
        return k, vt

    def store_kv(k, vt, k_dst, v_dst, r0):
        rows = k.shape[0]
        kn = k * lax.rsqrt(_group_mean_sq(k) + EPS) * kg_ref[...]
        for pair in range(N_KV_HEADS // 2):
            tile = kn[:, pair * 2 * HEAD_DIM:(pair + 1) * 2 * HEAD_DIM]
            swapped = pltpu.roll(tile, HEAD_DIM, 1)
            k_dst[2 * pair, r0:r0 + rows, :] = jnp.where(lower, tile, swapped).astype(jnp.bfloat16)
            k_dst[2 * pair + 1, r0:r0 + rows, :] = jnp.where(lower, swapped, tile).astype(jnp.bfloat16)
        for g in range(N_KV_HEADS):
            v_dst[g, 0:HEAD_DIM, r0:r0 + rows] = (
                vt[g * HEAD_DIM:(g + 1) * HEAD_DIM, :].astype(jnp.bfloat16))

    @pl.when(first)
    def _():
        _stage_weights(win_hbm, win_ref)
        _stage_weights(wo_hbm, wo_ref)
        wvt_ref[...] = win_ref[:, OFF_V:OFF_V + KV_W].astype(jnp.float32).T.astype(jnp.bfloat16)
        kj = lax.broadcasted_iota(jnp.int32, (BLOCK, BLOCK), 0)
        qi = lax.broadcasted_iota(jnp.int32, (BLOCK, BLOCK), 1)
        for blk in range(3):
            dist = jnp.abs(qi + BLOCK - (kj + blk * BLOCK))
            distf = dist.astype(jnp.float32)
            for h in range(N_Q_HEADS):
                bias_ref[3 * h + blk] = jnp.where(dist <= WINDOW,
                                                  -(ALIBI_SLOPES[h] * LOG2_E) * distf, NEG_INF)
        bias_ref[3 * N_Q_HEADS] = jnp.full((BLOCK, BLOCK), NEG_INF, jnp.float32)
        def ones_rows(lanes):
            rows = lax.broadcasted_iota(jnp.int32, (VT_ROWS - HEAD_DIM, lanes), 0)
            return jnp.where(rows == 0, 1.0, 0.0).astype(jnp.bfloat16)

        for g in range(N_KV_HEADS):
            vt_ref[g, HEAD_DIM:, :] = ones_rows(t + 2 * BLOCK)
            vc_ref[g, HEAD_DIM:, :] = ones_rows(2 * BLOCK)
        h0 = _rms_mod(x_ref[0:BLOCK, :], gain, scale1, shift1).astype(jnp.bfloat16)
        store_kv(*project_kv(h0), kc_ref, vc_ref, BLOCK)
        zero_k = jnp.zeros((BLOCK, 2 * HEAD_DIM), jnp.bfloat16)
        zero_v = jnp.zeros((HEAD_DIM, BLOCK), jnp.bfloat16)
        for g in range(N_KV_HEADS):
            kc_ref[g, 0:BLOCK, :] = zero_k
            vc_ref[g, 0:HEAD_DIM, 0:BLOCK] = zero_v

    h_tile = _rms_mod(x_ref[...], gain, scale1, shift1)
    h_ref[0:t, :] = h_tile.astype(jnp.bfloat16)
    ht = h_ref[0:t, :]
    q = jnp.dot(ht, win_ref[:, OFF_Q:OFF_Q + D_MODEL], preferred_element_type=jnp.float32)

    h_after = _rms_mod(xn_ref[...], g1n_ref[...], modn_ref[1:2, :], modn_ref[0:1, :])
    h_ref[t:, :] = h_after.astype(jnp.bfloat16)
    h_before = jnp.where(i > 0, _rms_mod(xp_ref[...], gain, scale1, shift1), 0.0)
    hc_ref[...] = jnp.concatenate(
        [h_before, h_tile, jnp.where(i < n_tiles - 1, h_after[:SUBLANES], 0.0)],
        axis=0).astype(jnp.bfloat16)

    for g in range(N_KV_HEADS):
        k2_ref[g, 0:2 * BLOCK, :] = kc_ref[g]
        vt_ref[g, :, 0:2 * BLOCK] = vc_ref[g]
    k, vt = project_kv(h_ref[BLOCK:, :])

    store_kv(k, vt, k2_ref, vt_ref, 2 * BLOCK)

    qscale = qg_ref[...] * (LOG2_E / math.sqrt(HEAD_DIM))
    for c in range(D_MODEL // KV_W):
        qc = q[:, c * KV_W:(c + 1) * KV_W]
        qn_ref[:, c * KV_W:(c + 1) * KV_W] = (
            qc * lax.rsqrt(_group_mean_sq(qc) + EPS) * qscale[:, c * KV_W:(c + 1) * KV_W]
        ).astype(jnp.bfloat16)

    def gate_a_block(c):
        cs = slice(c * SIDE_COLS, (c + 1) * SIDE_COLS)
        cb = jnp.dot(ht, win_ref[:, OFF_CB + c * SIDE_COLS:OFF_CB + (c + 1) * SIDE_COLS],
                     preferred_element_type=jnp.float32)
        ga = jnp.dot(ht, win_ref[:, OFF_GA + c * SIDE_COLS:OFF_GA + (c + 1) * SIDE_COLS],
                     preferred_element_type=jnp.float32)
        ya_ref[:, cs] = jax.nn.sigmoid(ga) * cb

    def conv_block(c):
        cs = slice(c * SIDE_COLS, (c + 1) * SIDE_COLS)
        hc = hc_ref[...]
        cc = jnp.dot(hc, win_ref[:, OFF_CC + c * SIDE_COLS:OFF_CC + (c + 1) * SIDE_COLS],
                     preferred_element_type=jnp.float32)
        cu = jnp.dot(hc, win_ref[:, OFF_CU + c * SIDE_COLS:OFF_CU + (c + 1) * SIDE_COLS],
                     preferred_element_type=jnp.float32)
        ya_ref[:, cs] = ya_ref[:, cs] * _dwconv3(cc * cu, convw_ref[:, cs])

    def gate_b_block(c):
        cs = slice(c * SIDE_COLS, (c + 1) * SIDE_COLS)
        gb = jnp.dot(ht, win_ref[:, OFF_GB + c * SIDE_COLS:OFF_GB + (c + 1) * SIDE_COLS],
                     preferred_element_type=jnp.float32)
        sgb_ref[:, cs] = jax.nn.sigmoid(gb)

    side_jobs = []
    for c in range(D_MODEL // SIDE_COLS):
        side_jobs.append(lambda c=c: gate_a_block(c))
        side_jobs.append(lambda c=c: conv_block(c))
        side_jobs.append(lambda c=c: gate_b_block(c))

    qlower = lax.broadcasted_iota(jnp.int32, (BLOCK, 2 * HEAD_DIM), 1) < HEAD_DIM
    zero_q = jnp.zeros((BLOCK, 2 * HEAD_DIM), jnp.bfloat16)
    mask_blk = 3 * N_Q_HEADS

    def scores_t(qb, g):
        r0 = qb * BLOCK
        kb = k2_ref[g, r0:r0 + 3 * BLOCK, :]
        qs = []
        for half in range(GROUP // 2):
            c0 = (g * GROUP + 2 * half) * HEAD_DIM
            pair = qn_ref[r0:r0 + BLOCK, c0:c0 + 2 * HEAD_DIM]
            qs.append(jnp.where(qlower, pair, zero_q))
            qs.append(jnp.where(qlower, zero_q, pair))
        q4 = jnp.concatenate(qs, axis=0)
        return lax.dot_general(kb, q4, (((1,), (1,)), ((), ())),
                               preferred_element_type=jnp.float32)

    def attend(qb, g, st):
        r0 = qb * BLOCK
        ps = []
        sink_terms = []
        for j in range(GROUP):
            hh = g * GROUP + j
            idx = [3 * hh, 3 * hh + 1, 3 * hh + 2]
            if qb == 0:
                idx[0] = jnp.where(i == 0, mask_blk, idx[0])
            if qb == t // BLOCK - 1:
                idx[2] = jnp.where(i == n_tiles - 1, mask_blk, idx[2])
            s = jnp.concatenate(
                [st[blk * BLOCK:(blk + 1) * BLOCK, j * BLOCK:(j + 1) * BLOCK] + bias_ref[idx[blk]]
                 for blk in range(3)], axis=0)
            sink = sink_ref[hh] * LOG2_E
            m = jnp.maximum(jnp.max(s, axis=0, keepdims=True), sink)
            ps.append(jnp.exp2(s - m).astype(jnp.bfloat16))
            sink_terms.append(jnp.exp2(sink - m))
        p4 = jnp.concatenate(ps, axis=1)
        ot = jnp.dot(vt_ref[g, :, r0:r0 + 3 * BLOCK], p4,
                     preferred_element_type=jnp.float32)
        den = ot[HEAD_DIM:HEAD_DIM + 1, :] + jnp.concatenate(sink_terms, axis=1)
        y = ot[0:HEAD_DIM, :] / den
        for j in range(GROUP):
            hh = g * GROUP + j
            ybt_ref[hh * HEAD_DIM:(hh + 1) * HEAD_DIM, r0:r0 + BLOCK] = y[:, j * BLOCK:(j + 1) * BLOCK]

    stages = [(qb, g) for qb in range(t // BLOCK) for g in range(N_KV_HEADS)]
    st_next = scores_t(*stages[0])
    for n, (qb, g) in enumerate(stages):
        st = st_next
        if n + 1 < len(stages):
            st_next = scores_t(*stages[n + 1])
        if n % 4 != 3 and side_jobs:
            side_jobs.pop(0)()
        attend(qb, g, st)
    for job in side_jobs:
        job()

    for g in range(N_KV_HEADS):
        kc_ref[g] = k2_ref[g, t:t + 2 * BLOCK, :]
        vc_ref[g] = vt_ref[g, :, t:t + 2 * BLOCK]

    yb = ybt_ref[...].T
    mrg_ref[...] = (ya_ref[...] + sgb_ref[...] * yb).astype(jnp.bfloat16)
    out = jnp.dot(mrg_ref[...], wo_ref[...], preferred_element_type=jnp.float32)
    o_ref[...] = x_ref[...] + gate1 * out


def _token_mixer(x, mod, norm_gain, w_in, conv_w, q_gain, k_gain, sink, w_o):
    b, s, d = x.shape
    t = SEQ_TILE
    assert s % t == 0 and t % BLOCK == 0
    n_tiles = s // t
    bpt = t // BLOCK
    rpt = t // SUBLANES
    n_blocks = s // BLOCK

    def after(bi, i):
        blk = jnp.minimum(bi * n_blocks + (i + 1) * bpt, b * n_blocks - 1)
        return blk // n_blocks, blk % n_blocks

    const = lambda shape: pl.BlockSpec(shape, lambda bi, i, *_: (0,) * len(shape),
                                       pipeline_mode=pl.Buffered(1))
    grid_spec = pltpu.PrefetchScalarGridSpec(
        num_scalar_prefetch=1,
        grid=(b, n_tiles),
        in_specs=[
            pl.BlockSpec((None, t, d), lambda bi, i, *_: (bi, i, 0)),
            pl.BlockSpec((None, SUBLANES, d), lambda bi, i, *_: (bi, jnp.maximum(i * rpt - 1, 0), 0)),
            pl.BlockSpec((None, BLOCK, d), lambda bi, i, *_: (*after(bi, i), 0)),
            pl.BlockSpec((None, 6, d), lambda bi, i, *_: (bi, 0, 0)),
            pl.BlockSpec((None, 6, d), lambda bi, i, *_: (after(bi, i)[0], 0, 0)),
            const((1, d)),
            pl.BlockSpec(memory_space=pl.ANY),
            const((3, D_CONV)),
            const((1, d)),
            const((1, KV_W)),
            pl.BlockSpec(memory_space=pl.ANY),
        ],
        out_specs=pl.BlockSpec((None, t, d), lambda bi, i, *_: (bi, i, 0)),
        scratch_shapes=[
            pltpu.VMEM((d, D_IN), jnp.bfloat16),
            pltpu.VMEM((d, d), jnp.bfloat16),
            pltpu.VMEM((KV_W, d), jnp.bfloat16),
            pltpu.VMEM((t + BLOCK, d), jnp.bfloat16),
            pltpu.VMEM((t + 2 * SUBLANES, d), jnp.bfloat16),
            pltpu.VMEM((N_KV_HEADS, t + 2 * BLOCK, 2 * HEAD_DIM), jnp.bfloat16),
            pltpu.VMEM((N_KV_HEADS, VT_ROWS, t + 2 * BLOCK), jnp.bfloat16),
            pltpu.VMEM((N_KV_HEADS, 2 * BLOCK, 2 * HEAD_DIM), jnp.bfloat16),
            pltpu.VMEM((N_KV_HEADS, VT_ROWS, 2 * BLOCK), jnp.bfloat16),
            pltpu.VMEM((t, d), jnp.bfloat16),
            pltpu.VMEM((t, d), jnp.float32),
            pltpu.VMEM((t, d), jnp.float32),
            pltpu.VMEM((d, t), jnp.float32),
            pltpu.VMEM((t, d), jnp.bfloat16),
            pltpu.VMEM((3 * N_Q_HEADS + 1, BLOCK, BLOCK), jnp.float32),
        ],
    )
    return pl.pallas_call(
        _mixer_kernel,
        grid_spec=grid_spec,
        out_shape=jax.ShapeDtypeStruct((b, s, d), jnp.float32),
        compiler_params=pltpu.CompilerParams(
            dimension_semantics=("arbitrary", "arbitrary"),
            vmem_limit_bytes=VMEM_LIMIT_BYTES),
        name="token_mixer",
    )(sink, x, x, x, mod, mod, norm_gain.reshape(1, d), w_in, conv_w,
      jnp.tile(q_gain, N_Q_HEADS).reshape(1, d), jnp.tile(k_gain, N_KV_HEADS).reshape(1, KV_W), w_o)


def _ffn_kernel(x_ref, xp_ref, xn_ref,
                mod_ref, g2n_ref, wup_hbm, fcw_ref, fcb_ref, wdn_hbm,
                o_ref,
                wup_ref,
                wdn_ref,
                ht_ref,
                he_ref,
                act_ref):
    t = x_ref.shape[0]
    i = pl.program_id(1)
    n_tiles = pl.num_programs(1)

    @pl.when(jnp.logical_and(pl.program_id(0) == 0, i == 0))
    def _():
        _stage_weights(wup_hbm, wup_ref)
        _stage_weights(wdn_hbm, wdn_ref)

    shift2 = mod_ref[3:4, :]
    scale2 = mod_ref[4:5, :]
    gate2 = mod_ref[5:6, :]
    gain = g2n_ref[...]

    h_prev = _rms_mod(xp_ref[...], gain, scale2, shift2)
    h_next = _rms_mod(xn_ref[...], gain, scale2, shift2)
    h_prev = jnp.where(i > 0, h_prev, 0.0)
    h_next = jnp.where(i < n_tiles - 1, h_next, 0.0)
    h_tile = _rms_mod(x_ref[...], gain, scale2, shift2)
    ht_ref[...] = h_tile.astype(jnp.bfloat16)
    he_ref[...] = jnp.concatenate([h_prev, h_tile, h_next], axis=0).astype(jnp.bfloat16)

    for c in range(D_FF // FF_CHUNK):
        c0 = c * FF_CHUNK
        a_ext = jnp.dot(he_ref[...], wup_ref[:, c0:c0 + FF_CHUNK],
                        preferred_element_type=jnp.float32)
        gbr = jnp.dot(ht_ref[...], wup_ref[:, D_FF + c0:D_FF + c0 + FF_CHUNK],
                      preferred_element_type=jnp.float32)
        a = _dwconv3(a_ext, fcw_ref[:, c0:c0 + FF_CHUNK]) + fcb_ref[:, c0:c0 + FF_CHUNK]
        gelu = 0.5 * a * (1.0 + jnp.tanh(math.sqrt(2.0 / math.pi) * (a + 0.044715 * (a * a * a))))
        act_ref[:, c0:c0 + FF_CHUNK] = (gelu * gbr).astype(jnp.bfloat16)

    y = jnp.dot(act_ref[...], wdn_ref[...], preferred_element_type=jnp.float32)
    o_ref[...] = x_ref[...] + gate2 * y


def _conv_ffn(x, mod, norm_gain, w_up, ffn_conv_w, ffn_conv_b, w_down):
    b, s, d = x.shape
    t = SEQ_TILE
    assert s % t == 0 and D_FF % FF_CHUNK == 0
    n_tiles = s // t
    rpt = t // SUBLANES
    n_groups = s // SUBLANES

    const = lambda shape: pl.BlockSpec(shape, lambda bi, i: (0,) * len(shape),
                                       pipeline_mode=pl.Buffered(1))
    return pl.pallas_call(
        _ffn_kernel,
        grid=(b, n_tiles),
        in_specs=[
            pl.BlockSpec((None, t, d), lambda bi, i: (bi, i, 0)),
            pl.BlockSpec((None, SUBLANES, d), lambda bi, i: (bi, jnp.maximum(i * rpt - 1, 0), 0)),
            pl.BlockSpec((None, SUBLANES, d),
                         lambda bi, i: (bi, jnp.minimum((i + 1) * rpt, n_groups - 1), 0)),
            pl.BlockSpec((None, 6, d), lambda bi, i: (bi, 0, 0)),
            const((1, d)),
            pl.BlockSpec(memory_space=pl.ANY),
            const((3, D_FF)),
            const((1, D_FF)),
            pl.BlockSpec(memory_space=pl.ANY),
        ],
        out_specs=pl.BlockSpec((None, t, d), lambda bi, i: (bi, i, 0)),
        out_shape=jax.ShapeDtypeStruct((b, s, d), jnp.float32),
        scratch_shapes=[
            pltpu.VMEM((d, 2 * D_FF), jnp.bfloat16),
            pltpu.VMEM((D_FF, d), jnp.bfloat16),
            pltpu.VMEM((t, d), jnp.bfloat16),
            pltpu.VMEM((t + 2 * SUBLANES, d), jnp.bfloat16),
            pltpu.VMEM((t, D_FF), jnp.bfloat16),
        ],
        compiler_params=pltpu.CompilerParams(
            dimension_semantics=("arbitrary", "arbitrary"),
            vmem_limit_bytes=VMEM_LIMIT_BYTES),
        name="conv_ffn",
    )(x, x, x, mod, norm_gain.reshape(1, d), w_up, ffn_conv_w, ffn_conv_b.reshape(1, D_FF), w_down)


def kernel(x, c, w_ada, b_ada, norm1_gain, w_in, conv_w, q_gain, k_gain, sink, w_o, norm2_gain,
           w_up, ffn_conv_w, ffn_conv_b, w_down):
    depth = w_ada.shape[0]
    b, s, d = x.shape
    for l in range(depth):
        mod = _modulation(c, w_ada[l], b_ada[l]).reshape(b, 6, d)
        x = _token_mixer(x, mod, norm1_gain[l], w_in[l], conv_w[l], q_gain[l], k_gain[l],
                         sink[l], w_o[l])
        x = _conv_ffn(x, mod, norm2_gain[l], w_up[l], ffn_conv_w[l], ffn_conv_b[l], w_down[l])
    return x
```

```python
import math

import jax
import jax.numpy as jnp
from jax import lax
from jax.experimental import pallas as pl
from jax.experimental.pallas import tpu as pltpu

D_MODEL = 1024
HEAD_DIM = 64
N_Q_HEADS = D_MODEL // HEAD_DIM
N_KV_HEADS = N_Q_HEADS // 4
GROUP = N_Q_HEADS // N_KV_HEADS
D_CONV = D_MODEL
WINDOW = 128
BLOCK = 128
D_FF = ((8 * D_MODEL // 3 + 127) // 128) * 128
EPS = 1e-6
NEG_INF = -1e30
LOG2_E = math.log2(math.e)

OFF_CB = 0
OFF_CC = OFF_CB + D_CONV
OFF_CU = OFF_CC + D_CONV
OFF_Q = OFF_CU + D_CONV
OFF_K = OFF_Q + N_Q_HEADS * HEAD_DIM
OFF_V = OFF_K + N_KV_HEADS * HEAD_DIM
OFF_GA = OFF_V + N_KV_HEADS * HEAD_DIM
OFF_GB = OFF_GA + D_MODEL
D_IN = OFF_GB + D_MODEL

KV_W = N_KV_HEADS * HEAD_DIM
SUBLANES = 8
BF16_ROWS = 16
VT_ROWS = HEAD_DIM + BF16_ROWS
SEQ_TILE = 512
FF_CHUNK = 256
SIDE_COLS = 256
WEIGHT_CHUNK_BYTES = 1024 * 1024
WEIGHT_DMA_DEPTH = 8
VMEM_LIMIT_BYTES = 56 * 1024 * 1024

ALIBI_SLOPES = [2.0 ** (-8.0 * (h + 1) / N_Q_HEADS) for h in range(N_Q_HEADS)]


def _rms_mod(xv, gain, scale, shift):
    ms = jnp.mean(xv * xv, axis=-1, keepdims=True)
    return xv * lax.rsqrt(ms + EPS) * (gain * (1.0 + scale)) + shift


def _dwconv3(v_ext, w):
    rows = v_ext.shape[0]
    t = rows - 2 * SUBLANES
    prev = pltpu.roll(v_ext, 1, 0)[SUBLANES:SUBLANES + t]
    nxt = pltpu.roll(v_ext, rows - 1, 0)[SUBLANES:SUBLANES + t]
    return w[0:1, :] * prev + w[1:2, :] * v_ext[SUBLANES:SUBLANES + t] + w[2:3, :] * nxt


def _stage_weights(src_hbm, dst_ref):
    rows, cols = src_hbm.shape
    chunk_rows = max(r for r in range(BF16_ROWS, rows + 1, BF16_ROWS)
                     if rows % r == 0 and r * cols * 4 <= WEIGHT_CHUNK_BYTES
                     and rows // r >= WEIGHT_DMA_DEPTH)
    n_chunks = rows // chunk_rows

    def body(stage_ref, sem_ref):
        def copy(c):
            slot = lax.rem(c, WEIGHT_DMA_DEPTH)
            r0 = pl.multiple_of(c * chunk_rows, chunk_rows)
            return pltpu.make_async_copy(src_hbm.at[pl.ds(r0, chunk_rows), :],
                                         stage_ref.at[slot], sem_ref.at[slot])

        for c in range(WEIGHT_DMA_DEPTH - 1):
            copy(c).start()

        def step(c, carry):
            @pl.when(c + WEIGHT_DMA_DEPTH - 1 < n_chunks)
            def _():
                copy(c + WEIGHT_DMA_DEPTH - 1).start()

            copy(c).wait()
            r0 = pl.multiple_of(c * chunk_rows, chunk_rows)
            dst_ref[pl.ds(r0, chunk_rows), :] = (
                stage_ref[lax.rem(c, WEIGHT_DMA_DEPTH)].astype(jnp.bfloat16))
            return carry

        lax.fori_loop(0, n_chunks, step, 0)

    pl.run_scoped(body, pltpu.VMEM((WEIGHT_DMA_DEPTH, chunk_rows, cols), jnp.float32),
                  pltpu.SemaphoreType.DMA((WEIGHT_DMA_DEPTH,)))


def _ada_kernel(c_ref, w_ref, b_ref, o_ref):
    c = c_ref[...]
    act = c * jax.nn.sigmoid(c)
    o_ref[...] = jnp.dot(act.astype(jnp.bfloat16), w_ref[...].astype(jnp.bfloat16),
                         preferred_element_type=jnp.float32) + b_ref[...]


def _modulation(c, w_ada, b_ada):
    b, d = c.shape
    n = w_ada.shape[1]
    rows = SUBLANES
    c_pad = jnp.zeros((rows, d), c.dtype).at[:b].set(c)
    tn = d
    out = pl.pallas_call(
        _ada_kernel,
        grid=(n // tn,),
        in_specs=[
            pl.BlockSpec((rows, d), lambda j: (0, 0)),
            pl.BlockSpec((d, tn), lambda j: (0, j)),
            pl.BlockSpec((1, tn), lambda j: (0, j)),
        ],
        out_specs=pl.BlockSpec((rows, tn), lambda j: (0, j)),
        out_shape=jax.ShapeDtypeStruct((rows, n), jnp.float32),
        compiler_params=pltpu.CompilerParams(dimension_semantics=("arbitrary",)),
        name="ada_modulation",
    )(c_pad, w_ada, b_ada.reshape(1, n))
    return out[:b]


def _group_mean_sq(v):
    sq = v * v
    lower = lax.broadcasted_iota(jnp.int32, (1, 2 * HEAD_DIM), 1) < HEAD_DIM
    sums = []
    for c in range(v.shape[1] // (2 * HEAD_DIM)):
        pair = sq[:, c * 2 * HEAD_DIM:(c + 1) * 2 * HEAD_DIM]
        first = jnp.sum(jnp.where(lower, pair, 0.0), axis=-1, keepdims=True)
        second = jnp.sum(jnp.where(lower, 0.0, pair), axis=-1, keepdims=True)
        sums.append(jnp.where(lower, first, second))
    return jnp.concatenate(sums, axis=1) * (1.0 / HEAD_DIM)


def _mixer_kernel(sink_ref,
                  x_ref, xp_ref, xn_ref,
                  mod_ref, modn_ref,
                  g1n_ref, win_hbm, convw_ref, qg_ref, kg_ref, wo_hbm,
                  o_ref,
                  win_ref,
                  wo_ref,
                  wvt_ref,
                  h_ref,
                  hc_ref,
                  k2_ref,
                  vt_ref,
                  kc_ref,
                  vc_ref,
                  qn_ref,
                  ya_ref,
                  sgb_ref,
                  ybt_ref,
                  mrg_ref,
                  bias_ref):
    t = x_ref.shape[0]
    i = pl.program_id(1)
    n_tiles = pl.num_programs(1)
    first = jnp.logical_and(pl.program_id(0) == 0, i == 0)
    shift1 = mod_ref[0:1, :]
    scale1 = mod_ref[1:2, :]
    gate1 = mod_ref[2:3, :]
    gain = g1n_ref[...]
    lower = lax.broadcasted_iota(jnp.int32, (1, 2 * HEAD_DIM), 1) < HEAD_DIM

    def project_kv(h_rows):
        k = jnp.dot(h_rows, win_ref[:, OFF_K:OFF_K + KV_W], preferred_element_type=jnp.float32)
        vt = lax.dot_general(wvt_ref[...], h_rows, (((1,), (1,)), ((), ())),
                             preferred_element_type=jnp.float32)
        return k, vt

    def store_kv(k, vt, k_dst, v_dst, r0):
        rows = k.shape[0]
        kn = k * lax.rsqrt(_group_mean_sq(k) + EPS) * kg_ref[...]
        for pair in range(N_KV_HEADS // 2):
            tile = kn[:, pair * 2 * HEAD_DIM:(pair + 1) * 2 * HEAD_DIM]
            swapped = pltpu.roll(tile, HEAD_DIM, 1)
            k_dst[2 * pair, r0:r0 + rows, :] = jnp.where(lower, tile, swapped).astype(jnp.bfloat16)
            k_dst[2 * pair + 1, r0:r0 + rows, :] = jnp.where(lower, swapped, tile).astype(jnp.bfloat16)
        for g in range(N_KV_HEADS):
            v_dst[g, 0:HEAD_DIM, r0:r0 + rows] = (
                vt[g * HEAD_DIM:(g + 1) * HEAD_DIM, :].astype(jnp.bfloat16))

    @pl.when(first)
    def _():
        _stage_weights(win_hbm, win_ref)
        _stage_weights(wo_hbm, wo_ref)
        wvt_ref[...] = win_ref[:, OFF_V:OFF_V + KV_W].astype(jnp.float32).T.astype(jnp.bfloat16)
        kj = lax.broadcasted_iota(jnp.int32, (BLOCK, BLOCK), 0)
        qi = lax.broadcasted_iota(jnp.int32, (BLOCK, BLOCK), 1)
        for blk in range(3):
            dist = jnp.abs(qi + BLOCK - (kj + blk * BLOCK))
            distf = dist.astype(jnp.float32)
            for h in range(N_Q_HEADS):
                bias_ref[3 * h + blk] = jnp.where(dist <= WINDOW,
                                                  -(ALIBI_SLOPES[h] * LOG2_E) * distf, NEG_INF)
        bias_ref[3 * N_Q_HEADS] = jnp.full((BLOCK, BLOCK), NEG_INF, jnp.float32)
        def ones_rows(lanes):
            rows = lax.broadcasted_iota(jnp.int32, (VT_ROWS - HEAD_DIM, lanes), 0)
            return jnp.where(rows == 0, 1.0, 0.0).astype(jnp.bfloat16)

        for g in range(N_KV_HEADS):
            vt_ref[g, HEAD_DIM:, :] = ones_rows(t + 2 * BLOCK)
            vc_ref[g, HEAD_DIM:, :] = ones_rows(2 * BLOCK)
        h0 = _rms_mod(x_ref[0:BLOCK, :], gain, scale1, shift1).astype(jnp.bfloat16)
        store_kv(*project_kv(h0), kc_ref, vc_ref, BLOCK)
        zero_k = jnp.zeros((BLOCK, 2 * HEAD_DIM), jnp.bfloat16)
        zero_v = jnp.zeros((HEAD_DIM, BLOCK), jnp.bfloat16)
        for g in range(N_KV_HEADS):
            kc_ref[g, 0:BLOCK, :] = zero_k
            vc_ref[g, 0:HEAD_DIM, 0:BLOCK] = zero_v

    h_tile = _rms_mod(x_ref[...], gain, scale1, shift1)
    h_ref[0:t, :] = h_tile.astype(jnp.bfloat16)
    ht = h_ref[0:t, :]
    q = jnp.dot(ht, win_ref[:, OFF_Q:OFF_Q + D_MODEL], preferred_element_type=jnp.float32)

    h_after = _rms_mod(xn_ref[...], g1n_ref[...], modn_ref[1:2, :], modn_ref[0:1, :])
    h_ref[t:, :] = h_after.astype(jnp.bfloat16)
    h_before = jnp.where(i > 0, _rms_mod(xp_ref[...], gain, scale1, shift1), 0.0)
    hc_ref[...] = jnp.concatenate(
        [h_before, h_tile, jnp.where(i < n_tiles - 1, h_after[:SUBLANES], 0.0)],
        axis=0).astype(jnp.bfloat16)

    for g in range(N_KV_HEADS):
        k2_ref[g, 0:2 * BLOCK, :] = kc_ref[g]
        vt_ref[g, :, 0:2 * BLOCK] = vc_ref[g]
    k, vt = project_kv(h_ref[BLOCK:, :])

    store_kv(k, vt, k2_ref, vt_ref, 2 * BLOCK)

    qscale = qg_ref[...] * (LOG2_E / math.sqrt(HEAD_DIM))
    for c in range(D_MODEL // KV_W):
        qc = q[:, c * KV_W:(c + 1) * KV_W]
        qn_ref[:, c * KV_W:(c + 1) * KV_W] = (
            qc * lax.rsqrt(_group_mean_sq(qc) + EPS) * qscale[:, c * KV_W:(c + 1) * KV_W]
        ).astype(jnp.bfloat16)

    def gate_a_block(c):
        cs = slice(c * SIDE_COLS, (c + 1) * SIDE_COLS)
        cb = jnp.dot(ht, win_ref[:, OFF_CB + c * SIDE_COLS:OFF_CB + (c + 1) * SIDE_COLS],
                     preferred_element_type=jnp.float32)
        ga = jnp.dot(ht, win_ref[:, OFF_GA + c * SIDE_COLS:OFF_GA + (c + 1) * SIDE_COLS],
                     preferred_element_type=jnp.float32)
        ya_ref[:, cs] = jax.nn.sigmoid(ga) * cb

    def conv_block(c):
        cs = slice(c * SIDE_COLS, (c + 1) * SIDE_COLS)
        hc = hc_ref[...]
        cc = jnp.dot(hc, win_ref[:, OFF_CC + c * SIDE_COLS:OFF_CC + (c + 1) * SIDE_COLS],
                     preferred_element_type=jnp.float32)
        cu = jnp.dot(hc, win_ref[:, OFF_CU + c * SIDE_COLS:OFF_CU + (c + 1) * SIDE_COLS],
                     preferred_element_type=jnp.float32)
        ya_ref[:, cs] = ya_ref[:, cs] * _dwconv3(cc * cu, convw_ref[:, cs])

    def gate_b_block(c):
        cs = slice(c * SIDE_COLS, (c + 1) * SIDE_COLS)
        gb = jnp.dot(ht, win_ref[:, OFF_GB + c * SIDE_COLS:OFF_GB + (c + 1) * SIDE_COLS],
                     preferred_element_type=jnp.float32)
        sgb_ref[:, cs] = jax.nn.sigmoid(gb)

    side_jobs = []
    for c in range(D_MODEL // SIDE_COLS):
        side_jobs.append(lambda c=c: gate_a_block(c))
        side_jobs.append(lambda c=c: conv_block(c))
        side_jobs.append(lambda c=c: gate_b_block(c))

    qlower = lax.broadcasted_iota(jnp.int32, (BLOCK, 2 * HEAD_DIM), 1) < HEAD_DIM
    zero_q = jnp.zeros((BLOCK, 2 * HEAD_DIM), jnp.bfloat16)
    mask_blk = 3 * N_Q_HEADS

    def scores_t(qb, g):
        r0 = qb * BLOCK
        kb = k2_ref[g, r0:r0 + 3 * BLOCK, :]
        qs = []
        for half in range(GROUP // 2):
            c0 = (g * GROUP + 2 * half) * HEAD_DIM
            pair = qn_ref[r0:r0 + BLOCK, c0:c0 + 2 * HEAD_DIM]
            qs.append(jnp.where(qlower, pair, zero_q))
            qs.append(jnp.where(qlower, zero_q, pair))
        q4 = jnp.concatenate(qs, axis=0)
        return lax.dot_general(kb, q4, (((1,), (1,)), ((), ())),
                               preferred_element_type=jnp.float32)

    def attend(qb, g, st):
        r0 = qb * BLOCK
        ps = []
        sink_terms = []
        for j in range(GROUP):
            hh = g * GROUP + j
            idx = [3 * hh, 3 * hh + 1, 3 * hh + 2]
            if qb == 0:
                idx[0] = jnp.where(i == 0, mask_blk, idx[0])
            if qb == t // BLOCK - 1:
                idx[2] = jnp.where(i == n_tiles - 1, mask_blk, idx[2])
            s = jnp.concatenate(
                [st[blk * BLOCK:(blk + 1) * BLOCK, j * BLOCK:(j + 1) * BLOCK] + bias_ref[idx[blk]]
                 for blk in range(3)], axis=0)
            sink = sink_ref[hh] * LOG2_E
            m = jnp.maximum(jnp.max(s, axis=0, keepdims=True), sink)
            ps.append(jnp.exp2(s - m).astype(jnp.bfloat16))
            sink_terms.append(jnp.exp2(sink - m))
        p4 = jnp.concatenate(ps, axis=1)
        ot = jnp.dot(vt_ref[g, :, r0:r0 + 3 * BLOCK], p4,
                     preferred_element_type=jnp.float32)
        den = ot[HEAD_DIM:HEAD_DIM + 1, :] + jnp.concatenate(sink_terms, axis=1)
        y = ot[0:HEAD_DIM, :] / den
        for j in range(GROUP):
            hh = g * GROUP + j
            ybt_ref[hh * HEAD_DIM:(hh + 1) * HEAD_DIM, r0:r0 + BLOCK] = y[:, j * BLOCK:(j + 1) * BLOCK]

    stages = [(qb, g) for qb in range(t // BLOCK) for g in range(N_KV_HEADS)]
    st_next = scores_t(*stages[0])
    for n, (qb, g) in enumerate(stages):
        st = st_next
        if n + 1 < len(stages):
            st_next = scores_t(*stages[n + 1])
        if n % 4 != 3 and side_jobs:
            side_jobs.pop(0)()
        attend(qb, g, st)
    for job in side_jobs:
        job()

    for g in range(N_KV_HEADS):
        kc_ref[g] = k2_ref[g, t:t + 2 * BLOCK, :]
        vc_ref[g] = vt_ref[g, :, t:t + 2 * BLOCK]

    yb = ybt_ref[...].T
    mrg_ref[...] = (ya_ref[...] + sgb_ref[...] * yb).astype(jnp.bfloat16)
    out = jnp.dot(mrg_ref[...], wo_ref[...], preferred_element_type=jnp.float32)
    o_ref[...] = x_ref[...] + gate1 * out


def _token_mixer(x, mod, norm_gain, w_in, conv_w, q_gain, k_gain, sink, w_o):
    b, s, d = x.shape
    t = SEQ_TILE
    assert s % t == 0 and t % BLOCK == 0
    n_tiles = s // t
    bpt = t // BLOCK
    rpt = t // SUBLANES
    n_blocks = s // BLOCK

    def after(bi, i):
        blk = jnp.minimum(bi * n_blocks + (i + 1) * bpt, b * n_blocks - 1)
        return blk // n_blocks, blk % n_blocks

    const = lambda shape: pl.BlockSpec(shape, lambda bi, i, *_: (0,) * len(shape),
                                       pipeline_mode=pl.Buffered(1))
    grid_spec = pltpu.PrefetchScalarGridSpec(
        num_scalar_prefetch=1,
        grid=(b, n_tiles),
        in_specs=[
            pl.BlockSpec((None, t, d), lambda bi, i, *_: (bi, i, 0)),
            pl.BlockSpec((None, SUBLANES, d), lambda bi, i, *_: (bi, jnp.maximum(i * rpt - 1, 0), 0)),
            pl.BlockSpec((None, BLOCK, d), lambda bi, i, *_: (*after(bi, i), 0)),
            pl.BlockSpec((None, 6, d), lambda bi, i, *_: (bi, 0, 0)),
            pl.BlockSpec((None, 6, d), lambda bi, i, *_: (after(bi, i)[0], 0, 0)),
            const((1, d)),
            pl.BlockSpec(memory_space=pl.ANY),
            const((3, D_CONV)),
            const((1, d)),
            const((1, KV_W)),
            pl.BlockSpec(memory_space=pl.ANY),
        ],
        out_specs=pl.BlockSpec((None, t, d), lambda bi, i, *_: (bi, i, 0)),
        scratch_shapes=[
            pltpu.VMEM((d, D_IN), jnp.bfloat16),
            pltpu.VMEM((d, d), jnp.bfloat16),
            pltpu.VMEM((KV_W, d), jnp.bfloat16),
            pltpu.VMEM((t + BLOCK, d), jnp.bfloat16),
            pltpu.VMEM((t + 2 * SUBLANES, d), jnp.bfloat16),
            pltpu.VMEM((N_KV_HEADS, t + 2 * BLOCK, 2 * HEAD_DIM), jnp.bfloat16),
            pltpu.VMEM((N_KV_HEADS, VT_ROWS, t + 2 * BLOCK), jnp.bfloat16),
            pltpu.VMEM((N_KV_HEADS, 2 * BLOCK, 2 * HEAD_DIM), jnp.bfloat16),
            pltpu.VMEM((N_KV_HEADS, VT_ROWS, 2 * BLOCK), jnp.bfloat16),
            pltpu.VMEM((t, d), jnp.bfloat16),
            pltpu.VMEM((t, d), jnp.float32),
            pltpu.VMEM((t, d), jnp.float32),
            pltpu.VMEM((d, t), jnp.float32),
            pltpu.VMEM((t, d), jnp.bfloat16),
            pltpu.VMEM((3 * N_Q_HEADS + 1, BLOCK, BLOCK), jnp.float32),
        ],
    )
    return pl.pallas_call(
        _mixer_kernel,
        grid_spec=grid_spec,
        out_shape=jax.ShapeDtypeStruct((b, s, d), jnp.float32),
        compiler_params=pltpu.CompilerParams(
            dimension_semantics=("arbitrary", "arbitrary"),
            vmem_limit_bytes=VMEM_LIMIT_BYTES),
        name="token_mixer",
    )(sink, x, x, x, mod, mod, norm_gain.reshape(1, d), w_in, conv_w,
      jnp.tile(q_gain, N_Q_HEADS).reshape(1, d), jnp.tile(k_gain, N_KV_HEADS).reshape(1, KV_W), w_o)


def _ffn_kernel(x_ref, xp_ref, xn_ref,
                mod_ref, g2n_ref, wup_hbm, fcw_ref, fcb_ref, wdn_hbm,
                o_ref,
                wup_ref,
                wdn_ref,
                ht_ref,
                he_ref,
                act_ref):
    t = x_ref.shape[0]
    i = pl.program_id(1)
    n_tiles = pl.num_programs(1)

    @pl.when(jnp.logical_and(pl.program_id(0) == 0, i == 0))
    def _():
        _stage_weights(wup_hbm, wup_ref)
        _stage_weights(wdn_hbm, wdn_ref)

    shift2 = mod_ref[3:4, :]
    scale2 = mod_ref[4:5, :]
    gate2 = mod_ref[5:6, :]
    gain = g2n_ref[...]

    h_prev = _rms_mod(xp_ref[...], gain, scale2, shift2)
    h_next = _rms_mod(xn_ref[...], gain, scale2, shift2)
    h_prev = jnp.where(i > 0, h_prev, 0.0)
    h_next = jnp.where(i < n_tiles - 1, h_next, 0.0)
    h_tile = _rms_mod(x_ref[...], gain, scale2, shift2)
    ht_ref[...] = h_tile.astype(jnp.bfloat16)
    he_ref[...] = jnp.concatenate([h_prev, h_tile, h_next], axis=0).astype(jnp.bfloat16)

    for c in range(D_FF // FF_CHUNK):
        c0 = c * FF_CHUNK
        a_ext = jnp.dot(he_ref[...], wup_ref[:, c0:c0 + FF_CHUNK],
                        preferred_element_type=jnp.float32)
        gbr = jnp.dot(ht_ref[...], wup_ref[:, D_FF + c0:D_FF + c0 + FF_CHUNK],
                      preferred_element_type=jnp.float32)
        a = _dwconv3(a_ext, fcw_ref[:, c0:c0 + FF_CHUNK]) + fcb_ref[:, c0:c0 + FF_CHUNK]
        gelu = 0.5 * a * (1.0 + jnp.tanh(math.sqrt(2.0 / math.pi) * (a + 0.044715 * (a * a * a))))
        act_ref[:, c0:c0 + FF_CHUNK] = (gelu * gbr).astype(jnp.bfloat16)

    y = jnp.dot(act_ref[...], wdn_ref[...], preferred_element_type=jnp.float32)
    o_ref[...] = x_ref[...] + gate2 * y


def _conv_ffn(x, mod, norm_gain, w_up, ffn_conv_w, ffn_conv_b, w_down):
    b, s, d = x.shape
    t = SEQ_TILE
    assert s % t == 0 and D_FF % FF_CHUNK == 0
    n_tiles = s // t
    rpt = t // SUBLANES
    n_groups = s // SUBLANES

    const = lambda shape: pl.BlockSpec(shape, lambda bi, i: (0,) * len(shape),
                                       pipeline_mode=pl.Buffered(1))
    return pl.pallas_call(
        _ffn_kernel,
        grid=(b, n_tiles),
        in_specs=[
            pl.BlockSpec((None, t, d), lambda bi, i: (bi, i, 0)),
            pl.BlockSpec((None, SUBLANES, d), lambda bi, i: (bi, jnp.maximum(i * rpt - 1, 0), 0)),
            pl.BlockSpec((None, SUBLANES, d),
                         lambda bi, i: (bi, jnp.minimum((i + 1) * rpt, n_groups - 1), 0)),
            pl.BlockSpec((None, 6, d), lambda bi, i: (bi, 0, 0)),
            const((1, d)),
            pl.BlockSpec(memory_space=pl.ANY),
            const((3, D_FF)),
            const((1, D_FF)),
            pl.BlockSpec(memory_space=pl.ANY),
        ],
        out_specs=pl.BlockSpec((None, t, d), lambda bi, i: (bi, i, 0)),
        out_shape=jax.ShapeDtypeStruct((b, s, d), jnp.float32),
        scratch_shapes=[
            pltpu.VMEM((d, 2 * D_FF), jnp.bfloat16),
            pltpu.VMEM((D_FF, d), jnp.bfloat16),
            pltpu.VMEM((t, d), jnp.bfloat16),
            pltpu.VMEM((t + 2 * SUBLANES, d), jnp.bfloat16),
            pltpu.VMEM((t, D_FF), jnp.bfloat16),
        ],
        compiler_params=pltpu.CompilerParams(
            dimension_semantics=("arbitrary", "arbitrary"),
            vmem_limit_bytes=VMEM_LIMIT_BYTES),
        name="conv_ffn",
    )(x, x, x, mod, norm_gain.reshape(1, d), w_up, ffn_conv_w, ffn_conv_b.reshape(1, D_FF), w_down)


def kernel(x, c, w_ada, b_ada, norm1_gain, w_in, conv_w, q_gain, k_gain, sink, w_o, norm2_gain,
           w_up, ffn_conv_w, ffn_conv_b, w_down):
    depth = w_ada.shape[0]
    b, s, d = x.shape
    for l in range(depth):
        mod = _modulation(c, w_ada[l], b_ada[l]).reshape(b, 6, d)
        x = _token_mixer(x, mod, norm1_gain[l], w_in[l], conv_w[l], q_gain[l], k_gain[l],
                         sink[l], w_o[l])
        x = _conv_ffn(x, mod, norm2_gain[l], w_up[l], ffn_conv_w[l], ffn_conv_b[l], w_down[l])
    return x
```

```python
import math

import jax
import jax.numpy as jnp
from jax import lax
from jax.experimental import pallas as pl
from jax.experimental.pallas import tpu as pltpu

D_MODEL = 1024
HEAD_DIM = 64
N_Q_HEADS = D_MODEL // HEAD_DIM
N_KV_HEADS = N_Q_HEADS // 4
GROUP = N_Q_HEADS // N_KV_HEADS
D_CONV = D_MODEL
WINDOW = 128
BLOCK = 128
D_FF = ((8 * D_MODEL // 3 + 127) // 128) * 128
EPS = 1e-6
NEG_INF = -1e30
LOG2_E = math.log2(math.e)

OFF_CB = 0
OFF_CC = OFF_CB + D_CONV
OFF_CU = OFF_CC + D_CONV
OFF_Q = OFF_CU + D_CONV
OFF_K = OFF_Q + N_Q_HEADS * HEAD_DIM
OFF_V = OFF_K + N_KV_HEADS * HEAD_DIM
OFF_GA = OFF_V + N_KV_HEADS * HEAD_DIM
OFF_GB = OFF_GA + D_MODEL
D_IN = OFF_GB + D_MODEL

KV_W = N_KV_HEADS * HEAD_DIM
SUBLANES = 8
BF16_ROWS = 16
VT_ROWS = HEAD_DIM + BF16_ROWS
SEQ_TILE = 512
FF_CHUNK = 256
SIDE_COLS = 256
WEIGHT_CHUNK_BYTES = 2 * 1024 * 1024
WEIGHT_DMA_DEPTH = 8
VMEM_LIMIT_BYTES = 56 * 1024 * 1024

ALIBI_SLOPES = [2.0 ** (-8.0 * (h + 1) / N_Q_HEADS) for h in range(N_Q_HEADS)]


def _rms_mod(xv, gain, scale, shift):
    ms = jnp.mean(xv * xv, axis=-1, keepdims=True)
    return xv * lax.rsqrt(ms + EPS) * (gain * (1.0 + scale)) + shift


def _dwconv3(v_ext, w):
    rows = v_ext.shape[0]
    t = rows - 2 * SUBLANES
    prev = pltpu.roll(v_ext, 1, 0)[SUBLANES:SUBLANES + t]
    nxt = pltpu.roll(v_ext, rows - 1, 0)[SUBLANES:SUBLANES + t]
    return w[0:1, :] * prev + w[1:2, :] * v_ext[SUBLANES:SUBLANES + t] + w[2:3, :] * nxt


def _stage_weights(src_hbm, dst_ref):
    rows, cols = src_hbm.shape
    chunk_rows = max(r for r in range(BF16_ROWS, rows + 1, BF16_ROWS)
                     if rows % r == 0 and r * cols * 4 <= WEIGHT_CHUNK_BYTES
                     and rows // r >= WEIGHT_DMA_DEPTH)
    n_chunks = rows // chunk_rows

    def body(stage_ref, sem_ref):
        def copy(c):
            slot = lax.rem(c, WEIGHT_DMA_DEPTH)
            r0 = pl.multiple_of(c * chunk_rows, chunk_rows)
            return pltpu.make_async_copy(src_hbm.at[pl.ds(r0, chunk_rows), :],
                                         stage_ref.at[slot], sem_ref.at[slot])

        for c in range(WEIGHT_DMA_DEPTH - 1):
            copy(c).start()

        def step(c, carry):
            @pl.when(c + WEIGHT_DMA_DEPTH - 1 < n_chunks)
            def _():
                copy(c + WEIGHT_DMA_DEPTH - 1).start()

            copy(c).wait()
            r0 = pl.multiple_of(c * chunk_rows, chunk_rows)
            dst_ref[pl.ds(r0, chunk_rows), :] = (
                stage_ref[lax.rem(c, WEIGHT_DMA_DEPTH)].astype(jnp.bfloat16))
            return carry

        lax.fori_loop(0, n_chunks, step, 0)

    pl.run_scoped(body, pltpu.VMEM((WEIGHT_DMA_DEPTH, chunk_rows, cols), jnp.float32),
                  pltpu.SemaphoreType.DMA((WEIGHT_DMA_DEPTH,)))


def _ada_kernel(c_ref, w_ref, b_ref, o_ref):
    c = c_ref[...]
    act = c * jax.nn.sigmoid(c)
    o_ref[...] = jnp.dot(act.astype(jnp.bfloat16), w_ref[...].astype(jnp.bfloat16),
                         preferred_element_type=jnp.float32) + b_ref[...]


def _modulation(c, w_ada, b_ada):
    b, d = c.shape
    n = w_ada.shape[1]
    rows = SUBLANES
    c_pad = jnp.zeros((rows, d), c.dtype).at[:b].set(c)
    tn = d
    out = pl.pallas_call(
        _ada_kernel,
        grid=(n // tn,),
        in_specs=[
            pl.BlockSpec((rows, d), lambda j: (0, 0)),
            pl.BlockSpec((d, tn), lambda j: (0, j)),
            pl.BlockSpec((1, tn), lambda j: (0, j)),
        ],
        out_specs=pl.BlockSpec((rows, tn), lambda j: (0, j)),
        out_shape=jax.ShapeDtypeStruct((rows, n), jnp.float32),
        compiler_params=pltpu.CompilerParams(dimension_semantics=("arbitrary",)),
        name="ada_modulation",
    )(c_pad, w_ada, b_ada.reshape(1, n))
    return out[:b]


def _group_mean_sq(v):
    sq = v * v
    lower = lax.broadcasted_iota(jnp.int32, (1, 2 * HEAD_DIM), 1) < HEAD_DIM
    sums = []
    for c in range(v.shape[1] // (2 * HEAD_DIM)):
        pair = sq[:, c * 2 * HEAD_DIM:(c + 1) * 2 * HEAD_DIM]
        first = jnp.sum(jnp.where(lower, pair, 0.0), axis=-1, keepdims=True)
        second = jnp.sum(jnp.where(lower, 0.0, pair), axis=-1, keepdims=True)
        sums.append(jnp.where(lower, first, second))
    return jnp.concatenate(sums, axis=1) * (1.0 / HEAD_DIM)


def _mixer_kernel(sink_ref,
                  x_ref, xp_ref, xn_ref,
                  mod_ref, modn_ref,
                  g1n_ref, win_hbm, convw_ref, qg_ref, kg_ref, wo_hbm,
                  o_ref,
                  win_ref,
                  wo_ref,
                  wvt_ref,
                  h_ref,
                  hc_ref,
                  k2_ref,
                  vt_ref,
                  kc_ref,
                  vc_ref,
                  qn_ref,
                  ya_ref,
                  sgb_ref,
                  ybt_ref,
                  mrg_ref,
                  bias_ref):
    t = x_ref.shape[0]
    i = pl.program_id(1)
    n_tiles = pl.num_programs(1)
    first = jnp.logical_and(pl.program_id(0) == 0, i == 0)
    shift1 = mod_ref[0:1, :]
    scale1 = mod_ref[1:2, :]
    gate1 = mod_ref[2:3, :]
    gain = g1n_ref[...]
    lower = lax.broadcasted_iota(jnp.int32, (1, 2 * HEAD_DIM), 1) < HEAD_DIM

    def project_kv(h_rows):
        k = jnp.dot(h_rows, win_ref[:, OFF_K:OFF_K + KV_W], preferred_element_type=jnp.float32)
        vt = lax.dot_general(wvt_ref[...], h_rows, (((1,), (1,)), ((), ())),
                             preferred_element_type=jnp.float32)
        return k, vt

    def store_kv(k, vt, k_dst, v_dst, r0):
        rows = k.shape[0]
        kn = k * lax.rsqrt(_group_mean_sq(k) + EPS) * kg_ref[...]
        for pair in range(N_KV_HEADS // 2):
            tile = kn[:, pair * 2 * HEAD_DIM:(pair + 1) * 2 * HEAD_DIM]
            swapped = pltpu.roll(tile, HEAD_DIM, 1)
            k_dst[2 * pair, r0:r0 + rows, :] = jnp.where(lower, tile, swapped).astype(jnp.bfloat16)
            k_dst[2 * pair + 1, r0:r0 + rows, :] = jnp.where(lower, swapped, tile).astype(jnp.bfloat16)
        for g in range(N_KV_HEADS):
            v_dst[g, 0:HEAD_DIM, r0:r0 + rows] = (
                vt[g * HEAD_DIM:(g + 1) * HEAD_DIM, :].astype(jnp.bfloat16))

    @pl.when(first)
    def _():
        _stage_weights(win_hbm, win_ref)
        _stage_weights(wo_hbm, wo_ref)
        wvt_ref[...] = win_ref[:, OFF_V:OFF_V + KV_W].astype(jnp.float32).T.astype(jnp.bfloat16)
        kj = lax.broadcasted_iota(jnp.int32, (BLOCK, BLOCK), 0)
        qi = lax.broadcasted_iota(jnp.int32, (BLOCK, BLOCK), 1)
        for blk in range(3):
            dist = jnp.abs(qi + BLOCK - (kj + blk * BLOCK))
            distf = dist.astype(jnp.float32)
            for h in range(N_Q_HEADS):
                bias_ref[3 * h + blk] = jnp.where(dist <= WINDOW,
                                                  -(ALIBI_SLOPES[h] * LOG2_E) * distf, NEG_INF)
        bias_ref[3 * N_Q_HEADS] = jnp.full((BLOCK, BLOCK), NEG_INF, jnp.float32)
        def ones_rows(lanes):
            rows = lax.broadcasted_iota(jnp.int32, (VT_ROWS - HEAD_DIM, lanes), 0)
            return jnp.where(rows == 0, 1.0, 0.0).astype(jnp.bfloat16)

        for g in range(N_KV_HEADS):
            vt_ref[g, HEAD_DIM:, :] = ones_rows(t + 2 * BLOCK)
            vc_ref[g, HEAD_DIM:, :] = ones_rows(2 * BLOCK)
        h0 = _rms_mod(x_ref[0:BLOCK, :], gain, scale1, shift1).astype(jnp.bfloat16)
        store_kv(*project_kv(h0), kc_ref, vc_ref, BLOCK)
        zero_k = jnp.zeros((BLOCK, 2 * HEAD_DIM), jnp.bfloat16)
        zero_v = jnp.zeros((HEAD_DIM, BLOCK), jnp.bfloat16)
        for g in range(N_KV_HEADS):
            kc_ref[g, 0:BLOCK, :] = zero_k
            vc_ref[g, 0:HEAD_DIM, 0:BLOCK] = zero_v

    h_tile = _rms_mod(x_ref[...], gain, scale1, shift1)
    h_ref[0:t, :] = h_tile.astype(jnp.bfloat16)
    ht = h_ref[0:t, :]
    q = jnp.dot(ht, win_ref[:, OFF_Q:OFF_Q + D_MODEL], preferred_element_type=jnp.float32)

    h_after = _rms_mod(xn_ref[...], g1n_ref[...], modn_ref[1:2, :], modn_ref[0:1, :])
    h_ref[t:, :] = h_after.astype(jnp.bfloat16)
    h_before = jnp.where(i > 0, _rms_mod(xp_ref[...], gain, scale1, shift1), 0.0)
    hc_ref[...] = jnp.concatenate(
        [h_before, h_tile, jnp.where(i < n_tiles - 1, h_after[:SUBLANES], 0.0)],
        axis=0).astype(jnp.bfloat16)

    for g in range(N_KV_HEADS):
        k2_ref[g, 0:2 * BLOCK, :] = kc_ref[g]
        vt_ref[g, :, 0:2 * BLOCK] = vc_ref[g]
    k, vt = project_kv(h_ref[BLOCK:, :])

    store_kv(k, vt, k2_ref, vt_ref, 2 * BLOCK)

    qscale = qg_ref[...] * (LOG2_E / math.sqrt(HEAD_DIM))
    for c in range(D_MODEL // KV_W):
        qc = q[:, c * KV_W:(c + 1) * KV_W]
        qn_ref[:, c * KV_W:(c + 1) * KV_W] = (
            qc * lax.rsqrt(_group_mean_sq(qc) + EPS) * qscale[:, c * KV_W:(c + 1) * KV_W]
        ).astype(jnp.bfloat16)

    def gate_a_block(c):
        cs = slice(c * SIDE_COLS, (c + 1) * SIDE_COLS)
        cb = jnp.dot(ht, win_ref[:, OFF_CB + c * SIDE_COLS:OFF_CB + (c + 1) * SIDE_COLS],
                     preferred_element_type=jnp.float32)
        ga = jnp.dot(ht, win_ref[:, OFF_GA + c * SIDE_COLS:OFF_GA + (c + 1) * SIDE_COLS],
                     preferred_element_type=jnp.float32)
        ya_ref[:, cs] = jax.nn.sigmoid(ga) * cb

    def conv_block(c):
        cs = slice(c * SIDE_COLS, (c + 1) * SIDE_COLS)
        hc = hc_ref[...]
        cc = jnp.dot(hc, win_ref[:, OFF_CC + c * SIDE_COLS:OFF_CC + (c + 1) * SIDE_COLS],
                     preferred_element_type=jnp.float32)
        cu = jnp.dot(hc, win_ref[:, OFF_CU + c * SIDE_COLS:OFF_CU + (c + 1) * SIDE_COLS],
                     preferred_element_type=jnp.float32)
        ya_ref[:, cs] = ya_ref[:, cs] * _dwconv3(cc * cu, convw_ref[:, cs])

    def gate_b_block(c):
        cs = slice(c * SIDE_COLS, (c + 1) * SIDE_COLS)
        gb = jnp.dot(ht, win_ref[:, OFF_GB + c * SIDE_COLS:OFF_GB + (c + 1) * SIDE_COLS],
                     preferred_element_type=jnp.float32)
        sgb_ref[:, cs] = jax.nn.sigmoid(gb)

    side_jobs = []
    for c in range(D_MODEL // SIDE_COLS):
        side_jobs.append(lambda c=c: gate_a_block(c))
        side_jobs.append(lambda c=c: conv_block(c))
        side_jobs.append(lambda c=c: gate_b_block(c))

    qlower = lax.broadcasted_iota(jnp.int32, (BLOCK, 2 * HEAD_DIM), 1) < HEAD_DIM
    zero_q = jnp.zeros((BLOCK, 2 * HEAD_DIM), jnp.bfloat16)
    mask_blk = 3 * N_Q_HEADS

    def scores_t(qb, g):
        r0 = qb * BLOCK
        kb = k2_ref[g, r0:r0 + 3 * BLOCK, :]
        qs = []
        for half in range(GROUP // 2):
            c0 = (g * GROUP + 2 * half) * HEAD_DIM
            pair = qn_ref[r0:r0 + BLOCK, c0:c0 + 2 * HEAD_DIM]
            qs.append(jnp.where(qlower, pair, zero_q))
            qs.append(jnp.where(qlower, zero_q, pair))
        q4 = jnp.concatenate(qs, axis=0)
        return lax.dot_general(kb, q4, (((1,), (1,)), ((), ())),
                               preferred_element_type=jnp.float32)

    def attend(qb, g, st):
        r0 = qb * BLOCK
        ps = []
        sink_terms = []
        for j in range(GROUP):
            hh = g * GROUP + j
            idx = [3 * hh, 3 * hh + 1, 3 * hh + 2]
            if qb == 0:
                idx[0] = jnp.where(i == 0, mask_blk, idx[0])
            if qb == t // BLOCK - 1:
                idx[2] = jnp.where(i == n_tiles - 1, mask_blk, idx[2])
            s = jnp.concatenate(
                [st[blk * BLOCK:(blk + 1) * BLOCK, j * BLOCK:(j + 1) * BLOCK] + bias_ref[idx[blk]]
                 for blk in range(3)], axis=0)
            sink = sink_ref[hh] * LOG2_E
            m = jnp.maximum(jnp.max(s, axis=0, keepdims=True), sink)
            ps.append(jnp.exp2(s - m).astype(jnp.bfloat16))
            sink_terms.append(jnp.exp2(sink - m))
        p4 = jnp.concatenate(ps, axis=1)
        ot = jnp.dot(vt_ref[g, :, r0:r0 + 3 * BLOCK], p4,
                     preferred_element_type=jnp.float32)
        den = ot[HEAD_DIM:HEAD_DIM + 1, :] + jnp.concatenate(sink_terms, axis=1)
        y = ot[0:HEAD_DIM, :] / den
        for j in range(GROUP):
            hh = g * GROUP + j
            ybt_ref[hh * HEAD_DIM:(hh + 1) * HEAD_DIM, r0:r0 + BLOCK] = y[:, j * BLOCK:(j + 1) * BLOCK]

    stages = [(qb, g) for qb in range(t // BLOCK) for g in range(N_KV_HEADS)]
    st_next = scores_t(*stages[0])
    for n, (qb, g) in enumerate(stages):
        st = st_next
        if n + 1 < len(stages):
            st_next = scores_t(*stages[n + 1])
        if n % 4 != 3 and side_jobs:
            side_jobs.pop(0)()
        attend(qb, g, st)
    for job in side_jobs:
        job()

    for g in range(N_KV_HEADS):
        kc_ref[g] = k2_ref[g, t:t + 2 * BLOCK, :]
        vc_ref[g] = vt_ref[g, :, t:t + 2 * BLOCK]

    yb = ybt_ref[...].T
    mrg_ref[...] = (ya_ref[...] + sgb_ref[...] * yb).astype(jnp.bfloat16)
    out = jnp.dot(mrg_ref[...], wo_ref[...], preferred_element_type=jnp.float32)
    o_ref[...] = x_ref[...] + gate1 * out


def _token_mixer(x, mod, norm_gain, w_in, conv_w, q_gain, k_gain, sink, w_o):
    b, s, d = x.shape
    t = SEQ_TILE
    assert s % t == 0 and t % BLOCK == 0
    n_tiles = s // t
    bpt = t // BLOCK
    rpt = t // SUBLANES
    n_blocks = s // BLOCK

    def after(bi, i):
        blk = jnp.minimum(bi * n_blocks + (i + 1) * bpt, b * n_blocks - 1)
        return blk // n_blocks, blk % n_blocks

    const = lambda shape: pl.BlockSpec(shape, lambda bi, i, *_: (0,) * len(shape),
                                       pipeline_mode=pl.Buffered(1))
    grid_spec = pltpu.PrefetchScalarGridSpec(
        num_scalar_prefetch=1,
        grid=(b, n_tiles),
        in_specs=[
            pl.BlockSpec((None, t, d), lambda bi, i, *_: (bi, i, 0)),
            pl.BlockSpec((None, SUBLANES, d), lambda bi, i, *_: (bi, jnp.maximum(i * rpt - 1, 0), 0)),
            pl.BlockSpec((None, BLOCK, d), lambda bi, i, *_: (*after(bi, i), 0)),
            pl.BlockSpec((None, 6, d), lambda bi, i, *_: (bi, 0, 0)),
            pl.BlockSpec((None, 6, d), lambda bi, i, *_: (after(bi, i)[0], 0, 0)),
            const((1, d)),
            pl.BlockSpec(memory_space=pl.ANY),
            const((3, D_CONV)),
            const((1, d)),
            const((1, KV_W)),
            pl.BlockSpec(memory_space=pl.ANY),
        ],
        out_specs=pl.BlockSpec((None, t, d), lambda bi, i, *_: (bi, i, 0)),
        scratch_shapes=[
            pltpu.VMEM((d, D_IN), jnp.bfloat16),
            pltpu.VMEM((d, d), jnp.bfloat16),
            pltpu.VMEM((KV_W, d), jnp.bfloat16),
            pltpu.VMEM((t + BLOCK, d), jnp.bfloat16),
            pltpu.VMEM((t + 2 * SUBLANES, d), jnp.bfloat16),
            pltpu.VMEM((N_KV_HEADS, t + 2 * BLOCK, 2 * HEAD_DIM), jnp.bfloat16),
            pltpu.VMEM((N_KV_HEADS, VT_ROWS, t + 2 * BLOCK), jnp.bfloat16),
            pltpu.VMEM((N_KV_HEADS, 2 * BLOCK, 2 * HEAD_DIM), jnp.bfloat16),
            pltpu.VMEM((N_KV_HEADS, VT_ROWS, 2 * BLOCK), jnp.bfloat16),
            pltpu.VMEM((t, d), jnp.bfloat16),
            pltpu.VMEM((t, d), jnp.float32),
            pltpu.VMEM((t, d), jnp.float32),
            pltpu.VMEM((d, t), jnp.float32),
            pltpu.VMEM((t, d), jnp.bfloat16),
            pltpu.VMEM((3 * N_Q_HEADS + 1, BLOCK, BLOCK), jnp.float32),
        ],
    )
    return pl.pallas_call(
        _mixer_kernel,
        grid_spec=grid_spec,
        out_shape=jax.ShapeDtypeStruct((b, s, d), jnp.float32),
        compiler_params=pltpu.CompilerParams(
            dimension_semantics=("arbitrary", "arbitrary"),
            vmem_limit_bytes=VMEM_LIMIT_BYTES),
        name="token_mixer",
    )(sink, x, x, x, mod, mod, norm_gain.reshape(1, d), w_in, conv_w,
      jnp.tile(q_gain, N_Q_HEADS).reshape(1, d), jnp.tile(k_gain, N_KV_HEADS).reshape(1, KV_W), w_o)


def _ffn_kernel(x_ref, xp_ref, xn_ref,
                mod_ref, g2n_ref, wup_hbm, fcw_ref, fcb_ref, wdn_hbm,
                o_ref,
                wup_ref,
                wdn_ref,
                ht_ref,
                he_ref,
                act_ref):
    t = x_ref.shape[0]
    i = pl.program_id(1)
    n_tiles = pl.num_programs(1)

    @pl.when(jnp.logical_and(pl.program_id(0) == 0, i == 0))
    def _():
        _stage_weights(wup_hbm, wup_ref)
        _stage_weights(wdn_hbm, wdn_ref)

    shift2 = mod_ref[3:4, :]
    scale2 = mod_ref[4:5, :]
    gate2 = mod_ref[5:6, :]
    gain = g2n_ref[...]

    h_prev = _rms_mod(xp_ref[...], gain, scale2, shift2)
    h_next = _rms_mod(xn_ref[...], gain, scale2, shift2)
    h_prev = jnp.where(i > 0, h_prev, 0.0)
    h_next = jnp.where(i < n_tiles - 1, h_next, 0.0)
    h_tile = _rms_mod(x_ref[...], gain, scale2, shift2)
    ht_ref[...] = h_tile.astype(jnp.bfloat16)
    he_ref[...] = jnp.concatenate([h_prev, h_tile, h_next], axis=0).astype(jnp.bfloat16)

    for c in range(D_FF // FF_CHUNK):
        c0 = c * FF_CHUNK
        a_ext = jnp.dot(he_ref[...], wup_ref[:, c0:c0 + FF_CHUNK],
                        preferred_element_type=jnp.float32)
        gbr = jnp.dot(ht_ref[...], wup_ref[:, D_FF + c0:D_FF + c0 + FF_CHUNK],
                      preferred_element_type=jnp.float32)
        a = _dwconv3(a_ext, fcw_ref[:, c0:c0 + FF_CHUNK]) + fcb_ref[:, c0:c0 + FF_CHUNK]
        gelu = 0.5 * a * (1.0 + jnp.tanh(math.sqrt(2.0 / math.pi) * (a + 0.044715 * (a * a * a))))
        act_ref[:, c0:c0 + FF_CHUNK] = (gelu * gbr).astype(jnp.bfloat16)

    y = jnp.dot(act_ref[...], wdn_ref[...], preferred_element_type=jnp.float32)
    o_ref[...] = x_ref[...] + gate2 * y


def _conv_ffn(x, mod, norm_gain, w_up, ffn_conv_w, ffn_conv_b, w_down):
    b, s, d = x.shape
    t = SEQ_TILE
    assert s % t == 0 and D_FF % FF_CHUNK == 0
    n_tiles = s // t
    rpt = t // SUBLANES
    n_groups = s // SUBLANES

    const = lambda shape: pl.BlockSpec(shape, lambda bi, i: (0,) * len(shape),
                                       pipeline_mode=pl.Buffered(1))
    return pl.pallas_call(
        _ffn_kernel,
        grid=(b, n_tiles),
        in_specs=[
            pl.BlockSpec((None, t, d), lambda bi, i: (bi, i, 0)),
            pl.BlockSpec((None, SUBLANES, d), lambda bi, i: (bi, jnp.maximum(i * rpt - 1, 0), 0)),
            pl.BlockSpec((None, SUBLANES, d),
                         lambda bi, i: (bi, jnp.minimum((i + 1) * rpt, n_groups - 1), 0)),
            pl.BlockSpec((None, 6, d), lambda bi, i: (bi, 0, 0)),
            const((1, d)),
            pl.BlockSpec(memory_space=pl.ANY),
            const((3, D_FF)),
            const((1, D_FF)),
            pl.BlockSpec(memory_space=pl.ANY),
        ],
        out_specs=pl.BlockSpec((None, t, d), lambda bi, i: (bi, i, 0)),
        out_shape=jax.ShapeDtypeStruct((b, s, d), jnp.float32),
        scratch_shapes=[
            pltpu.VMEM((d, 2 * D_FF), jnp.bfloat16),
            pltpu.VMEM((D_FF, d), jnp.bfloat16),
            pltpu.VMEM((t, d), jnp.bfloat16),
            pltpu.VMEM((t + 2 * SUBLANES, d), jnp.bfloat16),
            pltpu.VMEM((t, D_FF), jnp.bfloat16),
        ],
        compiler_params=pltpu.CompilerParams(
            dimension_semantics=("arbitrary", "arbitrary"),
            vmem_limit_bytes=VMEM_LIMIT_BYTES),
        name="conv_ffn",
    )(x, x, x, mod, norm_gain.reshape(1, d), w_up, ffn_conv_w, ffn_conv_b.reshape(1, D_FF), w_down)


def kernel(x, c, w_ada, b_ada, norm1_gain, w_in, conv_w, q_gain, k_gain, sink, w_o, norm2_gain,
           w_up, ffn_conv_w, ffn_conv_b, w_down):
    depth = w_ada.shape[0]
    b, s, d = x.shape
    for l in range(depth):
        mod = _modulation(c, w_ada[l], b_ada[l]).reshape(b, 6, d)
        x = _token_mixer(x, mod, norm1_gain[l], w_in[l], conv_w[l], q_gain[l], k_gain[l],
                         sink[l], w_o[l])
        x = _conv_ffn(x, mod, norm2_gain[l], w_up[l], ffn_conv_w[l], ffn_conv_b[l], w_down[l])
    return x
```

```python
import math

import jax
import jax.numpy as jnp
from jax import lax
from jax.experimental import pallas as pl
from jax.experimental.pallas import tpu as pltpu

D_MODEL = 1024
HEAD_DIM = 64
N_Q_HEADS = D_MODEL // HEAD_DIM
N_KV_HEADS = N_Q_HEADS // 4
GROUP = N_Q_HEADS // N_KV_HEADS
D_CONV = D_MODEL
WINDOW = 128
BLOCK = 128
D_FF = ((8 * D_MODEL // 3 + 127) // 128) * 128
EPS = 1e-6
NEG_INF = -1e30
LOG2_E = math.log2(math.e)

OFF_CB = 0
OFF_CC = OFF_CB + D_CONV
OFF_CU = OFF_CC + D_CONV
OFF_Q = OFF_CU + D_CONV
OFF_K = OFF_Q + N_Q_HEADS * HEAD_DIM
OFF_V = OFF_K + N_KV_HEADS * HEAD_DIM
OFF_GA = OFF_V + N_KV_HEADS * HEAD_DIM
OFF_GB = OFF_GA + D_MODEL
D_IN = OFF_GB + D_MODEL

KV_W = N_KV_HEADS * HEAD_DIM
SUBLANES = 8
BF16_ROWS = 16
VT_ROWS = HEAD_DIM + BF16_ROWS
SEQ_TILE = 512
FF_CHUNK = 256
SIDE_COLS = 256
WEIGHT_CHUNK_BYTES = 1024 * 1024
WEIGHT_DMA_DEPTH = 8
VMEM_LIMIT_BYTES = 56 * 1024 * 1024

ALIBI_SLOPES = [2.0 ** (-8.0 * (h + 1) / N_Q_HEADS) for h in range(N_Q_HEADS)]


def _rms_mod(xv, gain, scale, shift):
    ms = jnp.mean(xv * xv, axis=-1, keepdims=True)
    return xv * lax.rsqrt(ms + EPS) * (gain * (1.0 + scale)) + shift


def _dwconv3(v_ext, w):
    rows = v_ext.shape[0]
    t = rows - 2 * SUBLANES
    prev = pltpu.roll(v_ext, 1, 0)[SUBLANES:SUBLANES + t]
    nxt = pltpu.roll(v_ext, rows - 1, 0)[SUBLANES:SUBLANES + t]
    return w[0:1, :] * prev + w[1:2, :] * v_ext[SUBLANES:SUBLANES + t] + w[2:3, :] * nxt


def _stage_weights(src_hbm, dst_ref):
    rows, cols = src_hbm.shape
    chunk_rows = max(r for r in range(BF16_ROWS, rows + 1, BF16_ROWS)
                     if rows % r == 0 and r * cols * 4 <= WEIGHT_CHUNK_BYTES
                     and rows // r >= WEIGHT_DMA_DEPTH)
    n_chunks = rows // chunk_rows

    def body(stage_ref, sem_ref):
        def copy(c):
            slot = lax.rem(c, WEIGHT_DMA_DEPTH)
            r0 = pl.multiple_of(c * chunk_rows, chunk_rows)
            return pltpu.make_async_copy(src_hbm.at[pl.ds(r0, chunk_rows), :],
                                         stage_ref.at[slot], sem_ref.at[slot])

        for c in range(WEIGHT_DMA_DEPTH - 1):
            copy(c).start()

        def step(c, carry):
            @pl.when(c + WEIGHT_DMA_DEPTH - 1 < n_chunks)
            def _():
                copy(c + WEIGHT_DMA_DEPTH - 1).start()

            copy(c).wait()
            r0 = pl.multiple_of(c * chunk_rows, chunk_rows)
            dst_ref[pl.ds(r0, chunk_rows), :] = (
                stage_ref[lax.rem(c, WEIGHT_DMA_DEPTH)].astype(jnp.bfloat16))
            return carry

        lax.fori_loop(0, n_chunks, step, 0)

    pl.run_scoped(body, pltpu.VMEM((WEIGHT_DMA_DEPTH, chunk_rows, cols), jnp.float32),
                  pltpu.SemaphoreType.DMA((WEIGHT_DMA_DEPTH,)))


def _ada_kernel(c_ref, w_ref, b_ref, o_ref):
    c = c_ref[...]
    act = c * jax.nn.sigmoid(c)
    pad = jnp.zeros((SUBLANES - act.shape[0], act.shape[1]), act.dtype)
    act = jnp.concatenate([act, pad], axis=0).astype(jnp.bfloat16)
    out = jnp.dot(act, w_ref[...].astype(jnp.bfloat16), preferred_element_type=jnp.float32)
    o_ref[...] = out[:c.shape[0]] + b_ref[...]


def _modulation(c, w_ada, b_ada):
    b, d = c.shape
    n = w_ada.shape[1]
    assert b <= SUBLANES
    return pl.pallas_call(
        _ada_kernel,
        grid=(n // d,),
        in_specs=[
            pl.BlockSpec((b, d), lambda j: (0, 0)),
            pl.BlockSpec((d, d), lambda j: (0, j)),
            pl.BlockSpec((1, d), lambda j: (0, j)),
        ],
        out_specs=pl.BlockSpec((None, b, d), lambda j: (j, 0, 0)),
        out_shape=jax.ShapeDtypeStruct((n // d, b, d), jnp.float32),
        compiler_params=pltpu.CompilerParams(dimension_semantics=("arbitrary",)),
        name="ada_modulation",
    )(c, w_ada, b_ada.reshape(1, n))


def _group_mean_sq(v):
    sq = v * v
    lower = lax.broadcasted_iota(jnp.int32, (1, 2 * HEAD_DIM), 1) < HEAD_DIM
    sums = []
    for c in range(v.shape[1] // (2 * HEAD_DIM)):
        pair = sq[:, c * 2 * HEAD_DIM:(c + 1) * 2 * HEAD_DIM]
        first = jnp.sum(jnp.where(lower, pair, 0.0), axis=-1, keepdims=True)
        second = jnp.sum(jnp.where(lower, 0.0, pair), axis=-1, keepdims=True)
        sums.append(jnp.where(lower, first, second))
    return jnp.concatenate(sums, axis=1) * (1.0 / HEAD_DIM)


def _mixer_kernel(sink_ref,
                  x_ref, xp_ref, xn_ref,
                  mod_ref,
                  g1n_ref, win_hbm, convw_ref, qg_ref, kg_ref, wo_hbm,
                  o_ref,
                  win_ref,
                  wo_ref,
                  wvt_ref,
                  h_ref,
                  hc_ref,
                  k2_ref,
                  vt_ref,
                  kc_ref,
                  vc_ref,
                  qn_ref,
                  ya_ref,
                  sgb_ref,
                  ybt_ref,
                  mrg_ref,
                  bias_ref):
    t = x_ref.shape[0]
    i = pl.program_id(1)
    n_tiles = pl.num_programs(1)
    first = jnp.logical_and(pl.program_id(0) == 0, i == 0)
    bi = pl.program_id(0)
    shift1 = mod_ref[0, pl.ds(bi, 1), :]
    scale1 = mod_ref[1, pl.ds(bi, 1), :]
    gate1 = mod_ref[2, pl.ds(bi, 1), :]
    gain = g1n_ref[...]
    lower = lax.broadcasted_iota(jnp.int32, (1, 2 * HEAD_DIM), 1) < HEAD_DIM

    def project_kv(h_rows):
        k = jnp.dot(h_rows, win_ref[:, OFF_K:OFF_K + KV_W], preferred_element_type=jnp.float32)
        vt = lax.dot_general(wvt_ref[...], h_rows, (((1,), (1,)), ((), ())),
                             preferred_element_type=jnp.float32)
        return k, vt

    def store_kv(k, vt, k_dst, v_dst, r0):
        rows = k.shape[0]
        kn = k * lax.rsqrt(_group_mean_sq(k) + EPS) * kg_ref[...]
        for pair in range(N_KV_HEADS // 2):
            tile = kn[:, pair * 2 * HEAD_DIM:(pair + 1) * 2 * HEAD_DIM]
            swapped = pltpu.roll(tile, HEAD_DIM, 1)
            k_dst[2 * pair, r0:r0 + rows, :] = jnp.where(lower, tile, swapped).astype(jnp.bfloat16)
            k_dst[2 * pair + 1, r0:r0 + rows, :] = jnp.where(lower, swapped, tile).astype(jnp.bfloat16)
        for g in range(N_KV_HEADS):
            v_dst[g, 0:HEAD_DIM, r0:r0 + rows] = (
                vt[g * HEAD_DIM:(g + 1) * HEAD_DIM, :].astype(jnp.bfloat16))

    @pl.when(first)
    def _():
        _stage_weights(win_hbm, win_ref)
        _stage_weights(wo_hbm, wo_ref)
        wvt_ref[...] = win_ref[:, OFF_V:OFF_V + KV_W].astype(jnp.float32).T.astype(jnp.bfloat16)
        kj = lax.broadcasted_iota(jnp.int32, (BLOCK, BLOCK), 0)
        qi = lax.broadcasted_iota(jnp.int32, (BLOCK, BLOCK), 1)
        for blk in range(3):
            dist = jnp.abs(qi + BLOCK - (kj + blk * BLOCK))
            distf = dist.astype(jnp.float32)
            for h in range(N_Q_HEADS):
                bias_ref[3 * h + blk] = jnp.where(dist <= WINDOW,
                                                  -(ALIBI_SLOPES[h] * LOG2_E) * distf, NEG_INF)
        bias_ref[3 * N_Q_HEADS] = jnp.full((BLOCK, BLOCK), NEG_INF, jnp.float32)
        def ones_rows(lanes):
            rows = lax.broadcasted_iota(jnp.int32, (VT_ROWS - HEAD_DIM, lanes), 0)
            return jnp.where(rows == 0, 1.0, 0.0).astype(jnp.bfloat16)

        for g in range(N_KV_HEADS):
            vt_ref[g, HEAD_DIM:, :] = ones_rows(t + 2 * BLOCK)
            vc_ref[g, HEAD_DIM:, :] = ones_rows(2 * BLOCK)
        h0 = _rms_mod(x_ref[0:BLOCK, :], gain, scale1, shift1).astype(jnp.bfloat16)
        store_kv(*project_kv(h0), kc_ref, vc_ref, BLOCK)
        zero_k = jnp.zeros((BLOCK, 2 * HEAD_DIM), jnp.bfloat16)
        zero_v = jnp.zeros((HEAD_DIM, BLOCK), jnp.bfloat16)
        for g in range(N_KV_HEADS):
            kc_ref[g, 0:BLOCK, :] = zero_k
            vc_ref[g, 0:HEAD_DIM, 0:BLOCK] = zero_v

    h_tile = _rms_mod(x_ref[...], gain, scale1, shift1)
    h_ref[0:t, :] = h_tile.astype(jnp.bfloat16)
    ht = h_ref[0:t, :]
    q = jnp.dot(ht, win_ref[:, OFF_Q:OFF_Q + D_MODEL], preferred_element_type=jnp.float32)

    bn = jnp.minimum(bi + (i == n_tiles - 1).astype(jnp.int32), pl.num_programs(0) - 1)
    h_after = _rms_mod(xn_ref[...], gain, mod_ref[1, pl.ds(bn, 1), :], mod_ref[0, pl.ds(bn, 1), :])
    h_ref[t:, :] = h_after.astype(jnp.bfloat16)
    h_before = jnp.where(i > 0, _rms_mod(xp_ref[...], gain, scale1, shift1), 0.0)
    hc_ref[...] = jnp.concatenate(
        [h_before, h_tile, jnp.where(i < n_tiles - 1, h_after[:SUBLANES], 0.0)],
        axis=0).astype(jnp.bfloat16)

    for g in range(N_KV_HEADS):
        k2_ref[g, 0:2 * BLOCK, :] = kc_ref[g]
        vt_ref[g, :, 0:2 * BLOCK] = vc_ref[g]
    k, vt = project_kv(h_ref[BLOCK:, :])

    store_kv(k, vt, k2_ref, vt_ref, 2 * BLOCK)

    qscale = qg_ref[...] * (LOG2_E / math.sqrt(HEAD_DIM))
    for c in range(D_MODEL // KV_W):
        qc = q[:, c * KV_W:(c + 1) * KV_W]
        qn_ref[:, c * KV_W:(c + 1) * KV_W] = (
            qc * lax.rsqrt(_group_mean_sq(qc) + EPS) * qscale[:, c * KV_W:(c + 1) * KV_W]
        ).astype(jnp.bfloat16)

    def gate_a_block(c):
        cs = slice(c * SIDE_COLS, (c + 1) * SIDE_COLS)
        cb = jnp.dot(ht, win_ref[:, OFF_CB + c * SIDE_COLS:OFF_CB + (c + 1) * SIDE_COLS],
                     preferred_element_type=jnp.float32)
        ga = jnp.dot(ht, win_ref[:, OFF_GA + c * SIDE_COLS:OFF_GA + (c + 1) * SIDE_COLS],
                     preferred_element_type=jnp.float32)
        ya_ref[:, cs] = jax.nn.sigmoid(ga) * cb

    def conv_block(c):
        cs = slice(c * SIDE_COLS, (c + 1) * SIDE_COLS)
        hc = hc_ref[...]
        cc = jnp.dot(hc, win_ref[:, OFF_CC + c * SIDE_COLS:OFF_CC + (c + 1) * SIDE_COLS],
                     preferred_element_type=jnp.float32)
        cu = jnp.dot(hc, win_ref[:, OFF_CU + c * SIDE_COLS:OFF_CU + (c + 1) * SIDE_COLS],
                     preferred_element_type=jnp.float32)
        ya_ref[:, cs] = ya_ref[:, cs] * _dwconv3(cc * cu, convw_ref[:, cs])

    def gate_b_block(c):
        cs = slice(c * SIDE_COLS, (c + 1) * SIDE_COLS)
        gb = jnp.dot(ht, win_ref[:, OFF_GB + c * SIDE_COLS:OFF_GB + (c + 1) * SIDE_COLS],
                     preferred_element_type=jnp.float32)
        sgb_ref[:, cs] = jax.nn.sigmoid(gb)

    side_jobs = []
    for c in range(D_MODEL // SIDE_COLS):
        side_jobs.append(lambda c=c: gate_a_block(c))
        side_jobs.append(lambda c=c: conv_block(c))
        side_jobs.append(lambda c=c: gate_b_block(c))

    qlower = lax.broadcasted_iota(jnp.int32, (BLOCK, 2 * HEAD_DIM), 1) < HEAD_DIM
    zero_q = jnp.zeros((BLOCK, 2 * HEAD_DIM), jnp.bfloat16)
    mask_blk = 3 * N_Q_HEADS

    def scores_t(qb, g):
        r0 = qb * BLOCK
        kb = k2_ref[g, r0:r0 + 3 * BLOCK, :]
        qs = []
        for half in range(GROUP // 2):
            c0 = (g * GROUP + 2 * half) * HEAD_DIM
            pair = qn_ref[r0:r0 + BLOCK, c0:c0 + 2 * HEAD_DIM]
            qs.append(jnp.where(qlower, pair, zero_q))
            qs.append(jnp.where(qlower, zero_q, pair))
        q4 = jnp.concatenate(qs, axis=0)
        return lax.dot_general(kb, q4, (((1,), (1,)), ((), ())),
                               preferred_element_type=jnp.float32)

    def attend(qb, g, st):
        r0 = qb * BLOCK
        ps = []
        sink_terms = []
        for j in range(GROUP):
            hh = g * GROUP + j
            idx = [3 * hh, 3 * hh + 1, 3 * hh + 2]
            if qb == 0:
                idx[0] = jnp.where(i == 0, mask_blk, idx[0])
            if qb == t // BLOCK - 1:
                idx[2] = jnp.where(i == n_tiles - 1, mask_blk, idx[2])
            s = jnp.concatenate(
                [st[blk * BLOCK:(blk + 1) * BLOCK, j * BLOCK:(j + 1) * BLOCK] + bias_ref[idx[blk]]
                 for blk in range(3)], axis=0)
            sink = sink_ref[hh] * LOG2_E
            m = jnp.maximum(jnp.max(s, axis=0, keepdims=True), sink)
            ps.append(jnp.exp2(s - m).astype(jnp.bfloat16))
            sink_terms.append(jnp.exp2(sink - m))
        p4 = jnp.concatenate(ps, axis=1)
        ot = jnp.dot(vt_ref[g, :, r0:r0 + 3 * BLOCK], p4,
                     preferred_element_type=jnp.float32)
        den = ot[HEAD_DIM:HEAD_DIM + 1, :] + jnp.concatenate(sink_terms, axis=1)
        y = ot[0:HEAD_DIM, :] / den
        for j in range(GROUP):
            hh = g * GROUP + j
            ybt_ref[hh * HEAD_DIM:(hh + 1) * HEAD_DIM, r0:r0 + BLOCK] = y[:, j * BLOCK:(j + 1) * BLOCK]

    stages = [(qb, g) for qb in range(t // BLOCK) for g in range(N_KV_HEADS)]
    st_next = scores_t(*stages[0])
    for n, (qb, g) in enumerate(stages):
        st = st_next
        if n + 1 < len(stages):
            st_next = scores_t(*stages[n + 1])
        if n % 4 != 3 and side_jobs:
            side_jobs.pop(0)()
        attend(qb, g, st)
    for job in side_jobs:
        job()

    for g in range(N_KV_HEADS):
        kc_ref[g] = k2_ref[g, t:t + 2 * BLOCK, :]
        vc_ref[g] = vt_ref[g, :, t:t + 2 * BLOCK]

    yb = ybt_ref[...].T
    mrg_ref[...] = (ya_ref[...] + sgb_ref[...] * yb).astype(jnp.bfloat16)
    out = jnp.dot(mrg_ref[...], wo_ref[...], preferred_element_type=jnp.float32)
    o_ref[...] = x_ref[...] + gate1 * out


def _token_mixer(x, mod, norm_gain, w_in, conv_w, q_gain, k_gain, sink, w_o):
    b, s, d = x.shape
    t = SEQ_TILE
    assert s % t == 0 and t % BLOCK == 0
    n_tiles = s // t
    bpt = t // BLOCK
    rpt = t // SUBLANES
    n_blocks = s // BLOCK

    def after(bi, i):
        blk = jnp.minimum(bi * n_blocks + (i + 1) * bpt, b * n_blocks - 1)
        return blk // n_blocks, blk % n_blocks

    const = lambda shape: pl.BlockSpec(shape, lambda bi, i, *_: (0,) * len(shape),
                                       pipeline_mode=pl.Buffered(1))
    grid_spec = pltpu.PrefetchScalarGridSpec(
        num_scalar_prefetch=1,
        grid=(b, n_tiles),
        in_specs=[
            pl.BlockSpec((None, t, d), lambda bi, i, *_: (bi, i, 0)),
            pl.BlockSpec((None, SUBLANES, d), lambda bi, i, *_: (bi, jnp.maximum(i * rpt - 1, 0), 0)),
            pl.BlockSpec((None, BLOCK, d), lambda bi, i, *_: (*after(bi, i), 0)),
            const(mod.shape),
            const((1, d)),
            pl.BlockSpec(memory_space=pl.ANY),
            const((3, D_CONV)),
            const((1, d)),
            const((1, KV_W)),
            pl.BlockSpec(memory_space=pl.ANY),
        ],
        out_specs=pl.BlockSpec((None, t, d), lambda bi, i, *_: (bi, i, 0)),
        scratch_shapes=[
            pltpu.VMEM((d, D_IN), jnp.bfloat16),
            pltpu.VMEM((d, d), jnp.bfloat16),
            pltpu.VMEM((KV_W, d), jnp.bfloat16),
            pltpu.VMEM((t + BLOCK, d), jnp.bfloat16),
            pltpu.VMEM((t + 2 * SUBLANES, d), jnp.bfloat16),
            pltpu.VMEM((N_KV_HEADS, t + 2 * BLOCK, 2 * HEAD_DIM), jnp.bfloat16),
            pltpu.VMEM((N_KV_HEADS, VT_ROWS, t + 2 * BLOCK), jnp.bfloat16),
            pltpu.VMEM((N_KV_HEADS, 2 * BLOCK, 2 * HEAD_DIM), jnp.bfloat16),
            pltpu.VMEM((N_KV_HEADS, VT_ROWS, 2 * BLOCK), jnp.bfloat16),
            pltpu.VMEM((t, d), jnp.bfloat16),
            pltpu.VMEM((t, d), jnp.float32),
            pltpu.VMEM((t, d), jnp.float32),
            pltpu.VMEM((d, t), jnp.float32),
            pltpu.VMEM((t, d), jnp.bfloat16),
            pltpu.VMEM((3 * N_Q_HEADS + 1, BLOCK, BLOCK), jnp.float32),
        ],
    )
    return pl.pallas_call(
        _mixer_kernel,
        grid_spec=grid_spec,
        out_shape=jax.ShapeDtypeStruct((b, s, d), jnp.float32),
        compiler_params=pltpu.CompilerParams(
            dimension_semantics=("arbitrary", "arbitrary"),
            vmem_limit_bytes=VMEM_LIMIT_BYTES),
        name="token_mixer",
    )(sink, x, x, x, mod, norm_gain.reshape(1, d), w_in, conv_w,
      jnp.tile(q_gain, N_Q_HEADS).reshape(1, d), jnp.tile(k_gain, N_KV_HEADS).reshape(1, KV_W), w_o)


def _ffn_kernel(x_ref, xp_ref, xn_ref,
                mod_ref, g2n_ref, wup_hbm, fcw_ref, fcb_ref, wdn_hbm,
                o_ref,
                wup_ref,
                wdn_ref,
                ht_ref,
                he_ref,
                act_ref):
    t = x_ref.shape[0]
    i = pl.program_id(1)
    n_tiles = pl.num_programs(1)

    @pl.when(jnp.logical_and(pl.program_id(0) == 0, i == 0))
    def _():
        _stage_weights(wup_hbm, wup_ref)
        _stage_weights(wdn_hbm, wdn_ref)

    bi = pl.program_id(0)
    shift2 = mod_ref[3, pl.ds(bi, 1), :]
    scale2 = mod_ref[4, pl.ds(bi, 1), :]
    gate2 = mod_ref[5, pl.ds(bi, 1), :]
    gain = g2n_ref[...]

    h_prev = _rms_mod(xp_ref[...], gain, scale2, shift2)
    h_next = _rms_mod(xn_ref[...], gain, scale2, shift2)
    h_prev = jnp.where(i > 0, h_prev, 0.0)
    h_next = jnp.where(i < n_tiles - 1, h_next, 0.0)
    h_tile = _rms_mod(x_ref[...], gain, scale2, shift2)
    ht_ref[...] = h_tile.astype(jnp.bfloat16)
    he_ref[...] = jnp.concatenate([h_prev, h_tile, h_next], axis=0).astype(jnp.bfloat16)

    for c in range(D_FF // FF_CHUNK):
        c0 = c * FF_CHUNK
        a_ext = jnp.dot(he_ref[...], wup_ref[:, c0:c0 + FF_CHUNK],
                        preferred_element_type=jnp.float32)
        gbr = jnp.dot(ht_ref[...], wup_ref[:, D_FF + c0:D_FF + c0 + FF_CHUNK],
                      preferred_element_type=jnp.float32)
        a = _dwconv3(a_ext, fcw_ref[:, c0:c0 + FF_CHUNK]) + fcb_ref[:, c0:c0 + FF_CHUNK]
        gelu = 0.5 * a * (1.0 + jnp.tanh(math.sqrt(2.0 / math.pi) * (a + 0.044715 * (a * a * a))))
        act_ref[:, c0:c0 + FF_CHUNK] = (gelu * gbr).astype(jnp.bfloat16)

    y = jnp.dot(act_ref[...], wdn_ref[...], preferred_element_type=jnp.float32)
    o_ref[...] = x_ref[...] + gate2 * y


def _conv_ffn(x, mod, norm_gain, w_up, ffn_conv_w, ffn_conv_b, w_down):
    b, s, d = x.shape
    t = SEQ_TILE
    assert s % t == 0 and D_FF % FF_CHUNK == 0
    n_tiles = s // t
    rpt = t // SUBLANES
    n_groups = s // SUBLANES

    const = lambda shape: pl.BlockSpec(shape, lambda bi, i: (0,) * len(shape),
                                       pipeline_mode=pl.Buffered(1))
    return pl.pallas_call(
        _ffn_kernel,
        grid=(b, n_tiles),
        in_specs=[
            pl.BlockSpec((None, t, d), lambda bi, i: (bi, i, 0)),
            pl.BlockSpec((None, SUBLANES, d), lambda bi, i: (bi, jnp.maximum(i * rpt - 1, 0), 0)),
            pl.BlockSpec((None, SUBLANES, d),
                         lambda bi, i: (bi, jnp.minimum((i + 1) * rpt, n_groups - 1), 0)),
            const(mod.shape),
            const((1, d)),
            pl.BlockSpec(memory_space=pl.ANY),
            const((3, D_FF)),
            const((1, D_FF)),
            pl.BlockSpec(memory_space=pl.ANY),
        ],
        out_specs=pl.BlockSpec((None, t, d), lambda bi, i: (bi, i, 0)),
        out_shape=jax.ShapeDtypeStruct((b, s, d), jnp.float32),
        scratch_shapes=[
            pltpu.VMEM((d, 2 * D_FF), jnp.bfloat16),
            pltpu.VMEM((D_FF, d), jnp.bfloat16),
            pltpu.VMEM((t, d), jnp.bfloat16),
            pltpu.VMEM((t + 2 * SUBLANES, d), jnp.bfloat16),
            pltpu.VMEM((t, D_FF), jnp.bfloat16),
        ],
        compiler_params=pltpu.CompilerParams(
            dimension_semantics=("arbitrary", "arbitrary"),
            vmem_limit_bytes=VMEM_LIMIT_BYTES),
        name="conv_ffn",
    )(x, x, x, mod, norm_gain.reshape(1, d), w_up, ffn_conv_w, ffn_conv_b.reshape(1, D_FF), w_down)


def kernel(x, c, w_ada, b_ada, norm1_gain, w_in, conv_w, q_gain, k_gain, sink, w_o, norm2_gain,
           w_up, ffn_conv_w, ffn_conv_b, w_down):
    depth = w_ada.shape[0]
    b, s, d = x.shape
    for l in range(depth):
        mod = _modulation(c, w_ada[l], b_ada[l])
        x = _token_mixer(x, mod, norm1_gain[l], w_in[l], conv_w[l], q_gain[l], k_gain[l],
                         sink[l], w_o[l])
        x = _conv_ffn(x, mod, norm2_gain[l], w_up[l], ffn_conv_w[l], ffn_conv_b[l], w_down[l])
    return x
```
